```python
import math
import jax
import jax.numpy as jnp
from jax import lax
import numpy as np

D_MODEL = 1024
BATCH = 16
SEQ = 2048
DEPTH = 1

PLE_DIM = 256
MIX_WIDTH = D_MODEL
ATTN_WIDTH = MIX_WIDTH // 2
ATTN_HEAD_DIM = 64
ATTN_HEADS = ATTN_WIDTH // ATTN_HEAD_DIM
ATTN_PATTERNS = ((128, 1), (512, 4), (2048, 16))
ROPE_THETA = 500000.0
ROT_DIM = ATTN_HEAD_DIM // 4
MLSTM_WIDTH = MIX_WIDTH - ATTN_WIDTH
MLSTM_HEADS = 4
MLSTM_HEAD_DIM = MLSTM_WIDTH // MLSTM_HEADS
MLSTM_CHUNK = 64
MLSTM_CONV = 3
FFN_DIM = ((8 * D_MODEL // 3) + 255) // 256 * 256
FFN_CONV = 3
EPS = 1e-6
IN_WIDTHS = (ATTN_WIDTH,) * 3 + (MLSTM_WIDTH,) * 3 + (MLSTM_HEADS,) * 4
IN_DIM = sum(IN_WIDTHS)
IN_SPLITS = tuple(int(s) for s in np.cumsum(IN_WIDTHS)[:-1])

kernel_name = "hybrid_dilated_attn_mlstm_convffn_encoder"


def rmsnorm(x, g):
    xf = x.astype(jnp.float32)
    y = xf * lax.rsqrt(jnp.mean(jnp.square(xf), axis=-1, keepdims=True) + EPS)
    return (y * g.astype(jnp.float32)).astype(x.dtype)


def dwconv_centred(x, w, b):
    K = w.shape[0]
    S = x.shape[1]
    pad = K // 2
    xp = jnp.pad(x, ((0, 0), (pad, pad), (0, 0)))
    out = xp[:, 0:S, :] * w[0]
    for j in range(1, K):
        out = out + xp[:, j:j + S, :] * w[j]
    return out + b


def partial_rotary(t, positions):
    half = ROT_DIM // 2
    inv = jnp.power(ROPE_THETA, -jnp.arange(0, ROT_DIM, 2, dtype=jnp.float32) / ROT_DIM)
    ang = positions.astype(jnp.float32)[..., None] * inv
    cos = jnp.cos(ang)[:, :, None, :]
    sin = jnp.sin(ang)[:, :, None, :]
    t1 = t[..., :half]
    t2 = t[..., half:ROT_DIM]
    return jnp.concatenate([t1 * cos - t2 * sin, t2 * cos + t1 * sin, t[..., ROT_DIM:]], axis=-1)


def banded_attention(q, k, v, radius):
    B, H, R, M, hd = q.shape
    blk = radius
    nb = -(-M // blk)
    Mp = nb * blk
    qb = jnp.pad(q, ((0, 0),) * 3 + ((0, Mp - M), (0, 0))).reshape(B, H, R, nb, blk, hd)

    def windows(t):
        tp = jnp.pad(t, ((0, 0),) * 3 + ((blk, Mp - M + blk), (0, 0))).reshape(B, H, R, nb + 2, blk, hd)
        return jnp.concatenate([tp[:, :, :, j:j + nb] for j in range(3)], axis=-2)

    kw = windows(k)
    vw = windows(v)
    qpos = jnp.arange(Mp).reshape(nb, blk)
    kpos = (jnp.arange(nb)[:, None] - 1) * blk + jnp.arange(3 * blk)[None, :]
    valid = ((jnp.abs(qpos[:, :, None] - kpos[:, None, :]) <= radius)
             & (kpos[:, None, :] >= 0) & (kpos[:, None, :] < M))
    s = jnp.einsum('bhrnqd,bhrnkd->bhrnqk', qb, kw)
    s = jnp.where(valid, s, -jnp.inf)
    lse = jax.nn.logsumexp(s, axis=-1)
    o = jnp.einsum('bhrnqk,bhrnkd->bhrnqd', jnp.exp(s - lse[..., None]), vw)
    return o.reshape(B, H, R, Mp, hd)[:, :, :, :M], lse.reshape(B, H, R, Mp)[:, :, :, :M]


def dilated_branch(q, k, v, dil, n_side):
    B, H, S, hd = q.shape
    M = S // dil

    def to_res(t):
        return t.reshape(B, H, M, dil, hd).transpose(0, 1, 3, 2, 4)

    o, lse = banded_attention(to_res(q), to_res(k), to_res(v), n_side)
    return (o.transpose(0, 1, 3, 2, 4).reshape(B, H, S, hd),
            lse.transpose(0, 1, 3, 2).reshape(B, H, S))


def dilated_mixture_attention(q, k, v, positions):
    B, S, H, hd = q.shape
    q = partial_rotary(q, positions) * (hd ** -0.5)
    k = partial_rotary(k, positions)
    q, k, v = (t.transpose(0, 2, 1, 3) for t in (q, k, v))
    outs, lses = [], []
    for window, dil in ATTN_PATTERNS:
        o, l = dilated_branch(q, k, v, dil, (window // 2) // dil)
        outs.append(o)
        lses.append(l)
    wts = jax.nn.softmax(jnp.stack(lses, axis=0), axis=0)
    out = jnp.sum(wts[..., None] * jnp.stack(outs, axis=0), axis=0)
    return out.transpose(0, 2, 1, 3).reshape(B, S, H * hd)


def mlstm_scan(q, k, v, i_pre, f_pre):
    B, H, S, d = q.shape
    L = MLSTM_CHUNK
    nc = S // L

    def chunks(t):
        return jnp.moveaxis(t.reshape((B, H, nc, L) + t.shape[3:]), 2, 0)

    logf = jax.nn.log_sigmoid(f_pre)
    tril = jnp.tril(jnp.ones((L, L), dtype=bool))

    def step(carry, inp):
        C, n, m = carry
        qc, kc, vc, ic, lf = inp
        b = jnp.cumsum(lf, axis=-1)
        dmat = jnp.where(tril, b[..., :, None] - b[..., None, :] + ic[..., None, :], -jnp.inf)
        m_inter = b + m[..., None]
        m_t = jnp.maximum(m_inter, jnp.max(dmat, axis=-1))
        pmat = jnp.einsum('bhtd,bhsd->bhts', qc, kc) * jnp.exp(dmat - m_t[..., None])
        a = jnp.exp(m_inter - m_t)
        num = a[..., None] * jnp.einsum('bhtd,bhde->bhte', qc, C) + jnp.einsum('bhts,bhse->bhte', pmat, vc)
        den = a * jnp.einsum('bhtd,bhd->bht', qc, n) + jnp.sum(pmat, axis=-1)
        h = num / jnp.maximum(jnp.abs(den), jnp.exp(-m_t))[..., None]
        g = b[..., -1]
        logw = g[..., None] - b + ic
        m_new = jnp.maximum(g + m, jnp.max(logw, axis=-1))
        decay = jnp.exp(g + m - m_new)
        w = jnp.exp(logw - m_new[..., None])
        C = decay[..., None, None] * C + jnp.einsum('bhs,bhsd,bhse->bhde', w, kc, vc)
        n = decay[..., None] * n + jnp.einsum('bhs,bhsd->bhd', w, kc)
        return (C, n, m_new), h

    init = (jnp.zeros((B, H, d, d), jnp.float32), jnp.zeros((B, H, d), jnp.float32),
            jnp.zeros((B, H), jnp.float32))
    _, h = lax.scan(step, init, (chunks(q), chunks(k), chunks(v), chunks(i_pre), chunks(logf)))
    return jnp.moveaxis(h, 0, 2).reshape(B, H, S, d)


def mlstm_mixer(u, v, o, i_f, i_b, f_f, f_b, conv_w, conv_b, w_q, w_k, b_ig, b_fg, gn_g):
    B, S, _ = u.shape
    H, d = MLSTM_HEADS, MLSTM_HEAD_DIM
    uc = jax.nn.silu(dwconv_centred(u, conv_w, conv_b)).reshape(B, S, H, d)
    q = jnp.einsum('bshd,hde->bhse', uc, w_q)
    k = jnp.einsum('bshd,hde->bhse', uc, w_k) * (d ** -0.5)
    vh = v.reshape(B, S, H, d).transpose(0, 2, 1, 3)

    def gate(g, b):
        return (g + b).transpose(0, 2, 1)

    def flip(t):
        return jnp.flip(t, axis=2)

    h_fwd = mlstm_scan(q, k, vh, gate(i_f, b_ig[0]), gate(f_f, b_fg[0]))
    h_bwd = flip(mlstm_scan(flip(q), flip(k), flip(vh), flip(gate(i_b, b_ig[1])), flip(gate(f_b, b_fg[1]))))
    hs = h_fwd + h_bwd
    mu = jnp.mean(hs, axis=-1, keepdims=True)
    var = jnp.mean(jnp.square(hs - mu), axis=-1, keepdims=True)
    hn = ((hs - mu) * lax.rsqrt(var + EPS)).transpose(0, 2, 1, 3).reshape(B, S, H * d)
    return hn * gn_g * jax.nn.sigmoid(o)


def conv_glu_ffn(h, w_gate, conv_w, conv_b, w_up, w_down):
    a = dwconv_centred(h @ w_gate, conv_w, conv_b)
    return (jax.nn.gelu(a, approximate=False) * (h @ w_up)) @ w_down


def setup_inputs(seed: int = 0) -> dict:
    key = jax.random.key(seed)
    ks = jax.random.split(key, 32)

    def nrm(k, shape, scale):
        return jax.random.normal(k, shape, jnp.float32) * scale

    def gain(k, shape):
        return 1.0 + nrm(k, shape, 0.02)

    H = MLSTM_HEADS
    fg_bias = jnp.broadcast_to(jnp.linspace(3.0, 6.0, H, dtype=jnp.float32), (DEPTH, 2, H))
    return {
        'x': nrm(ks[0], (BATCH, SEQ, D_MODEL), 1.0),
        'p': nrm(ks[1], (DEPTH, BATCH, SEQ, PLE_DIM), 1.0),
        'positions': jnp.broadcast_to(jnp.arange(SEQ, dtype=jnp.int32), (BATCH, SEQ)),
        'ln_mix_g': gain(ks[2], (DEPTH, D_MODEL)),
        'w_in': nrm(ks[3], (DEPTH, D_MODEL, IN_DIM), D_MODEL ** -0.5),
        'mlstm_conv_w': nrm(ks[4], (DEPTH, MLSTM_CONV, MLSTM_WIDTH), MLSTM_CONV ** -0.5),
        'mlstm_conv_b': nrm(ks[5], (DEPTH, MLSTM_WIDTH), 0.02),
        'w_mq': nrm(ks[6], (DEPTH, H, MLSTM_HEAD_DIM, MLSTM_HEAD_DIM), MLSTM_HEAD_DIM ** -0.5),
        'w_mk': nrm(ks[7], (DEPTH, H, MLSTM_HEAD_DIM, MLSTM_HEAD_DIM), MLSTM_HEAD_DIM ** -0.5),
        'b_igate': nrm(ks[8], (DEPTH, 2, H), 0.1),
        'b_fgate': fg_bias + nrm(ks[9], (DEPTH, 2, H), 0.1),
        'mlstm_gn_g': gain(ks[10], (DEPTH, MLSTM_WIDTH)),
        'w_out': nrm(ks[11], (DEPTH, MIX_WIDTH, D_MODEL), MIX_WIDTH ** -0.5),
        'ln_ffn_g': gain(ks[12], (DEPTH, D_MODEL)),
        'w_ffn_gate': nrm(ks[13], (DEPTH, D_MODEL, FFN_DIM), D_MODEL ** -0.5),
        'ffn_conv_w': nrm(ks[14], (DEPTH, FFN_CONV, FFN_DIM), FFN_CONV ** -0.5),
        'ffn_conv_b': nrm(ks[15], (DEPTH, FFN_DIM), 0.02),
        'w_ffn_up': nrm(ks[16], (DEPTH, D_MODEL, FFN_DIM), D_MODEL ** -0.5),
        'w_ffn_down': nrm(ks[17], (DEPTH, FFN_DIM, D_MODEL), FFN_DIM ** -0.5),
        'ln_ple_g': gain(ks[18], (DEPTH, D_MODEL)),
        'w_ple_gate': nrm(ks[19], (DEPTH, D_MODEL, D_MODEL), D_MODEL ** -0.5),
        'b_ple_gate': nrm(ks[20], (DEPTH, D_MODEL), 0.02),
        'w_ple_proj': nrm(ks[21], (DEPTH, PLE_DIM, D_MODEL), PLE_DIM ** -0.5),
        'ln_final_g': gain(ks[22], (D_MODEL,)),
    }


def reference(x, p, positions, ln_mix_g, w_in, mlstm_conv_w, mlstm_conv_b, w_mq, w_mk,
              b_igate, b_fgate, mlstm_gn_g, w_out, ln_ffn_g, w_ffn_gate, ffn_conv_w,
              ffn_conv_b, w_ffn_up, w_ffn_down, ln_ple_g, w_ple_gate, b_ple_gate,
              w_ple_proj, ln_final_g):
    B, S, _ = x.shape
    f32 = jnp.float32
    for i in range(DEPTH):
        h = rmsnorm(x, ln_mix_g[i])
        z = (h @ w_in[i]).astype(f32)
        q_a, k_a, v_a, u_m, v_m, o_m, i_f, i_b, f_f, f_b = jnp.split(z, IN_SPLITS, axis=-1)
        attn = dilated_mixture_attention(
            q_a.reshape(B, S, ATTN_HEADS, ATTN_HEAD_DIM),
            k_a.reshape(B, S, ATTN_HEADS, ATTN_HEAD_DIM),
            v_a.reshape(B, S, ATTN_HEADS, ATTN_HEAD_DIM), positions)
        mem = mlstm_mixer(u_m, v_m, o_m, i_f, i_b, f_f, f_b,
                          mlstm_conv_w[i].astype(f32), mlstm_conv_b[i].astype(f32),
                          w_mq[i].astype(f32), w_mk[i].astype(f32),
                          b_igate[i].astype(f32), b_fgate[i].astype(f32),
                          mlstm_gn_g[i].astype(f32))
        mixed = jnp.concatenate([attn, mem], axis=-1).astype(x.dtype)
        x = x + mixed @ w_out[i]
        h = rmsnorm(x, ln_ffn_g[i])
        x = x + conv_glu_ffn(h, w_ffn_gate[i], ffn_conv_w[i], ffn_conv_b[i], w_ffn_up[i], w_ffn_down[i])
        h = rmsnorm(x, ln_ple_g[i])
        x = x + (p[i] @ w_ple_proj[i]) * jax.nn.sigmoid(h @ w_ple_gate[i] + b_ple_gate[i])
    return rmsnorm(x, ln_final_g)
```

```python
import functools
import math

import jax
import jax.numpy as jnp
from jax import lax
from jax.experimental import pallas as pl
from jax.experimental.pallas import tpu as pltpu

F32 = jnp.float32
BF16 = jnp.bfloat16

EPS = 1e-6
ATTN_HEAD_DIM = 64
ATTN_PATTERNS = ((128, 1), (512, 4), (2048, 16))
ATTN_RADIUS = 64
ROPE_THETA = 500000.0
ROT_DIM = ATTN_HEAD_DIM // 4
MLSTM_HEADS = 4
MLSTM_HEAD_DIM = 128
LANES = 128
ATT_BLK = 128
MLSTM_CHUNK = 128
VMEM_LIMIT = 56 * 1024 * 1024

_NEG = float("-inf")


def _rms(x, g):
    ms = jnp.sum(x * x, axis=-1, keepdims=True) * (1.0 / x.shape[-1])
    return x * lax.rsqrt(ms + EPS) * g


def _const_spec(shape):
    nd = len(shape)
    return pl.BlockSpec(shape, lambda *_: (0,) * nd, pipeline_mode=pl.Buffered(1))


def _inproj_kernel(x_ref, pos_ref, g_ref, invf_ref, sgn_ref, wqkv_ref, wuvo_ref, wgt_ref,
                   q_ref, k_ref, v_ref, u_ref, vm_ref, o_ref, gate_ref):
    h = _rms(x_ref[...], g_ref[...]).astype(BF16)
    ang = pos_ref[...].astype(F32) * invf_ref[...]
    cos = jnp.cos(ang)
    sin = jnp.sin(ang) * sgn_ref[...]
    lane = lax.broadcasted_iota(jnp.int32, (1, LANES), 1)
    first_half = (lane % ATTN_HEAD_DIM) < (ROT_DIM // 2)

    def rotary(z):
        up = pltpu.roll(z, LANES - ROT_DIM // 2, 1)
        dn = pltpu.roll(z, ROT_DIM // 2, 1)
        return z * cos + jnp.where(first_half, up, dn) * sin

    zqkv = jnp.dot(h, wqkv_ref[...], preferred_element_type=F32)
    aw = zqkv.shape[1] // 3
    for j in range(aw // LANES):
        sl = slice(j * LANES, (j + 1) * LANES)
        q_ref[:, sl] = rotary(zqkv[:, j * LANES:(j + 1) * LANES]) * (ATTN_HEAD_DIM ** -0.5)
        k_ref[:, sl] = rotary(zqkv[:, aw + j * LANES:aw + (j + 1) * LANES])
    v_ref[...] = zqkv[:, 2 * aw:]
    zuvo = jnp.dot(h, wuvo_ref[...], preferred_element_type=F32)
    mw = zuvo.shape[1] // 3
    u_ref[...] = zuvo[:, :mw]
    vm_ref[...] = zuvo[:, mw:2 * mw].astype(BF16)
    o_ref[...] = zuvo[:, 2 * mw:]
    gate_ref[...] = jnp.dot(h, wgt_ref[...], preferred_element_type=F32)


def _inproj(x2, pos2, g, invf, sgn, wqkv, wuvo, wgt, tm):
    T, D = x2.shape
    aw = wqkv.shape[1] // 3
    mw = wuvo.shape[1] // 3
    row = lambda w: pl.BlockSpec((tm, w), lambda i: (i, 0))
    return pl.pallas_call(
        _inproj_kernel,
        grid=(T // tm,),
        in_specs=[row(D), row(1), _const_spec((1, D)), _const_spec((1, LANES)), _const_spec((1, LANES)),
                  _const_spec(wqkv.shape), _const_spec(wuvo.shape), _const_spec(wgt.shape)],
        out_specs=[row(aw), row(aw), row(aw), row(mw), row(mw), row(mw), row(LANES)],
        out_shape=[jax.ShapeDtypeStruct((T, aw), F32)] * 3
        + [jax.ShapeDtypeStruct((T, mw), F32), jax.ShapeDtypeStruct((T, mw), BF16),
           jax.ShapeDtypeStruct((T, mw), F32), jax.ShapeDtypeStruct((T, LANES), F32)],
        compiler_params=pltpu.CompilerParams(dimension_semantics=("arbitrary",), vmem_limit_bytes=VMEM_LIMIT),
        name="inproj",
    )(x2, pos2, g, invf, sgn, wqkv, wuvo, wgt)


def _attn_bias():
    W = 2 * ATT_BLK
    qi = jnp.arange(ATT_BLK)[:, None]
    kj = jnp.arange(W)[None, :]
    band = (kj - qi >= 0) & (kj - qi <= 2 * ATTN_RADIUS)
    tiles = [band & (kj >= ATTN_RADIUS), band, band & (kj < W - ATTN_RADIUS),
             (jnp.abs(kj - qi) <= ATTN_RADIUS) & (kj < ATT_BLK)]
    return jnp.where(jnp.stack(tiles), 0.0, _NEG).astype(F32)


def _attn_kernel(q_ref, k_ref, v_ref, bias_ref, out_ref, qc_ref, kp_ref, vp_ref, o_scr, l_scr):
    S = q_ref.shape[0]
    PAD = ATTN_RADIUS
    lane = lax.broadcasted_iota(jnp.int32, (1, LANES), 1)
    head0 = lane < ATTN_HEAD_DIM
    zpad = jnp.zeros((PAD, LANES), BF16)
    kp_ref[0:PAD, :] = zpad
    vp_ref[0:PAD, :] = zpad

    def softmax_block(qb, kw, vw, bias):
        outs, lses = [], []
        for hh in range(2):
            mask = head0 if hh == 0 else jnp.logical_not(head0)
            qm = jnp.where(mask, qb, jnp.zeros_like(qb))
            s = lax.dot_general(qm, kw, (((1,), (1,)), ((), ())), preferred_element_type=F32) + bias
            m = jnp.max(s, axis=1, keepdims=True)
            p = jnp.exp(s - m)
            l = jnp.sum(p, axis=1, keepdims=True)
            o = jnp.dot(p.astype(BF16), vw, preferred_element_type=F32)
            outs.append(o / l)
            lses.append(m + jnp.log(l))
        return jnp.where(head0, outs[0], outs[1]), jnp.where(head0, lses[0], lses[1])

    for pi, (_, d) in enumerate(ATTN_PATTERNS):
        M = S // d
        nb = M // ATT_BLK
        if nb > 1:
            kp_ref[PAD + M:2 * PAD + M, :] = zpad
            vp_ref[PAD + M:2 * PAD + M, :] = zpad

        def class_body(r, carry, d=d, M=M, nb=nb, pi=pi):
            qc_ref[0:M, :] = q_ref[pl.ds(r, M, stride=d), :].astype(BF16)
            kp_ref[PAD:PAD + M, :] = k_ref[pl.ds(r, M, stride=d), :].astype(BF16)
            vp_ref[PAD:PAD + M, :] = v_ref[pl.ds(r, M, stride=d), :].astype(BF16)
            if nb == 1:
                o, lse = softmax_block(qc_ref[0:M, :], kp_ref[PAD:PAD + M, :], vp_ref[PAD:PAD + M, :],
                                       bias_ref[3][:, :ATT_BLK])
                o_scr[pi, pl.ds(r, M, stride=d), :] = o
                l_scr[pi, pl.ds(r, M, stride=d), :] = lse
                return carry

            def blk_body(i, c2):
                row0 = pl.multiple_of(i * ATT_BLK, ATT_BLK)
                sel = jnp.where(i == 0, 0, jnp.where(i == nb - 1, 2, 1))
                o, lse = softmax_block(qc_ref[pl.ds(row0, ATT_BLK), :], kp_ref[pl.ds(row0, 2 * ATT_BLK), :],
                                       vp_ref[pl.ds(row0, 2 * ATT_BLK), :], bias_ref[sel])
                tok0 = r + d * row0
                o_scr[pi, pl.ds(tok0, ATT_BLK, stride=d), :] = o
                l_scr[pi, pl.ds(tok0, ATT_BLK, stride=d), :] = lse
                return c2

            return lax.fori_loop(0, nb, blk_body, carry)

        lax.fori_loop(0, d, class_body, 0)

    lse = l_scr[...]
    mx = jnp.max(lse, axis=0)
    w = jnp.exp(lse - mx[None])
    out = jnp.sum(w * o_scr[...], axis=0) / jnp.sum(w, axis=0)
    out_ref[...] = out.astype(out_ref.dtype)


def _attn(q, k, v, bias):
    B, S, AW = q.shape
    blk = pl.BlockSpec((None, S, LANES), lambda b, j: (b, 0, j))
    npat = len(ATTN_PATTERNS)
    return pl.pallas_call(
        _attn_kernel,
        grid=(B, AW // LANES),
        in_specs=[blk, blk, blk, _const_spec(bias.shape)],
        out_specs=blk,
        out_shape=jax.ShapeDtypeStruct((B, S, AW), BF16),
        scratch_shapes=[pltpu.VMEM((S, LANES), BF16), pltpu.VMEM((S + 2 * ATTN_RADIUS, LANES), BF16),
                        pltpu.VMEM((S + 2 * ATTN_RADIUS, LANES), BF16),
                        pltpu.VMEM((npat, S, LANES), F32), pltpu.VMEM((npat, S, LANES), F32)],
        compiler_params=pltpu.CompilerParams(dimension_semantics=("arbitrary", "arbitrary"),
                                             vmem_limit_bytes=VMEM_LIMIT),
        name="dilated_attn",
    )(q, k, v, bias)


def _split3(x):
    hi = x.astype(BF16)
    r1 = x - hi.astype(F32)
    mid = r1.astype(BF16)
    lo = (r1 - mid.astype(F32)).astype(BF16)
    return hi, mid, lo


def _tri_dot_left(tri, x):
    return sum(jnp.dot(tri, part, preferred_element_type=F32) for part in _split3(x))


def _tri_dot_right(x, tri):
    return sum(jnp.dot(part, tri, preferred_element_type=F32) for part in _split3(x))


def _log_sigmoid(x):
    return jnp.minimum(x, 0.0) - jnp.log1p(jnp.exp(-jnp.abs(x)))


def _mlstm_kernel(bias_ref, u_ref, vm_ref, o_ref, gcol_ref, grow_ref, cw_ref, cb_ref, wq_ref, wk_ref, gn_ref,
                  out_ref, q_scr, k_scr, hs_scr, lfb_scr, ib_scr, brow_scr, irow_scr):
    S, dh = u_ref.shape
    L = MLSTM_CHUNK
    nc = S // L
    hd = pl.program_id(1)

    u = u_ref[...]
    rows = lax.broadcasted_iota(jnp.int32, (S, 1), 0)
    u_prev = jnp.where(rows == 0, 0.0, pltpu.roll(u, 1, 0))
    u_next = jnp.where(rows == S - 1, 0.0, pltpu.roll(u, S - 1, 0))
    cw = cw_ref[...]
    a = u_prev * cw[0:1] + u * cw[1:2] + u_next * cw[2:3] + cb_ref[...]
    uc = (a * jax.nn.sigmoid(a)).astype(BF16)
    q_scr[...] = jnp.dot(uc, wq_ref[...], preferred_element_type=F32).astype(BF16)
    k_scr[...] = (jnp.dot(uc, wk_ref[...], preferred_element_type=F32) * (dh ** -0.5)).astype(BF16)
    hs_scr[...] = jnp.zeros_like(hs_scr)

    ri = lax.broadcasted_iota(jnp.int32, (L, L), 0)
    ci = lax.broadcasted_iota(jnp.int32, (L, L), 1)
    lower = ri >= ci
    tril = lower.astype(BF16)
    triu = (ri <= ci).astype(BF16)

    for di in range(2):
        b_i = bias_ref[0, di, hd]
        b_f = bias_ref[1, di, hd]
        ib_scr[di] = jnp.broadcast_to(gcol_ref[:, di:di + 1] + b_i, (S, dh))
        lfb_scr[di] = jnp.broadcast_to(_log_sigmoid(gcol_ref[:, 2 + di:3 + di] + b_f), (S, dh))
        irow_scr[di] = grow_ref[di] + b_i
        lf_rows = _log_sigmoid(grow_ref[2 + di] + b_f)
        brow_scr[di] = _tri_dot_right(lf_rows, triu if di == 0 else tril)

    def chunk_step(di, c, state):
        C, n, m = state
        r0 = pl.multiple_of(c * L, L)
        qc = q_scr[pl.ds(r0, L), :]
        kc = k_scr[pl.ds(r0, L), :]
        vc = vm_ref[pl.ds(r0, L), :]
        lfb = lfb_scr[di, pl.ds(r0, L), :]
        icol = ib_scr[di, pl.ds(r0, L), :]
        bcol = _tri_dot_left(tril if di == 0 else triu, lfb)
        brow = brow_scr[di, pl.ds(c, 1), :]
        irow = irow_scr[di, pl.ds(c, 1), :]
        causal = lower if di == 0 else jnp.logical_not(ri > ci)
        dmat = jnp.where(causal, bcol - brow + irow, _NEG)
        m_inter = bcol + m
        m_t = jnp.maximum(m_inter, jnp.max(dmat, axis=1, keepdims=True))
        qk = lax.dot_general(qc, kc, (((1,), (1,)), ((), ())), preferred_element_type=F32)
        pmat = qk * jnp.exp(dmat - m_t)
        a_in = jnp.exp(m_inter - m_t)
        num = a_in * jnp.dot(qc, C.astype(BF16), preferred_element_type=F32) \
            + jnp.dot(pmat.astype(BF16), vc, preferred_element_type=F32)
        den = a_in * jnp.sum(qc.astype(F32) * n, axis=1, keepdims=True) + jnp.sum(pmat, axis=1, keepdims=True)
        hval = num / jnp.maximum(jnp.abs(den), jnp.exp(-m_t))
        hs_scr[pl.ds(r0, L), :] += hval
        g = bcol[L - 1:L, :] if di == 0 else bcol[0:1, :]
        logw = g - bcol + icol
        m_new = jnp.maximum(g + m, jnp.max(logw, axis=0, keepdims=True))
        decay = jnp.exp(g + m - m_new)
        kw = kc.astype(F32) * jnp.exp(logw - m_new)
        C_new = decay * C + lax.dot_general(kw.astype(BF16), vc, (((0,), (0,)), ((), ())),
                                            preferred_element_type=F32)
        n_new = decay * n + jnp.sum(kw, axis=0, keepdims=True)
        return C_new, n_new, m_new

    def loop_body(c, states):
        sf, sb = states
        return chunk_step(0, c, sf), chunk_step(1, nc - 1 - c, sb)

    init = (jnp.zeros((dh, dh), F32), jnp.zeros((1, dh), F32), jnp.zeros((1, dh), F32))
    lax.fori_loop(0, nc, loop_body, (init, init))

    hs = hs_scr[...]
    mu = jnp.sum(hs, axis=1, keepdims=True) * (1.0 / dh)
    cen = hs - mu
    var = jnp.sum(cen * cen, axis=1, keepdims=True) * (1.0 / dh)
    hn = cen * lax.rsqrt(var + EPS)
    out_ref[...] = (hn * gn_ref[...] * jax.nn.sigmoid(o_ref[...])).astype(out_ref.dtype)


def _mlstm(gate_bias, u, vm, o, gcol, grow, cw, cb, wq, wk, gn):
    B, S, MW = u.shape
    H, dh = MLSTM_HEADS, MLSTM_HEAD_DIM
    L = MLSTM_CHUNK
    nc = S // L
    tok = pl.BlockSpec((None, S, dh), lambda b, h: (b, 0, h))
    return pl.pallas_call(
        _mlstm_kernel,
        grid=(B, H),
        in_specs=[pl.BlockSpec(memory_space=pltpu.SMEM), tok, tok, tok,
                  pl.BlockSpec((None, None, S, 4), lambda b, h: (b, h, 0, 0)),
                  pl.BlockSpec((None, None, 4, nc, L), lambda b, h: (b, h, 0, 0, 0)),
                  pl.BlockSpec((3, dh), lambda b, h: (0, h)), pl.BlockSpec((1, dh), lambda b, h: (0, h)),
                  pl.BlockSpec((None, dh, dh), lambda b, h: (h, 0, 0)),
                  pl.BlockSpec((None, dh, dh), lambda b, h: (h, 0, 0)),
                  pl.BlockSpec((1, dh), lambda b, h: (0, h))],
        out_specs=tok,
        out_shape=jax.ShapeDtypeStruct((B, S, MW), BF16),
        scratch_shapes=[pltpu.VMEM((S, dh), BF16), pltpu.VMEM((S, dh), BF16), pltpu.VMEM((S, dh), F32),
                        pltpu.VMEM((2, S, dh), F32), pltpu.VMEM((2, S, dh), F32),
                        pltpu.VMEM((2, nc, L), F32), pltpu.VMEM((2, nc, L), F32)],
        compiler_params=pltpu.CompilerParams(dimension_semantics=("arbitrary", "arbitrary"),
                                             vmem_limit_bytes=VMEM_LIMIT),
        name="mlstm",
    )(gate_bias, u, vm, o, gcol, grow, cw, cb, wq, wk, gn)


def _outproj_kernel(x_ref, a_ref, m_ref, wa_ref, wm_ref, g_ref, x1_ref, h_ref):
    x1 = x_ref[...] + jnp.dot(a_ref[...], wa_ref[...], preferred_element_type=F32) \
        + jnp.dot(m_ref[...], wm_ref[...], preferred_element_type=F32)
    x1_ref[...] = x1
    h_ref[...] = _rms(x1, g_ref[...]).astype(BF16)


def _outproj(x2, attn, mem, wa, wm, g, tm):
    T, D = x2.shape
    row = lambda w: pl.BlockSpec((tm, w), lambda i: (i, 0))
    return pl.pallas_call(
        _outproj_kernel,
        grid=(T // tm,),
        in_specs=[row(D), row(attn.shape[1]), row(mem.shape[1]), _const_spec(wa.shape), _const_spec(wm.shape),
                  _const_spec((1, D))],
        out_specs=[row(D), row(D)],
        out_shape=[jax.ShapeDtypeStruct((T, D), F32), jax.ShapeDtypeStruct((T, D), BF16)],
        compiler_params=pltpu.CompilerParams(dimension_semantics=("arbitrary",), vmem_limit_bytes=VMEM_LIMIT),
        name="outproj",
    )(x2, attn, mem, wa, wm, g)


HALO = 16


def _ffn_kernel(x1_ref, h_ref, hp_ref, hn_ref, p_ref, wg_ref, wu_ref, wd_ref, cw_ref, cb_ref,
                gp_ref, wpg_ref, bpg_ref, wpp_ref, gf_ref, out_ref, hcat_ref, a_scr, acc_ref, *, tiles_per_seq):
    tm = h_ref.shape[0]
    i = pl.program_id(0)
    first = (i % tiles_per_seq) == 0
    last = (i % tiles_per_seq) == tiles_per_seq - 1
    hcat_ref[0:HALO, :] = jnp.where(first, jnp.zeros_like(hp_ref[...]), hp_ref[...])
    hcat_ref[HALO:HALO + tm, :] = h_ref[...]
    hcat_ref[HALO + tm:, :] = jnp.where(last, jnp.zeros_like(hn_ref[...]), hn_ref[...])
    acc_ref[...] = jnp.zeros_like(acc_ref)

    def chunk(c, carry):
        a_scr[...] = jnp.dot(hcat_ref[...], wg_ref[c], preferred_element_type=F32)
        cw = cw_ref[c]
        a = a_scr[pl.ds(HALO - 1, tm), :] * cw[0:1] + a_scr[pl.ds(HALO, tm), :] * cw[1:2] \
            + a_scr[pl.ds(HALO + 1, tm), :] * cw[2:3] + cb_ref[c]
        up = jnp.dot(h_ref[...], wu_ref[c], preferred_element_type=F32)
        act = 0.5 * a * (1.0 + lax.erf(a * (2.0 ** -0.5))) * up
        acc_ref[...] += jnp.dot(act.astype(BF16), wd_ref[c], preferred_element_type=F32)
        return carry

    lax.fori_loop(0, wg_ref.shape[0], chunk, 0)
    x2 = x1_ref[...] + acc_ref[...]
    h3 = _rms(x2, gp_ref[...]).astype(BF16)
    gate = jax.nn.sigmoid(jnp.dot(h3, wpg_ref[...], preferred_element_type=F32) + bpg_ref[...])
    pp = jnp.dot(p_ref[...].astype(BF16), wpp_ref[...], preferred_element_type=F32)
    out_ref[...] = _rms(x2 + pp * gate, gf_ref[...])


def _ffn(x1, h2, p2, wg, wu, wd, cw, cb, gp, wpg, bpg, wpp, gf, tm, S):
    T, D = x1.shape
    nck, _, ck = wg.shape
    row = lambda w: pl.BlockSpec((tm, w), lambda i: (i, 0))
    hb = tm // HALO
    nhb = T // HALO
    prev = pl.BlockSpec((HALO, D), lambda i: (jnp.maximum(i * hb - 1, 0), 0))
    nxt = pl.BlockSpec((HALO, D), lambda i: (jnp.minimum((i + 1) * hb, nhb - 1), 0))
    return pl.pallas_call(
        functools.partial(_ffn_kernel, tiles_per_seq=S // tm),
        grid=(T // tm,),
        in_specs=[row(D), row(D), prev, nxt, row(p2.shape[1]),
                  _const_spec(wg.shape), _const_spec(wu.shape), _const_spec(wd.shape),
                  _const_spec(cw.shape), _const_spec(cb.shape), _const_spec((1, D)),
                  _const_spec(wpg.shape), _const_spec((1, D)), _const_spec(wpp.shape), _const_spec((1, D))],
        out_specs=row(D),
        out_shape=jax.ShapeDtypeStruct((T, D), F32),
        scratch_shapes=[pltpu.VMEM((tm + 2 * HALO, D), BF16), pltpu.VMEM((tm + 2 * HALO, ck), F32),
                        pltpu.VMEM((tm, D), F32)],
        compiler_params=pltpu.CompilerParams(dimension_semantics=("arbitrary",), vmem_limit_bytes=VMEM_LIMIT),
        name="ffn_ple",
    )(x1, h2, h2, h2, p2, wg, wu, wd, cw, cb, gp, wpg, bpg, wpp, gf)


def kernel(x, p, positions, ln_mix_g, w_in, mlstm_conv_w, mlstm_conv_b, w_mq, w_mk, b_igate, b_fgate,
           mlstm_gn_g, w_out, ln_ffn_g, w_ffn_gate, ffn_conv_w, ffn_conv_b, w_ffn_up, w_ffn_down,
           ln_ple_g, w_ple_gate, b_ple_gate, w_ple_proj, ln_final_g):
    B, S, D = x.shape
    depth = w_in.shape[0]
    T = B * S
    H, dh = MLSTM_HEADS, MLSTM_HEAD_DIM
    aw = D // 2
    mw = D - aw
    tm = 512
    ck = 256
    L = MLSTM_CHUNK
    nc = S // L

    half = ROT_DIM // 2
    lane = jnp.arange(LANES) % ATTN_HEAD_DIM
    inv = jnp.power(ROPE_THETA, -jnp.arange(0, ROT_DIM, 2, dtype=F32) / ROT_DIM)
    invf = jnp.where(lane < ROT_DIM, inv[lane % half], 0.0).astype(F32)[None]
    sgn = jnp.where(lane < half, -1.0, jnp.where(lane < ROT_DIM, 1.0, 0.0)).astype(F32)[None]
    bias = _attn_bias()
    pos2 = positions.reshape(T, 1)

    assert depth == 1, "the final norm is fused into the layer's last kernel"
    xf = x.reshape(T, D)
    for i in range(depth):
        w = w_in[i]
        wqkv = w[:, :3 * aw].astype(BF16)
        wuvo = w[:, 3 * aw:3 * aw + 3 * mw].astype(BF16)
        wgt = jnp.pad(w[:, 3 * aw + 3 * mw:], ((0, 0), (0, LANES - 4 * H))).astype(BF16)
        q, k, v, u, vm, o, gates = _inproj(xf, pos2, ln_mix_g[i][None], invf, sgn, wqkv, wuvo, wgt, tm)

        attn = _attn(q.reshape(B, S, aw), k.reshape(B, S, aw), v.reshape(B, S, aw), bias)

        g4 = gates[:, :4 * H].reshape(B, S, 4, H)
        gcol = g4.transpose(0, 3, 1, 2)
        grow = g4.transpose(0, 3, 2, 1).reshape(B, H, 4, nc, L)
        gate_bias = jnp.stack([b_igate[i], b_fgate[i]]).astype(F32)
        mem = _mlstm(gate_bias, u.reshape(B, S, mw), vm.reshape(B, S, mw), o.reshape(B, S, mw), gcol, grow,
                     mlstm_conv_w[i], mlstm_conv_b[i][None], w_mq[i].astype(BF16), w_mk[i].astype(BF16),
                     mlstm_gn_g[i][None])

        wo = w_out[i].astype(BF16)
        x1, h2 = _outproj(xf, attn.reshape(T, aw), mem.reshape(T, mw), wo[:aw], wo[aw:], ln_ffn_g[i][None], tm)

        F = w_ffn_gate.shape[-1]
        nck = F // ck
        wg = w_ffn_gate[i].astype(BF16).reshape(D, nck, ck).transpose(1, 0, 2)
        wu = w_ffn_up[i].astype(BF16).reshape(D, nck, ck).transpose(1, 0, 2)
        wd = w_ffn_down[i].astype(BF16).reshape(nck, ck, D)
        cw = ffn_conv_w[i].reshape(3, nck, ck).transpose(1, 0, 2)
        cb = ffn_conv_b[i].reshape(nck, 1, ck)
        xf = _ffn(x1, h2, p[i].reshape(T, -1), wg, wu, wd, cw, cb, ln_ple_g[i][None],
                  w_ple_gate[i].astype(BF16), b_ple_gate[i][None], w_ple_proj[i].astype(BF16), ln_final_g[None], tm, S)
    return xf.reshape(B, S, D)
```

```python
import functools
import math

import jax
import jax.numpy as jnp
from jax import lax
from jax.experimental import pallas as pl
from jax.experimental.pallas import tpu as pltpu

F32 = jnp.float32
BF16 = jnp.bfloat16

EPS = 1e-6
ATTN_HEAD_DIM = 64
ATTN_PATTERNS = ((128, 1), (512, 4), (2048, 16))
ATTN_RADIUS = 64
ROPE_THETA = 500000.0
ROT_DIM = ATTN_HEAD_DIM // 4
MLSTM_HEADS = 4
MLSTM_HEAD_DIM = 128
LANES = 128
ATT_BLK = 128
MLSTM_CHUNK = 128
VMEM_LIMIT = 56 * 1024 * 1024

_NEG = float("-inf")


def _rms(x, g):
    ms = jnp.sum(x * x, axis=-1, keepdims=True) * (1.0 / x.shape[-1])
    return x * lax.rsqrt(ms + EPS) * g


def _const_spec(shape):
    nd = len(shape)
    return pl.BlockSpec(shape, lambda *_: (0,) * nd, pipeline_mode=pl.Buffered(1))


def _inproj_kernel(x_ref, pos_ref, g_ref, invf_ref, sgn_ref, wqkv_ref, wuvo_ref, wgt_ref,
                   q_ref, k_ref, v_ref, u_ref, vm_ref, o_ref, gate_ref):
    h = _rms(x_ref[...], g_ref[...]).astype(BF16)
    ang = pos_ref[...].astype(F32) * invf_ref[...]
    cos = jnp.cos(ang)
    sin = jnp.sin(ang) * sgn_ref[...]
    lane = lax.broadcasted_iota(jnp.int32, (1, LANES), 1)
    first_half = (lane % ATTN_HEAD_DIM) < (ROT_DIM // 2)

    def rotary(z):
        up = pltpu.roll(z, LANES - ROT_DIM // 2, 1)
        dn = pltpu.roll(z, ROT_DIM // 2, 1)
        return z * cos + jnp.where(first_half, up, dn) * sin

    zqkv = jnp.dot(h, wqkv_ref[...], preferred_element_type=F32)
    aw = zqkv.shape[1] // 3
    for j in range(aw // LANES):
        sl = slice(j * LANES, (j + 1) * LANES)
        q_ref[:, sl] = rotary(zqkv[:, j * LANES:(j + 1) * LANES]) * (ATTN_HEAD_DIM ** -0.5)
        k_ref[:, sl] = rotary(zqkv[:, aw + j * LANES:aw + (j + 1) * LANES])
    v_ref[...] = zqkv[:, 2 * aw:]
    zuvo = jnp.dot(h, wuvo_ref[...], preferred_element_type=F32)
    mw = zuvo.shape[1] // 3
    u_ref[...] = zuvo[:, :mw]
    vm_ref[...] = zuvo[:, mw:2 * mw].astype(BF16)
    o_ref[...] = zuvo[:, 2 * mw:]
    gate_ref[...] = jnp.dot(h, wgt_ref[...], preferred_element_type=F32)


def _inproj(x2, pos2, g, invf, sgn, wqkv, wuvo, wgt, tm):
    T, D = x2.shape
    aw = wqkv.shape[1] // 3
    mw = wuvo.shape[1] // 3
    row = lambda w: pl.BlockSpec((tm, w), lambda i: (i, 0))
    return pl.pallas_call(
        _inproj_kernel,
        grid=(T // tm,),
        in_specs=[row(D), row(1), _const_spec((1, D)), _const_spec((1, LANES)), _const_spec((1, LANES)),
                  _const_spec(wqkv.shape), _const_spec(wuvo.shape), _const_spec(wgt.shape)],
        out_specs=[row(aw), row(aw), row(aw), row(mw), row(mw), row(mw), row(LANES)],
        out_shape=[jax.ShapeDtypeStruct((T, aw), F32)] * 3
        + [jax.ShapeDtypeStruct((T, mw), F32), jax.ShapeDtypeStruct((T, mw), BF16),
           jax.ShapeDtypeStruct((T, mw), F32), jax.ShapeDtypeStruct((T, LANES), F32)],
        compiler_params=pltpu.CompilerParams(dimension_semantics=("arbitrary",), vmem_limit_bytes=VMEM_LIMIT),
        name="inproj",
    )(x2, pos2, g, invf, sgn, wqkv, wuvo, wgt)


def _attn_bias():
    W = 2 * ATT_BLK
    qi = jnp.arange(ATT_BLK)[:, None]
    kj = jnp.arange(W)[None, :]
    band = (kj - qi >= 0) & (kj - qi <= 2 * ATTN_RADIUS)
    tiles = [band & (kj >= ATTN_RADIUS), band, band & (kj < W - ATTN_RADIUS),
             (jnp.abs(kj - qi) <= ATTN_RADIUS) & (kj < ATT_BLK)]
    return jnp.where(jnp.stack(tiles), 0.0, _NEG).astype(F32)


def _attn_kernel(q_ref, k_ref, v_ref, bias_ref, out_ref, qc_ref, kp_ref, vp_ref, o_scr, l_scr):
    S = q_ref.shape[0]
    PAD = ATTN_RADIUS
    lane = lax.broadcasted_iota(jnp.int32, (1, LANES), 1)
    head0 = lane < ATTN_HEAD_DIM
    zpad = jnp.zeros((PAD, LANES), BF16)
    kp_ref[0:PAD, :] = zpad
    vp_ref[0:PAD, :] = zpad

    def softmax_block(qb, kw, vw, bias):
        outs, lses = [], []
        for hh in range(2):
            mask = head0 if hh == 0 else jnp.logical_not(head0)
            qm = jnp.where(mask, qb, jnp.zeros_like(qb))
            s = lax.dot_general(qm, kw, (((1,), (1,)), ((), ())), preferred_element_type=F32) + bias
            m = jnp.max(s, axis=1, keepdims=True)
            p = jnp.exp(s - m)
            l = jnp.sum(p, axis=1, keepdims=True)
            o = jnp.dot(p.astype(BF16), vw, preferred_element_type=F32)
            outs.append(o * (1.0 / l))
            lses.append(m + jnp.log(l))
        return jnp.where(head0, outs[0], outs[1]), jnp.where(head0, lses[0], lses[1])

    def rows_of(r, start, size, d):
        return pl.ds(start, size) if d == 1 else pl.ds(r + d * start, size, stride=d)

    for pi, (_, d) in enumerate(ATTN_PATTERNS):
        M = S // d
        nb = M // ATT_BLK
        if nb > 1:
            kp_ref[PAD + M:2 * PAD + M, :] = zpad
            vp_ref[PAD + M:2 * PAD + M, :] = zpad

            def class_body(r, carry, d=d, M=M, nb=nb, pi=pi):
                qc_ref[0:M, :] = q_ref[rows_of(r, 0, M, d), :].astype(BF16)
                kp_ref[PAD:PAD + M, :] = k_ref[rows_of(r, 0, M, d), :].astype(BF16)
                vp_ref[PAD:PAD + M, :] = v_ref[rows_of(r, 0, M, d), :].astype(BF16)
                for i in range(nb):
                    row0 = i * ATT_BLK
                    sel = 0 if i == 0 else (2 if i == nb - 1 else 1)
                    o, lse = softmax_block(qc_ref[row0:row0 + ATT_BLK, :], kp_ref[row0:row0 + 2 * ATT_BLK, :],
                                           vp_ref[row0:row0 + 2 * ATT_BLK, :], bias_ref[sel])
                    o_scr[pi, rows_of(r, row0, ATT_BLK, d), :] = o
                    l_scr[pi, rows_of(r, row0, ATT_BLK, d), :] = lse
                return carry

            lax.fori_loop(0, d, class_body, 0)
        else:
            group = 4

            def group_body(gi, carry, d=d, M=M, pi=pi):
                for j in range(group):
                    r = gi * group + j
                    rows = rows_of(r, 0, M, d)
                    o, lse = softmax_block(q_ref[rows, :].astype(BF16), k_ref[rows, :].astype(BF16),
                                           v_ref[rows, :].astype(BF16), bias_ref[3][:, :ATT_BLK])
                    o_scr[pi, rows, :] = o
                    l_scr[pi, rows, :] = lse
                return carry

            lax.fori_loop(0, d // group, group_body, 0)

    lse = l_scr[...]
    mx = jnp.max(lse, axis=0)
    w = jnp.exp(lse - mx[None])
    out = jnp.sum(w * o_scr[...], axis=0) / jnp.sum(w, axis=0)
    out_ref[...] = out.astype(out_ref.dtype)


def _attn(q, k, v, bias):
    B, S, AW = q.shape
    blk = pl.BlockSpec((None, S, LANES), lambda b, j: (b, 0, j))
    npat = len(ATTN_PATTERNS)
    return pl.pallas_call(
        _attn_kernel,
        grid=(B, AW // LANES),
        in_specs=[blk, blk, blk, _const_spec(bias.shape)],
        out_specs=blk,
        out_shape=jax.ShapeDtypeStruct((B, S, AW), BF16),
        scratch_shapes=[pltpu.VMEM((S, LANES), BF16), pltpu.VMEM((S + 2 * ATTN_RADIUS, LANES), BF16),
                        pltpu.VMEM((S + 2 * ATTN_RADIUS, LANES), BF16),
                        pltpu.VMEM((npat, S, LANES), F32), pltpu.VMEM((npat, S, LANES), F32)],
        compiler_params=pltpu.CompilerParams(dimension_semantics=("arbitrary", "arbitrary"),
                                             vmem_limit_bytes=VMEM_LIMIT),
        name="dilated_attn",
    )(q, k, v, bias)


def _split3(x):
    hi = x.astype(BF16)
    r1 = x - hi.astype(F32)
    mid = r1.astype(BF16)
    lo = (r1 - mid.astype(F32)).astype(BF16)
    return hi, mid, lo


def _tri_dot_left(tri, x):
    return sum(jnp.dot(tri, part, preferred_element_type=F32) for part in _split3(x))


def _tri_dot_right(x, tri):
    return sum(jnp.dot(part, tri, preferred_element_type=F32) for part in _split3(x))


def _log_sigmoid(x):
    return jnp.minimum(x, 0.0) - jnp.log1p(jnp.exp(-jnp.abs(x)))


def _mlstm_kernel(bias_ref, u_ref, vm_ref, o_ref, gcol_ref, grow_ref, cw_ref, cb_ref, wq_ref, wk_ref, gn_ref,
                  out_ref, q_scr, k_scr, hs_scr, lfb_scr, ib_scr, brow_scr, irow_scr):
    S, dh = u_ref.shape
    L = MLSTM_CHUNK
    nc = S // L
    hd = pl.program_id(1)

    u = u_ref[...]
    rows = lax.broadcasted_iota(jnp.int32, (S, 1), 0)
    u_prev = jnp.where(rows == 0, 0.0, pltpu.roll(u, 1, 0))
    u_next = jnp.where(rows == S - 1, 0.0, pltpu.roll(u, S - 1, 0))
    cw = cw_ref[...]
    a = u_prev * cw[0:1] + u * cw[1:2] + u_next * cw[2:3] + cb_ref[...]
    uc = (a * jax.nn.sigmoid(a)).astype(BF16)
    q_scr[...] = jnp.dot(uc, wq_ref[...], preferred_element_type=F32).astype(BF16)
    k_scr[...] = (jnp.dot(uc, wk_ref[...], preferred_element_type=F32) * (dh ** -0.5)).astype(BF16)
    hs_scr[...] = jnp.zeros_like(hs_scr)

    ri = lax.broadcasted_iota(jnp.int32, (L, L), 0)
    ci = lax.broadcasted_iota(jnp.int32, (L, L), 1)
    lower = ri >= ci
    tril = lower.astype(BF16)
    triu = (ri <= ci).astype(BF16)

    for di in range(2):
        b_i = bias_ref[0, di, hd]
        b_f = bias_ref[1, di, hd]
        ib_scr[di] = jnp.broadcast_to(gcol_ref[:, di:di + 1] + b_i, (S, dh))
        lfb_scr[di] = jnp.broadcast_to(_log_sigmoid(gcol_ref[:, 2 + di:3 + di] + b_f), (S, dh))
        irow_scr[di] = grow_ref[di] + b_i
        lf_rows = _log_sigmoid(grow_ref[2 + di] + b_f)
        brow_scr[di] = _tri_dot_right(lf_rows, triu if di == 0 else tril)

    def chunk_step(di, c, state):
        C, n, m = state
        r0 = pl.multiple_of(c * L, L)
        qc = q_scr[pl.ds(r0, L), :]
        kc = k_scr[pl.ds(r0, L), :]
        vc = vm_ref[pl.ds(r0, L), :]
        lfb = lfb_scr[di, pl.ds(r0, L), :]
        icol = ib_scr[di, pl.ds(r0, L), :]
        bcol = _tri_dot_left(tril if di == 0 else triu, lfb)
        brow = brow_scr[di, pl.ds(c, 1), :]
        irow = irow_scr[di, pl.ds(c, 1), :]
        causal = lower if di == 0 else jnp.logical_not(ri > ci)
        dmat = jnp.where(causal, bcol - brow + irow, _NEG)
        m_inter = bcol + m
        m_t = jnp.maximum(m_inter, jnp.max(dmat, axis=1, keepdims=True))
        qk = lax.dot_general(qc, kc, (((1,), (1,)), ((), ())), preferred_element_type=F32)
        pmat = qk * jnp.exp(dmat - m_t)
        a_in = jnp.exp(m_inter - m_t)
        num = a_in * jnp.dot(qc, C.astype(BF16), preferred_element_type=F32) \
            + jnp.dot(pmat.astype(BF16), vc, preferred_element_type=F32)
        den = a_in * jnp.sum(qc.astype(F32) * n, axis=1, keepdims=True) + jnp.sum(pmat, axis=1, keepdims=True)
        hval = num / jnp.maximum(jnp.abs(den), jnp.exp(-m_t))
        hs_scr[pl.ds(r0, L), :] += hval
        g = bcol[L - 1:L, :] if di == 0 else bcol[0:1, :]
        logw = g - bcol + icol
        m_new = jnp.maximum(g + m, jnp.max(logw, axis=0, keepdims=True))
        decay = jnp.exp(g + m - m_new)
        kw = kc.astype(F32) * jnp.exp(logw - m_new)
        C_new = decay * C + lax.dot_general(kw.astype(BF16), vc, (((0,), (0,)), ((), ())),
                                            preferred_element_type=F32)
        n_new = decay * n + jnp.sum(kw, axis=0, keepdims=True)
        return C_new, n_new, m_new

    def loop_body(c, states):
        sf, sb = states
        return chunk_step(0, c, sf), chunk_step(1, nc - 1 - c, sb)

    init = (jnp.zeros((dh, dh), F32), jnp.zeros((1, dh), F32), jnp.zeros((1, dh), F32))
    lax.fori_loop(0, nc, loop_body, (init, init), unroll=4)

    hs = hs_scr[...]
    mu = jnp.sum(hs, axis=1, keepdims=True) * (1.0 / dh)
    cen = hs - mu
    var = jnp.sum(cen * cen, axis=1, keepdims=True) * (1.0 / dh)
    hn = cen * lax.rsqrt(var + EPS)
    out_ref[...] = (hn * gn_ref[...] * jax.nn.sigmoid(o_ref[...])).astype(out_ref.dtype)


def _mlstm(gate_bias, u, vm, o, gcol, grow, cw, cb, wq, wk, gn):
    B, S, MW = u.shape
    H, dh = MLSTM_HEADS, MLSTM_HEAD_DIM
    L = MLSTM_CHUNK
    nc = S // L
    tok = pl.BlockSpec((None, S, dh), lambda b, h: (b, 0, h))
    return pl.pallas_call(
        _mlstm_kernel,
        grid=(B, H),
        in_specs=[pl.BlockSpec(memory_space=pltpu.SMEM), tok, tok, tok,
                  pl.BlockSpec((None, None, S, 4), lambda b, h: (b, h, 0, 0)),
                  pl.BlockSpec((None, None, 4, nc, L), lambda b, h: (b, h, 0, 0, 0)),
                  pl.BlockSpec((3, dh), lambda b, h: (0, h)), pl.BlockSpec((1, dh), lambda b, h: (0, h)),
                  pl.BlockSpec((None, dh, dh), lambda b, h: (h, 0, 0)),
                  pl.BlockSpec((None, dh, dh), lambda b, h: (h, 0, 0)),
                  pl.BlockSpec((1, dh), lambda b, h: (0, h))],
        out_specs=tok,
        out_shape=jax.ShapeDtypeStruct((B, S, MW), BF16),
        scratch_shapes=[pltpu.VMEM((S, dh), BF16), pltpu.VMEM((S, dh), BF16), pltpu.VMEM((S, dh), F32),
                        pltpu.VMEM((2, S, dh), F32), pltpu.VMEM((2, S, dh), F32),
                        pltpu.VMEM((2, nc, L), F32), pltpu.VMEM((2, nc, L), F32)],
        compiler_params=pltpu.CompilerParams(dimension_semantics=("arbitrary", "arbitrary"),
                                             vmem_limit_bytes=VMEM_LIMIT),
        name="mlstm",
    )(gate_bias, u, vm, o, gcol, grow, cw, cb, wq, wk, gn)


def _outproj_kernel(x_ref, a_ref, m_ref, wa_ref, wm_ref, g_ref, x1_ref, h_ref):
    x1 = x_ref[...] + jnp.dot(a_ref[...], wa_ref[...], preferred_element_type=F32) \
        + jnp.dot(m_ref[...], wm_ref[...], preferred_element_type=F32)
    x1_ref[...] = x1
    h_ref[...] = _rms(x1, g_ref[...]).astype(BF16)


def _outproj(x2, attn, mem, wa, wm, g, tm):
    T, D = x2.shape
    row = lambda w: pl.BlockSpec((tm, w), lambda i: (i, 0))
    return pl.pallas_call(
        _outproj_kernel,
        grid=(T // tm,),
        in_specs=[row(D), row(attn.shape[1]), row(mem.shape[1]), _const_spec(wa.shape), _const_spec(wm.shape),
                  _const_spec((1, D))],
        out_specs=[row(D), row(D)],
        out_shape=[jax.ShapeDtypeStruct((T, D), F32), jax.ShapeDtypeStruct((T, D), BF16)],
        compiler_params=pltpu.CompilerParams(dimension_semantics=("arbitrary",), vmem_limit_bytes=VMEM_LIMIT),
        name="outproj",
    )(x2, attn, mem, wa, wm, g)


HALO = 16


def _ffn_kernel(x1_ref, h_ref, hp_ref, hn_ref, p_ref, wg_ref, wu_ref, wd_ref, cw_ref, cb_ref,
                gp_ref, wpg_ref, bpg_ref, wpp_ref, gf_ref, out_ref, hcat_ref, a_scr, acc_ref, *, tiles_per_seq):
    tm = h_ref.shape[0]
    i = pl.program_id(0)
    first = (i % tiles_per_seq) == 0
    last = (i % tiles_per_seq) == tiles_per_seq - 1
    hcat_ref[0:HALO, :] = jnp.where(first, jnp.zeros_like(hp_ref[...]), hp_ref[...])
    hcat_ref[HALO:HALO + tm, :] = h_ref[...]
    hcat_ref[HALO + tm:, :] = jnp.where(last, jnp.zeros_like(hn_ref[...]), hn_ref[...])
    acc_ref[...] = jnp.zeros_like(acc_ref)

    for c in range(wg_ref.shape[0]):
        a_buf = a_scr.at[c % 2]
        a_buf[...] = jnp.dot(hcat_ref[...], wg_ref[c], preferred_element_type=F32)
        cw = cw_ref[c]
        a = a_buf[pl.ds(HALO - 1, tm), :] * cw[0:1] + a_buf[pl.ds(HALO, tm), :] * cw[1:2] \
            + a_buf[pl.ds(HALO + 1, tm), :] * cw[2:3] + cb_ref[c]
        up = jnp.dot(h_ref[...], wu_ref[c], preferred_element_type=F32)
        act = 0.5 * a * (1.0 + lax.erf(a * (2.0 ** -0.5))) * up
        acc_ref[...] += jnp.dot(act.astype(BF16), wd_ref[c], preferred_element_type=F32)
    x2 = x1_ref[...] + acc_ref[...]
    h3 = _rms(x2, gp_ref[...]).astype(BF16)
    gate = jax.nn.sigmoid(jnp.dot(h3, wpg_ref[...], preferred_element_type=F32) + bpg_ref[...])
    pp = jnp.dot(p_ref[...].astype(BF16), wpp_ref[...], preferred_element_type=F32)
    out_ref[...] = _rms(x2 + pp * gate, gf_ref[...])


def _ffn(x1, h2, p2, wg, wu, wd, cw, cb, gp, wpg, bpg, wpp, gf, tm, S):
    T, D = x1.shape
    nck, _, ck = wg.shape
    row = lambda w: pl.BlockSpec((tm, w), lambda i: (i, 0))
    hb = tm // HALO
    nhb = T // HALO
    prev = pl.BlockSpec((HALO, D), lambda i: (jnp.maximum(i * hb - 1, 0), 0))
    nxt = pl.BlockSpec((HALO, D), lambda i: (jnp.minimum((i + 1) * hb, nhb - 1), 0))
    return pl.pallas_call(
        functools.partial(_ffn_kernel, tiles_per_seq=S // tm),
        grid=(T // tm,),
        in_specs=[row(D), row(D), prev, nxt, row(p2.shape[1]),
                  _const_spec(wg.shape), _const_spec(wu.shape), _const_spec(wd.shape),
                  _const_spec(cw.shape), _const_spec(cb.shape), _const_spec((1, D)),
                  _const_spec(wpg.shape), _const_spec((1, D)), _const_spec(wpp.shape), _const_spec((1, D))],
        out_specs=row(D),
        out_shape=jax.ShapeDtypeStruct((T, D), F32),
        scratch_shapes=[pltpu.VMEM((tm + 2 * HALO, D), BF16), pltpu.VMEM((2, tm + 2 * HALO, ck), F32),
                        pltpu.VMEM((tm, D), F32)],
        compiler_params=pltpu.CompilerParams(dimension_semantics=("arbitrary",), vmem_limit_bytes=VMEM_LIMIT),
        name="ffn_ple",
    )(x1, h2, h2, h2, p2, wg, wu, wd, cw, cb, gp, wpg, bpg, wpp, gf)


def kernel(x, p, positions, ln_mix_g, w_in, mlstm_conv_w, mlstm_conv_b, w_mq, w_mk, b_igate, b_fgate,
           mlstm_gn_g, w_out, ln_ffn_g, w_ffn_gate, ffn_conv_w, ffn_conv_b, w_ffn_up, w_ffn_down,
           ln_ple_g, w_ple_gate, b_ple_gate, w_ple_proj, ln_final_g):
    B, S, D = x.shape
    depth = w_in.shape[0]
    T = B * S
    H, dh = MLSTM_HEADS, MLSTM_HEAD_DIM
    aw = D // 2
    mw = D - aw
    tm = 512
    ck = 256
    L = MLSTM_CHUNK
    nc = S // L

    half = ROT_DIM // 2
    lane = jnp.arange(LANES) % ATTN_HEAD_DIM
    inv = jnp.power(ROPE_THETA, -jnp.arange(0, ROT_DIM, 2, dtype=F32) / ROT_DIM)
    invf = jnp.where(lane < ROT_DIM, inv[lane % half], 0.0).astype(F32)[None]
    sgn = jnp.where(lane < half, -1.0, jnp.where(lane < ROT_DIM, 1.0, 0.0)).astype(F32)[None]
    bias = _attn_bias()
    pos2 = positions.reshape(T, 1)

    assert depth == 1, "the final norm is fused into the layer's last kernel"
    xf = x.reshape(T, D)
    for i in range(depth):
        w = w_in[i]
        wqkv = w[:, :3 * aw].astype(BF16)
        wuvo = w[:, 3 * aw:3 * aw + 3 * mw].astype(BF16)
        wgt = jnp.pad(w[:, 3 * aw + 3 * mw:], ((0, 0), (0, LANES - 4 * H))).astype(BF16)
        q, k, v, u, vm, o, gates = _inproj(xf, pos2, ln_mix_g[i][None], invf, sgn, wqkv, wuvo, wgt, tm)

        attn = _attn(q.reshape(B, S, aw), k.reshape(B, S, aw), v.reshape(B, S, aw), bias)

        g4 = gates[:, :4 * H].reshape(B, S, 4, H)
        gcol = g4.transpose(0, 3, 1, 2)
        grow = g4.transpose(0, 3, 2, 1).reshape(B, H, 4, nc, L)
        gate_bias = jnp.stack([b_igate[i], b_fgate[i]]).astype(F32)
        mem = _mlstm(gate_bias, u.reshape(B, S, mw), vm.reshape(B, S, mw), o.reshape(B, S, mw), gcol, grow,
                     mlstm_conv_w[i], mlstm_conv_b[i][None], w_mq[i].astype(BF16), w_mk[i].astype(BF16),
                     mlstm_gn_g[i][None])

        wo = w_out[i].astype(BF16)
        x1, h2 = _outproj(xf, attn.reshape(T, aw), mem.reshape(T, mw), wo[:aw], wo[aw:], ln_ffn_g[i][None], tm)

        F = w_ffn_gate.shape[-1]
        nck = F // ck
        wg = w_ffn_gate[i].astype(BF16).reshape(D, nck, ck).transpose(1, 0, 2)
        wu = w_ffn_up[i].astype(BF16).reshape(D, nck, ck).transpose(1, 0, 2)
        wd = w_ffn_down[i].astype(BF16).reshape(nck, ck, D)
        cw = ffn_conv_w[i].reshape(3, nck, ck).transpose(1, 0, 2)
        cb = ffn_conv_b[i].reshape(nck, 1, ck)
        xf = _ffn(x1, h2, p[i].reshape(T, -1), wg, wu, wd, cw, cb, ln_ple_g[i][None],
                  w_ple_gate[i].astype(BF16), b_ple_gate[i][None], w_ple_proj[i].astype(BF16), ln_final_g[None], tm, S)
    return xf.reshape(B, S, D)
```

```python
import functools
import math

import jax
import jax.numpy as jnp
from jax import lax
from jax.experimental import pallas as pl
from jax.experimental.pallas import tpu as pltpu

F32 = jnp.float32
BF16 = jnp.bfloat16

EPS = 1e-6
ATTN_HEAD_DIM = 64
ATTN_PATTERNS = ((128, 1), (512, 4), (2048, 16))
ATTN_RADIUS = 64
ROPE_THETA = 500000.0
ROT_DIM = ATTN_HEAD_DIM // 4
MLSTM_HEADS = 4
MLSTM_HEAD_DIM = 128
LANES = 128
ATT_BLK = 128
ATT_REGION = 8
MLSTM_CHUNK = 128
VMEM_LIMIT = 56 * 1024 * 1024

_NEG = float("-inf")


def _rms(x, g):
    ms = jnp.sum(x * x, axis=-1, keepdims=True) * (1.0 / x.shape[-1])
    return x * lax.rsqrt(ms + EPS) * g


def _const_spec(shape):
    nd = len(shape)
    return pl.BlockSpec(shape, lambda *_: (0,) * nd, pipeline_mode=pl.Buffered(1))


def _inproj_kernel(x_ref, pos_ref, g_ref, invf_ref, sgn_ref, wqkv_ref, wuvo_ref, wgt_ref,
                   q_ref, k_ref, v_ref, u_ref, vm_ref, o_ref, gate_ref):
    h = _rms(x_ref[...], g_ref[...]).astype(BF16)
    ang = pos_ref[...].astype(F32) * invf_ref[...]
    cos = jnp.cos(ang)
    sin = jnp.sin(ang) * sgn_ref[...]
    lane = lax.broadcasted_iota(jnp.int32, (1, LANES), 1)
    first_half = (lane % ATTN_HEAD_DIM) < (ROT_DIM // 2)

    def rotary(z):
        up = pltpu.roll(z, LANES - ROT_DIM // 2, 1)
        dn = pltpu.roll(z, ROT_DIM // 2, 1)
        return z * cos + jnp.where(first_half, up, dn) * sin

    zqkv = jnp.dot(h, wqkv_ref[...], preferred_element_type=F32)
    aw = zqkv.shape[1] // 3
    for j in range(aw // LANES):
        sl = slice(j * LANES, (j + 1) * LANES)
        q_ref[:, sl] = rotary(zqkv[:, j * LANES:(j + 1) * LANES]) * (ATTN_HEAD_DIM ** -0.5)
        k_ref[:, sl] = rotary(zqkv[:, aw + j * LANES:aw + (j + 1) * LANES])
    v_ref[...] = zqkv[:, 2 * aw:]
    zuvo = jnp.dot(h, wuvo_ref[...], preferred_element_type=F32)
    mw = zuvo.shape[1] // 3
    u_ref[...] = zuvo[:, :mw]
    vm_ref[...] = zuvo[:, mw:2 * mw].astype(BF16)
    o_ref[...] = zuvo[:, 2 * mw:]
    gate_ref[...] = jnp.dot(h, wgt_ref[...], preferred_element_type=F32)


def _inproj(x2, pos2, g, invf, sgn, wqkv, wuvo, wgt, tm):
    T, D = x2.shape
    aw = wqkv.shape[1] // 3
    mw = wuvo.shape[1] // 3
    row = lambda w: pl.BlockSpec((tm, w), lambda i: (i, 0))
    return pl.pallas_call(
        _inproj_kernel,
        grid=(T // tm,),
        in_specs=[row(D), row(1), _const_spec((1, D)), _const_spec((1, LANES)), _const_spec((1, LANES)),
                  _const_spec(wqkv.shape), _const_spec(wuvo.shape), _const_spec(wgt.shape)],
        out_specs=[row(aw), row(aw), row(aw), row(mw), row(mw), row(mw), row(LANES)],
        out_shape=[jax.ShapeDtypeStruct((T, aw), F32)] * 3
        + [jax.ShapeDtypeStruct((T, mw), F32), jax.ShapeDtypeStruct((T, mw), BF16),
           jax.ShapeDtypeStruct((T, mw), F32), jax.ShapeDtypeStruct((T, LANES), F32)],
        compiler_params=pltpu.CompilerParams(dimension_semantics=("arbitrary",), vmem_limit_bytes=VMEM_LIMIT),
        name="inproj",
    )(x2, pos2, g, invf, sgn, wqkv, wuvo, wgt)


def _attn_bias():
    W, R, Q = 2 * ATT_BLK, ATTN_RADIUS, ATT_BLK
    kj = jnp.arange(W)[:, None]
    qi = jnp.arange(Q)[None, :]
    interior = (kj - qi >= 0) & (kj - qi <= 2 * R)
    near = jnp.abs(kj - qi) <= R
    edge = ((qi < Q // 2) & (kj < Q) & near) | ((qi >= Q // 2) & (kj >= Q) & (jnp.abs(kj - Q - qi) <= R))
    single = near & (kj < Q)
    return jnp.where(jnp.stack([interior, edge, single]), 0.0, _NEG).astype(F32)


def _attn_kernel(q_ref, k_ref, v_ref, bias_ref, out_ref, qc_ref, kc_ref, vt0_ref, vt1_ref, o_scr, l_scr):
    S = q_ref.shape[0]
    Q, HD = ATT_BLK, ATTN_HEAD_DIM
    lane = lax.broadcasted_iota(jnp.int32, (1, LANES), 1)
    head0 = lane < HD
    npat = len(ATTN_PATTERNS)
    vt0_ref[:, HD:, :] = jnp.ones((npat, HD, S), BF16)
    vt1_ref[:, :HD, :] = jnp.ones((npat, HD, S), BF16)

    def rows_of(r, start, size, d):
        return pl.ds(start, size) if d == 1 else pl.ds(r + d * start, size, stride=d)

    def cat(parts, axis):
        return parts[0] if len(parts) == 1 else jnp.concatenate(parts, axis=axis)

    blocks = []
    for pi, (_, d) in enumerate(ATTN_PATTERNS):
        M = S // d
        nb = M // Q
        for r in range(d):
            base = r * M
            tok = rows_of(r, 0, M, d)
            qc_ref[pi, base:base + M, :] = q_ref[tok, :].astype(BF16)
            kc_ref[pi, base:base + M, :] = k_ref[tok, :].astype(BF16)
            vt = v_ref[tok, :].T.astype(BF16)
            vt0_ref[pi, :HD, base:base + M] = vt[:HD]
            vt1_ref[pi, HD:, base:base + M] = vt[HD:]
            if nb == 1:
                blocks.append(dict(pi=pi, q=[slice(base, base + Q)], k=[slice(base, base + Q)], bias=2, W=Q,
                                   out=[(tok, slice(0, Q))]))
                continue
            for i in range(nb - 1):
                k0 = base + i * Q
                blocks.append(dict(pi=pi, q=[slice(k0 + Q // 2, k0 + Q // 2 + Q)], k=[slice(k0, k0 + 2 * Q)],
                                   bias=0, W=2 * Q, out=[(rows_of(r, i * Q + Q // 2, Q, d), slice(0, Q))]))
            blocks.append(dict(pi=pi, q=[slice(base, base + Q // 2), slice(base + M - Q // 2, base + M)],
                               k=[slice(base, base + Q), slice(base + M - Q, base + M)], bias=1, W=2 * Q,
                               out=[(rows_of(r, 0, Q // 2, d), slice(0, Q // 2)),
                                    (rows_of(r, M - Q // 2, Q // 2, d), slice(Q // 2, Q))]))

    def run_region(region):
        scores = []
        for b in region:
            qb = cat([qc_ref[b["pi"], s, :] for s in b["q"]], 0)
            kw = cat([kc_ref[b["pi"], s, :] for s in b["k"]], 0)
            bias = bias_ref[b["bias"]][:b["W"]]
            for hh in range(2):
                qm = jnp.where(head0 if hh == 0 else jnp.logical_not(head0), qb, jnp.zeros_like(qb))
                scores.append(lax.dot_general(kw, qm, (((1,), (1,)), ((), ())), preferred_element_type=F32) + bias)
        maxes = [jnp.max(s, axis=0, keepdims=True) for s in scores]
        probs = [jnp.exp(s - m).astype(BF16) for s, m in zip(scores, maxes)]
        accs = []
        for bi, b in enumerate(region):
            for hh, vref in enumerate((vt0_ref, vt1_ref)):
                vw = cat([vref[b["pi"], :, s] for s in b["k"]], 1)
                accs.append(jnp.dot(vw, probs[2 * bi + hh], preferred_element_type=F32))
        for bi, b in enumerate(region):
            o_t, l_t = [], []
            for hh in range(2):
                a = accs[2 * bi + hh]
                l = a[HD:HD + 1] if hh == 0 else a[0:1]
                o_t.append((a[:HD] if hh == 0 else a[HD:]) * (1.0 / l))
                l_t.append(jnp.broadcast_to(maxes[2 * bi + hh] + jnp.log(l), (HD, Q)))
            o = jnp.concatenate(o_t, axis=0).T
            lse = jnp.concatenate(l_t, axis=0).T
            for rows, sl in b["out"]:
                o_scr[b["pi"], rows, :] = o[sl]
                l_scr[b["pi"], rows, :] = lse[sl]

    for g in range(0, len(blocks), ATT_REGION):
        run_region(blocks[g:g + ATT_REGION])

    lse = l_scr[...]
    mx = jnp.max(lse, axis=0)
    w = jnp.exp(lse - mx[None])
    out = jnp.sum(w * o_scr[...], axis=0) / jnp.sum(w, axis=0)
    out_ref[...] = out.astype(out_ref.dtype)


def _attn(q, k, v, bias):
    B, S, AW = q.shape
    blk = pl.BlockSpec((None, S, LANES), lambda b, j: (b, 0, j))
    npat = len(ATTN_PATTERNS)
    return pl.pallas_call(
        _attn_kernel,
        grid=(B, AW // LANES),
        in_specs=[blk, blk, blk, _const_spec(bias.shape)],
        out_specs=blk,
        out_shape=jax.ShapeDtypeStruct((B, S, AW), BF16),
        scratch_shapes=[pltpu.VMEM((npat, S, LANES), BF16), pltpu.VMEM((npat, S, LANES), BF16),
                        pltpu.VMEM((npat, LANES, S), BF16), pltpu.VMEM((npat, LANES, S), BF16),
                        pltpu.VMEM((npat, S, LANES), F32), pltpu.VMEM((npat, S, LANES), F32)],
        compiler_params=pltpu.CompilerParams(dimension_semantics=("arbitrary", "arbitrary"),
                                             vmem_limit_bytes=VMEM_LIMIT),
        name="dilated_attn",
    )(q, k, v, bias)


def _split2(x):
    hi = x.astype(BF16)
    return hi, (x - hi.astype(F32)).astype(BF16)


def _log_sigmoid(x):
    return jnp.minimum(x, 0.0) - jnp.log1p(jnp.exp(-jnp.abs(x)))


def _mlstm_kernel(bias_ref, u_ref, vm_ref, o_ref, grow_ref, cw_ref, cb_ref, wq_ref, wk_ref, gn_ref,
                  out_ref, q_scr, kt_scr, va_scr, hs_scr, lf_scr, b_scr, i_scr, w_scr, mp_scr, dc_scr,
                  e_scr, a_scr, f_scr, p_scr, nd_scr, cp_scr):
    S, dh = u_ref.shape
    L = MLSTM_CHUNK
    nc = S // L
    hd = pl.program_id(1)

    u = u_ref[...]
    rows = lax.broadcasted_iota(jnp.int32, (S, 1), 0)
    u_prev = jnp.where(rows == 0, 0.0, pltpu.roll(u, 1, 0))
    u_next = jnp.where(rows == S - 1, 0.0, pltpu.roll(u, S - 1, 0))
    cw = cw_ref[...]
    a = u_prev * cw[0:1] + u * cw[1:2] + u_next * cw[2:3] + cb_ref[...]
    uc = (a * jax.nn.sigmoid(a)).astype(BF16)
    q_scr[...] = jnp.dot(uc, wq_ref[...], preferred_element_type=F32).astype(BF16)
    k = jnp.dot(uc, wk_ref[...], preferred_element_type=F32) * (dh ** -0.5)
    for c in range(nc):
        kt_scr[c] = k[c * L:(c + 1) * L].T.astype(BF16)
    va_scr[:, :dh] = vm_ref[...]
    va_scr[:, dh:] = jnp.ones((S, dh), BF16)

    ri = lax.broadcasted_iota(jnp.int32, (L, L), 0)
    ci = lax.broadcasted_iota(jnp.int32, (L, L), 1)
    lower = ri >= ci
    upper = ri <= ci

    for di in range(2):
        b_i = bias_ref[0, di, hd]
        b_f = bias_ref[1, di, hd]
        irow = grow_ref[di] + b_i
        lf = _log_sigmoid(grow_ref[2 + di] + b_f)
        tri = (upper if di == 0 else lower).astype(BF16)
        brow = sum(jnp.dot(part, tri, preferred_element_type=F32) for part in _split2(lf))
        g = jnp.sum(lf, axis=1, keepdims=True)
        logw = g - brow + irow
        gb = jnp.broadcast_to(g, (nc, dh))
        mxb = jnp.broadcast_to(jnp.max(logw, axis=1, keepdims=True), (nc, dh))
        m = jnp.zeros((1, dh), F32)
        mprev, mnew = [None] * nc, [None] * nc
        for step in range(nc):
            c = step if di == 0 else nc - 1 - step
            mprev[c] = m
            m = jnp.maximum(gb[c:c + 1] + m, mxb[c:c + 1])
            mnew[c] = m
        mprev = jnp.concatenate(mprev, axis=0)
        mnew = jnp.concatenate(mnew, axis=0)
        lf_scr[di] = lf
        b_scr[di] = brow
        i_scr[di] = irow
        w_scr[di] = jnp.exp(logw - mnew)
        mp_scr[di] = mprev
        dc_scr[di] = jnp.exp(gb + mprev - mnew)

    ones_rhs = jnp.ones((2 * L, dh), BF16)
    chunk = lambda c: slice(c * L, (c + 1) * L)

    for c in range(nc):
        for di in range(2):
            lfr, brow, irow, mprev = (ref[di, c:c + 1, :] for ref in (lf_scr, b_scr, i_scr, mp_scr))
            causal = lower if di == 0 else upper
            hi, lo = _split2(jnp.where(causal, lfr, 0.0))
            bcol = jnp.dot(jnp.concatenate([hi, lo], axis=1), ones_rhs, preferred_element_type=F32)
            dmat = jnp.where(causal, bcol - brow + irow, _NEG)
            m_inter = bcol + mprev
            m_t = jnp.maximum(m_inter, jnp.max(dmat, axis=1, keepdims=True))
            e_scr[di, c] = jnp.exp(dmat - m_t)
            a_scr[di, c] = jnp.exp(m_inter - m_t)
            f_scr[di, c] = jnp.exp(-m_t)

    for c in range(nc):
        qk = jnp.dot(q_scr[chunk(c), :], kt_scr[c], preferred_element_type=F32)
        for di in range(2):
            p_scr[di, c] = (qk * e_scr[di, c]).astype(BF16)

    states = [jnp.zeros((dh, 2 * dh), F32)] * 2
    for step in range(nc):
        for di in range(2):
            c = step if di == 0 else nc - 1 - step
            va = va_scr[chunk(c), :]
            nd_scr[di, c] = jnp.dot(p_scr[di, c], va, preferred_element_type=F32)
            cp_scr[di, c] = states[di].astype(BF16)
            ktw = (kt_scr[c].astype(F32) * w_scr[di, c:c + 1, :]).astype(BF16)
            decay = dc_scr[di, c:c + 1, :]
            states[di] = jnp.concatenate([decay, decay], axis=1) * states[di] \
                + jnp.dot(ktw, va, preferred_element_type=F32)

    for c in range(nc):
        qc = q_scr[chunk(c), :]
        hsum = None
        for di in range(2):
            a_in = a_scr[di, c]
            nd = jnp.concatenate([a_in, a_in], axis=1) * jnp.dot(qc, cp_scr[di, c], preferred_element_type=F32) \
                + nd_scr[di, c]
            hval = nd[:, :dh] * (1.0 / jnp.maximum(jnp.abs(nd[:, dh:]), f_scr[di, c]))
            hsum = hval if hsum is None else hsum + hval
        hs_scr[chunk(c), :] = hsum

    hs = hs_scr[...]
    mu = jnp.sum(hs, axis=1, keepdims=True) * (1.0 / dh)
    cen = hs - mu
    var = jnp.sum(cen * cen, axis=1, keepdims=True) * (1.0 / dh)
    hn = cen * lax.rsqrt(var + EPS)
    out_ref[...] = (hn * gn_ref[...] * jax.nn.sigmoid(o_ref[...])).astype(out_ref.dtype)


def _mlstm(gate_bias, u, vm, o, grow, cw, cb, wq, wk, gn):
    B, S, MW = u.shape
    H, dh = MLSTM_HEADS, MLSTM_HEAD_DIM
    L = MLSTM_CHUNK
    assert L == dh == LANES
    nc = S // L
    tok = pl.BlockSpec((None, S, dh), lambda b, h: (b, 0, h))
    return pl.pallas_call(
        _mlstm_kernel,
        grid=(B, H),
        in_specs=[pl.BlockSpec(memory_space=pltpu.SMEM), tok, tok, tok,
                  pl.BlockSpec((None, None, 4, nc, L), lambda b, h: (b, h, 0, 0, 0)),
                  pl.BlockSpec((3, dh), lambda b, h: (0, h)), pl.BlockSpec((1, dh), lambda b, h: (0, h)),
                  pl.BlockSpec((None, dh, dh), lambda b, h: (h, 0, 0)),
                  pl.BlockSpec((None, dh, dh), lambda b, h: (h, 0, 0)),
                  pl.BlockSpec((1, dh), lambda b, h: (0, h))],
        out_specs=tok,
        out_shape=jax.ShapeDtypeStruct((B, S, MW), BF16),
        scratch_shapes=[pltpu.VMEM((S, dh), BF16), pltpu.VMEM((nc, dh, L), BF16), pltpu.VMEM((S, 2 * dh), BF16),
                        pltpu.VMEM((S, dh), F32)] + [pltpu.VMEM((2, nc, L), F32)] * 6
        + [pltpu.VMEM((2, nc, L, L), F32), pltpu.VMEM((2, nc, L, dh), F32), pltpu.VMEM((2, nc, L, dh), F32),
           pltpu.VMEM((2, nc, L, L), BF16), pltpu.VMEM((2, nc, L, 2 * dh), F32),
           pltpu.VMEM((2, nc, dh, 2 * dh), BF16)],
        compiler_params=pltpu.CompilerParams(dimension_semantics=("arbitrary", "arbitrary"),
                                             vmem_limit_bytes=VMEM_LIMIT),
        name="mlstm",
    )(gate_bias, u, vm, o, grow, cw, cb, wq, wk, gn)


def _outproj_kernel(x_ref, a_ref, m_ref, wa_ref, wm_ref, g_ref, x1_ref, h_ref):
    x1 = x_ref[...] + jnp.dot(a_ref[...], wa_ref[...], preferred_element_type=F32) \
        + jnp.dot(m_ref[...], wm_ref[...], preferred_element_type=F32)
    x1_ref[...] = x1
    h_ref[...] = _rms(x1, g_ref[...]).astype(BF16)


def _outproj(x2, attn, mem, wa, wm, g, tm):
    T, D = x2.shape
    row = lambda w: pl.BlockSpec((tm, w), lambda i: (i, 0))
    return pl.pallas_call(
        _outproj_kernel,
        grid=(T // tm,),
        in_specs=[row(D), row(attn.shape[1]), row(mem.shape[1]), _const_spec(wa.shape), _const_spec(wm.shape),
                  _const_spec((1, D))],
        out_specs=[row(D), row(D)],
        out_shape=[jax.ShapeDtypeStruct((T, D), F32), jax.ShapeDtypeStruct((T, D), BF16)],
        compiler_params=pltpu.CompilerParams(dimension_semantics=("arbitrary",), vmem_limit_bytes=VMEM_LIMIT),
        name="outproj",
    )(x2, attn, mem, wa, wm, g)


HALO = 16


def _ffn_kernel(x1_ref, h_ref, hp_ref, hn_ref, p_ref, wg_ref, wu_ref, wd_ref, cw_ref, cb_ref,
                gp_ref, wpg_ref, bpg_ref, wpp_ref, gf_ref, out_ref, hcat_ref, a_scr, acc_ref, *, tiles_per_seq):
    tm = h_ref.shape[0]
    i = pl.program_id(0)
    first = (i % tiles_per_seq) == 0
    last = (i % tiles_per_seq) == tiles_per_seq - 1
    hcat_ref[0:HALO, :] = jnp.where(first, jnp.zeros_like(hp_ref[...]), hp_ref[...])
    hcat_ref[HALO:HALO + tm, :] = h_ref[...]
    hcat_ref[HALO + tm:, :] = jnp.where(last, jnp.zeros_like(hn_ref[...]), hn_ref[...])
    acc_ref[...] = jnp.zeros_like(acc_ref)

    nck = wg_ref.shape[0]

    def gate_up(c):
        a_scr[c % 2] = jnp.dot(hcat_ref[...], wg_ref[c], preferred_element_type=F32)
        return jnp.dot(h_ref[...], wu_ref[c], preferred_element_type=F32)

    up = gate_up(0)
    for c in range(nck):
        up_next = gate_up(c + 1) if c + 1 < nck else None
        a_buf = a_scr.at[c % 2]
        cw = cw_ref[c]
        a = a_buf[pl.ds(HALO - 1, tm), :] * cw[0:1] + a_buf[pl.ds(HALO, tm), :] * cw[1:2] \
            + a_buf[pl.ds(HALO + 1, tm), :] * cw[2:3] + cb_ref[c]
        act = 0.5 * a * (1.0 + lax.erf(a * (2.0 ** -0.5))) * up
        acc_ref[...] += jnp.dot(act.astype(BF16), wd_ref[c], preferred_element_type=F32)
        up = up_next
    x2 = x1_ref[...] + acc_ref[...]
    h3 = _rms(x2, gp_ref[...]).astype(BF16)
    gate = jax.nn.sigmoid(jnp.dot(h3, wpg_ref[...], preferred_element_type=F32) + bpg_ref[...])
    pp = jnp.dot(p_ref[...].astype(BF16), wpp_ref[...], preferred_element_type=F32)
    out_ref[...] = _rms(x2 + pp * gate, gf_ref[...])


def _ffn(x1, h2, p2, wg, wu, wd, cw, cb, gp, wpg, bpg, wpp, gf, tm, S):
    T, D = x1.shape
    nck, _, ck = wg.shape
    row = lambda w: pl.BlockSpec((tm, w), lambda i: (i, 0))
    hb = tm // HALO
    nhb = T // HALO
    prev = pl.BlockSpec((HALO, D), lambda i: (jnp.maximum(i * hb - 1, 0), 0))
    nxt = pl.BlockSpec((HALO, D), lambda i: (jnp.minimum((i + 1) * hb, nhb - 1), 0))
    return pl.pallas_call(
        functools.partial(_ffn_kernel, tiles_per_seq=S // tm),
        grid=(T // tm,),
        in_specs=[row(D), row(D), prev, nxt, row(p2.shape[1]),
                  _const_spec(wg.shape), _const_spec(wu.shape), _const_spec(wd.shape),
                  _const_spec(cw.shape), _const_spec(cb.shape), _const_spec((1, D)),
                  _const_spec(wpg.shape), _const_spec((1, D)), _const_spec(wpp.shape), _const_spec((1, D))],
        out_specs=row(D),
        out_shape=jax.ShapeDtypeStruct((T, D), F32),
        scratch_shapes=[pltpu.VMEM((tm + 2 * HALO, D), BF16), pltpu.VMEM((2, tm + 2 * HALO, ck), F32),
                        pltpu.VMEM((tm, D), F32)],
        compiler_params=pltpu.CompilerParams(dimension_semantics=("arbitrary",), vmem_limit_bytes=VMEM_LIMIT),
        name="ffn_ple",
    )(x1, h2, h2, h2, p2, wg, wu, wd, cw, cb, gp, wpg, bpg, wpp, gf)


def kernel(x, p, positions, ln_mix_g, w_in, mlstm_conv_w, mlstm_conv_b, w_mq, w_mk, b_igate, b_fgate,
           mlstm_gn_g, w_out, ln_ffn_g, w_ffn_gate, ffn_conv_w, ffn_conv_b, w_ffn_up, w_ffn_down,
           ln_ple_g, w_ple_gate, b_ple_gate, w_ple_proj, ln_final_g):
    B, S, D = x.shape
    depth = w_in.shape[0]
    T = B * S
    H, dh = MLSTM_HEADS, MLSTM_HEAD_DIM
    aw = D // 2
    mw = D - aw
    tm = 512
    ck = 256
    L = MLSTM_CHUNK
    nc = S // L

    half = ROT_DIM // 2
    lane = jnp.arange(LANES) % ATTN_HEAD_DIM
    inv = jnp.power(ROPE_THETA, -jnp.arange(0, ROT_DIM, 2, dtype=F32) / ROT_DIM)
    invf = jnp.where(lane < ROT_DIM, inv[lane % half], 0.0).astype(F32)[None]
    sgn = jnp.where(lane < half, -1.0, jnp.where(lane < ROT_DIM, 1.0, 0.0)).astype(F32)[None]
    bias = _attn_bias()
    pos2 = positions.reshape(T, 1)

    assert depth == 1, "the final norm is fused into the layer's last kernel"
    xf = x.reshape(T, D)
    for i in range(depth):
        w = w_in[i]
        wqkv = w[:, :3 * aw].astype(BF16)
        wuvo = w[:, 3 * aw:3 * aw + 3 * mw].astype(BF16)
        wgt = jnp.pad(w[:, 3 * aw + 3 * mw:], ((0, 0), (0, LANES - 4 * H))).astype(BF16)
        q, k, v, u, vm, o, gates = _inproj(xf, pos2, ln_mix_g[i][None], invf, sgn, wqkv, wuvo, wgt, tm)

        attn = _attn(q.reshape(B, S, aw), k.reshape(B, S, aw), v.reshape(B, S, aw), bias)

        grow = gates[:, :4 * H].reshape(B, S, 4, H).transpose(0, 3, 2, 1).reshape(B, H, 4, nc, L)
        gate_bias = jnp.stack([b_igate[i], b_fgate[i]]).astype(F32)
        mem = _mlstm(gate_bias, u.reshape(B, S, mw), vm.reshape(B, S, mw), o.reshape(B, S, mw), grow,
                     mlstm_conv_w[i], mlstm_conv_b[i][None], w_mq[i].astype(BF16), w_mk[i].astype(BF16),
                     mlstm_gn_g[i][None])

        wo = w_out[i].astype(BF16)
        x1, h2 = _outproj(xf, attn.reshape(T, aw), mem.reshape(T, mw), wo[:aw], wo[aw:], ln_ffn_g[i][None], tm)

        F = w_ffn_gate.shape[-1]
        nck = F // ck
        wg = w_ffn_gate[i].astype(BF16).reshape(D, nck, ck).transpose(1, 0, 2)
        wu = w_ffn_up[i].astype(BF16).reshape(D, nck, ck).transpose(1, 0, 2)
        wd = w_ffn_down[i].astype(BF16).reshape(nck, ck, D)
        cw = ffn_conv_w[i].reshape(3, nck, ck).transpose(1, 0, 2)
        cb = ffn_conv_b[i].reshape(nck, 1, ck)
        xf = _ffn(x1, h2, p[i].reshape(T, -1), wg, wu, wd, cw, cb, ln_ple_g[i][None],
                  w_ple_gate[i].astype(BF16), b_ple_gate[i][None], w_ple_proj[i].astype(BF16), ln_final_g[None], tm, S)
    return xf.reshape(B, S, D)
```

```python
import functools
import math

import jax
import jax.numpy as jnp
from jax import lax
from jax.experimental import pallas as pl
from jax.experimental.pallas import tpu as pltpu

F32 = jnp.float32
BF16 = jnp.bfloat16

EPS = 1e-6
ATTN_HEAD_DIM = 64
ATTN_PATTERNS = ((128, 1), (512, 4), (2048, 16))
ATTN_RADIUS = 64
ROPE_THETA = 500000.0
ROT_DIM = ATTN_HEAD_DIM // 4
MLSTM_HEADS = 4
MLSTM_HEAD_DIM = 128
LANES = 128
ATT_BLK = 128
ATT_REGION = 16
MLSTM_CHUNK = 128
VMEM_LIMIT = 56 * 1024 * 1024
ROW_TILE = 512

_NEG = float("-inf")


def _rms(x, g):
    ms = jnp.sum(x * x, axis=-1, keepdims=True) * (1.0 / x.shape[-1])
    return x * lax.rsqrt(ms + EPS) * g


def _const_spec(shape):
    nd = len(shape)
    return pl.BlockSpec(shape, lambda *_: (0,) * nd, pipeline_mode=pl.Buffered(1))


def _inproj_kernel(x_ref, pos_ref, g_ref, invf_ref, sgn_ref, w_ref,
                   q_ref, k_ref, v_ref, u_ref, vm_ref, o_ref, gate_ref):
    aw3 = 3 * q_ref.shape[1]
    mw3 = 3 * u_ref.shape[1]
    h = _rms(x_ref[...], g_ref[...]).astype(BF16)
    ang = pos_ref[...].astype(F32) * invf_ref[...]
    cos = jnp.cos(ang)
    sin = jnp.sin(ang) * sgn_ref[...]
    lane = lax.broadcasted_iota(jnp.int32, (1, LANES), 1)
    first_half = (lane % ATTN_HEAD_DIM) < (ROT_DIM // 2)

    def rotary(z):
        up = pltpu.roll(z, LANES - ROT_DIM // 2, 1)
        dn = pltpu.roll(z, ROT_DIM // 2, 1)
        return z * cos + jnp.where(first_half, up, dn) * sin

    zqkv = jnp.dot(h, w_ref[:, :aw3], preferred_element_type=F32)
    aw = zqkv.shape[1] // 3
    for j in range(aw // LANES):
        sl = slice(j * LANES, (j + 1) * LANES)
        q_ref[:, sl] = rotary(zqkv[:, j * LANES:(j + 1) * LANES]) * (ATTN_HEAD_DIM ** -0.5)
        k_ref[:, sl] = rotary(zqkv[:, aw + j * LANES:aw + (j + 1) * LANES])
    v_ref[...] = zqkv[:, 2 * aw:]
    zuvo = jnp.dot(h, w_ref[:, aw3:aw3 + mw3], preferred_element_type=F32)
    mw = mw3 // 3
    u_ref[...] = zuvo[:, :mw]
    vm_ref[...] = zuvo[:, mw:2 * mw].astype(BF16)
    o_ref[...] = zuvo[:, 2 * mw:]
    gate_ref[...] = jnp.dot(h, w_ref[:, aw3 + mw3:], preferred_element_type=F32)


def _inproj(x2, pos2, g, invf, sgn, w, aw, mw, tm):
    T, D = x2.shape
    ng = w.shape[1] - 3 * aw - 3 * mw
    row = lambda n: pl.BlockSpec((tm, n), lambda i: (i, 0))
    return pl.pallas_call(
        _inproj_kernel,
        grid=(T // tm,),
        in_specs=[row(D), row(1), _const_spec((1, D)), _const_spec((1, LANES)), _const_spec((1, LANES)),
                  _const_spec(w.shape)],
        out_specs=[row(aw), row(aw), row(aw), row(mw), row(mw), row(mw), row(ng)],
        out_shape=[jax.ShapeDtypeStruct((T, aw), F32)] * 3
        + [jax.ShapeDtypeStruct((T, mw), F32), jax.ShapeDtypeStruct((T, mw), BF16),
           jax.ShapeDtypeStruct((T, mw), F32), jax.ShapeDtypeStruct((T, ng), F32)],
        compiler_params=pltpu.CompilerParams(dimension_semantics=("arbitrary",), vmem_limit_bytes=VMEM_LIMIT),
        name="inproj",
    )(x2, pos2, g, invf, sgn, w)


def _attn_bias():
    W, R, Q = 2 * ATT_BLK, ATTN_RADIUS, ATT_BLK
    kj = jnp.arange(W)[:, None]
    qi = jnp.arange(Q)[None, :]
    interior = (kj - qi >= 0) & (kj - qi <= 2 * R)
    near = jnp.abs(kj - qi) <= R
    edge = ((qi < Q // 2) & (kj < Q) & near) | ((qi >= Q // 2) & (kj >= Q) & (jnp.abs(kj - Q - qi) <= R))
    single = near & (kj < Q)
    return jnp.where(jnp.stack([interior, edge, single]), 0.0, _NEG).astype(F32)


def _attn_kernel(q_ref, k_ref, v_ref, bias_ref, out_ref, qc_ref, kc_ref, vt0_ref, vt1_ref, o_scr, l_scr):
    S = q_ref.shape[0]
    Q, HD = ATT_BLK, ATTN_HEAD_DIM
    lane = lax.broadcasted_iota(jnp.int32, (1, LANES), 1)
    head0 = lane < HD
    npat = len(ATTN_PATTERNS)
    vt0_ref[:, HD:, :] = jnp.ones((npat, HD, S), BF16)
    vt1_ref[:, :HD, :] = jnp.ones((npat, HD, S), BF16)

    def rows_of(r, start, size, d):
        return pl.ds(start, size) if d == 1 else pl.ds(r + d * start, size, stride=d)

    def cat(parts, axis):
        return parts[0] if len(parts) == 1 else jnp.concatenate(parts, axis=axis)

    blocks = []
    for pi, (_, d) in enumerate(ATTN_PATTERNS):
        M = S // d
        nb = M // Q
        for r in range(d):
            base = r * M
            tok = rows_of(r, 0, M, d)
            qc_ref[pi, base:base + M, :] = q_ref[tok, :].astype(BF16)
            kc_ref[pi, base:base + M, :] = k_ref[tok, :].astype(BF16)
            vt = v_ref[tok, :].T.astype(BF16)
            vt0_ref[pi, :HD, base:base + M] = vt[:HD]
            vt1_ref[pi, HD:, base:base + M] = vt[HD:]
            if nb == 1:
                blocks.append(dict(pi=pi, q=[slice(base, base + Q)], k=[slice(base, base + Q)], bias=2, W=Q,
                                   out=[(tok, slice(0, Q))]))
                continue
            for i in range(nb - 1):
                k0 = base + i * Q
                blocks.append(dict(pi=pi, q=[slice(k0 + Q // 2, k0 + Q // 2 + Q)], k=[slice(k0, k0 + 2 * Q)],
                                   bias=0, W=2 * Q, out=[(rows_of(r, i * Q + Q // 2, Q, d), slice(0, Q))]))
            blocks.append(dict(pi=pi, q=[slice(base, base + Q // 2), slice(base + M - Q // 2, base + M)],
                               k=[slice(base, base + Q), slice(base + M - Q, base + M)], bias=1, W=2 * Q,
                               out=[(rows_of(r, 0, Q // 2, d), slice(0, Q // 2)),
                                    (rows_of(r, M - Q // 2, Q // 2, d), slice(Q // 2, Q))]))

    def run_region(region):
        scores = []
        for b in region:
            qb = cat([qc_ref[b["pi"], s, :] for s in b["q"]], 0)
            kw = cat([kc_ref[b["pi"], s, :] for s in b["k"]], 0)
            bias = bias_ref[b["bias"]][:b["W"]]
            for hh in range(2):
                qm = jnp.where(head0 if hh == 0 else jnp.logical_not(head0), qb, jnp.zeros_like(qb))
                scores.append(lax.dot_general(kw, qm, (((1,), (1,)), ((), ())), preferred_element_type=F32) + bias)
        maxes = [jnp.max(s, axis=0, keepdims=True) for s in scores]
        probs = [jnp.exp(s - m).astype(BF16) for s, m in zip(scores, maxes)]
        accs = []
        for bi, b in enumerate(region):
            for hh, vref in enumerate((vt0_ref, vt1_ref)):
                vw = cat([vref[b["pi"], :, s] for s in b["k"]], 1)
                accs.append(jnp.dot(vw, probs[2 * bi + hh], preferred_element_type=F32))
        for bi, b in enumerate(region):
            o_t, l_t = [], []
            for hh in range(2):
                a = accs[2 * bi + hh]
                l = a[HD:HD + 1] if hh == 0 else a[0:1]
                o_t.append((a[:HD] if hh == 0 else a[HD:]) * (1.0 / l))
                l_t.append(jnp.broadcast_to(maxes[2 * bi + hh] + jnp.log(l), (HD, Q)))
            o = jnp.concatenate(o_t, axis=0).T
            lse = jnp.concatenate(l_t, axis=0).T
            for rows, sl in b["out"]:
                o_scr[b["pi"], rows, :] = o[sl]
                l_scr[b["pi"], rows, :] = lse[sl]

    for g in range(0, len(blocks), ATT_REGION):
        run_region(blocks[g:g + ATT_REGION])

    lse = l_scr[...]
    mx = jnp.max(lse, axis=0)
    w = jnp.exp(lse - mx[None])
    out = jnp.sum(w * o_scr[...], axis=0) / jnp.sum(w, axis=0)
    out_ref[...] = out.astype(out_ref.dtype)


def _attn(q, k, v, bias):
    B, S, AW = q.shape
    blk = pl.BlockSpec((None, S, LANES), lambda b, j: (b, 0, j))
    npat = len(ATTN_PATTERNS)
    return pl.pallas_call(
        _attn_kernel,
        grid=(B, AW // LANES),
        in_specs=[blk, blk, blk, _const_spec(bias.shape)],
        out_specs=blk,
        out_shape=jax.ShapeDtypeStruct((B, S, AW), BF16),
        scratch_shapes=[pltpu.VMEM((npat, S, LANES), BF16), pltpu.VMEM((npat, S, LANES), BF16),
                        pltpu.VMEM((npat, LANES, S), BF16), pltpu.VMEM((npat, LANES, S), BF16),
                        pltpu.VMEM((npat, S, LANES), F32), pltpu.VMEM((npat, S, LANES), F32)],
        compiler_params=pltpu.CompilerParams(dimension_semantics=("arbitrary", "arbitrary"),
                                             vmem_limit_bytes=VMEM_LIMIT),
        name="dilated_attn",
    )(q, k, v, bias)


def _split2(x):
    hi = x.astype(BF16)
    return hi, (x - hi.astype(F32)).astype(BF16)


def _log_sigmoid(x):
    return jnp.minimum(x, 0.0) - jnp.log1p(jnp.exp(-jnp.abs(x)))


def _mlstm_kernel(bias_ref, u_ref, vm_ref, o_ref, grow_ref, cw_ref, cb_ref, wq_ref, wk_ref, gn_ref,
                  out_ref, q_scr, kt_scr, va_scr, hs_scr, lf_scr, b_scr, i_scr, w_scr, mp_scr, dc_scr,
                  e_scr, a_scr, f_scr, p_scr, nd_scr, cp_scr):
    S, dh = u_ref.shape
    L = MLSTM_CHUNK
    nc = S // L
    hd = pl.program_id(1)

    u = u_ref[...]
    rows = lax.broadcasted_iota(jnp.int32, (S, 1), 0)
    u_prev = jnp.where(rows == 0, 0.0, pltpu.roll(u, 1, 0))
    u_next = jnp.where(rows == S - 1, 0.0, pltpu.roll(u, S - 1, 0))
    cw = cw_ref[...]
    a = u_prev * cw[0:1] + u * cw[1:2] + u_next * cw[2:3] + cb_ref[...]
    uc = (a * jax.nn.sigmoid(a)).astype(BF16)
    q_scr[...] = jnp.dot(uc, wq_ref[...], preferred_element_type=F32).astype(BF16)
    k = jnp.dot(uc, wk_ref[...], preferred_element_type=F32) * (dh ** -0.5)
    for c in range(nc):
        kt_scr[c] = k[c * L:(c + 1) * L].T.astype(BF16)
    va_scr[:, :dh] = vm_ref[...]
    va_scr[:, dh:] = jnp.ones((S, dh), BF16)

    ri = lax.broadcasted_iota(jnp.int32, (L, L), 0)
    ci = lax.broadcasted_iota(jnp.int32, (L, L), 1)
    lower = ri >= ci
    upper = ri <= ci

    for di in range(2):
        b_i = bias_ref[0, di, hd]
        b_f = bias_ref[1, di, hd]
        irow = grow_ref[di] + b_i
        lf = _log_sigmoid(grow_ref[2 + di] + b_f)
        tri = (upper if di == 0 else lower).astype(BF16)
        brow = sum(jnp.dot(part, tri, preferred_element_type=F32) for part in _split2(lf))
        g = jnp.sum(lf, axis=1, keepdims=True)
        logw = g - brow + irow
        gb = jnp.broadcast_to(g, (nc, dh))
        mxb = jnp.broadcast_to(jnp.max(logw, axis=1, keepdims=True), (nc, dh))
        m = jnp.zeros((1, dh), F32)
        mprev, mnew = [None] * nc, [None] * nc
        for step in range(nc):
            c = step if di == 0 else nc - 1 - step
            mprev[c] = m
            m = jnp.maximum(gb[c:c + 1] + m, mxb[c:c + 1])
            mnew[c] = m
        mprev = jnp.concatenate(mprev, axis=0)
        mnew = jnp.concatenate(mnew, axis=0)
        lf_scr[di] = lf
        b_scr[di] = brow
        i_scr[di] = irow
        w_scr[di] = jnp.exp(logw - mnew)
        mp_scr[di] = mprev
        dc_scr[di] = jnp.exp(gb + mprev - mnew)

    ones_rhs = jnp.ones((2 * L, dh), BF16)
    chunk = lambda c: slice(c * L, (c + 1) * L)

    for c in range(nc):
        for di in range(2):
            lfr, brow, irow, mprev = (ref[di, c:c + 1, :] for ref in (lf_scr, b_scr, i_scr, mp_scr))
            causal = lower if di == 0 else upper
            hi, lo = _split2(jnp.where(causal, lfr, 0.0))
            bcol = jnp.dot(jnp.concatenate([hi, lo], axis=1), ones_rhs, preferred_element_type=F32)
            dmat = jnp.where(causal, bcol - brow + irow, _NEG)
            m_inter = bcol + mprev
            m_t = jnp.maximum(m_inter, jnp.max(dmat, axis=1, keepdims=True))
            e_scr[di, c] = jnp.exp(dmat - m_t)
            a_scr[di, c] = jnp.exp(m_inter - m_t)
            f_scr[di, c] = jnp.exp(-m_t)

    for c in range(nc):
        qk = jnp.dot(q_scr[chunk(c), :], kt_scr[c], preferred_element_type=F32)
        for di in range(2):
            p_scr[di, c] = (qk * e_scr[di, c]).astype(BF16)

    states = [jnp.zeros((dh, 2 * dh), F32)] * 2
    for step in range(nc):
        for di in range(2):
            c = step if di == 0 else nc - 1 - step
            va = va_scr[chunk(c), :]
            nd_scr[di, c] = jnp.dot(p_scr[di, c], va, preferred_element_type=F32)
            cp_scr[di, c] = states[di].astype(BF16)
            ktw = (kt_scr[c].astype(F32) * w_scr[di, c:c + 1, :]).astype(BF16)
            decay = dc_scr[di, c:c + 1, :]
            states[di] = jnp.concatenate([decay, decay], axis=1) * states[di] \
                + jnp.dot(ktw, va, preferred_element_type=F32)

    for c in range(nc):
        qc = q_scr[chunk(c), :]
        hsum = None
        for di in range(2):
            a_in = a_scr[di, c]
            nd = jnp.concatenate([a_in, a_in], axis=1) * jnp.dot(qc, cp_scr[di, c], preferred_element_type=F32) \
                + nd_scr[di, c]
            hval = nd[:, :dh] * (1.0 / jnp.maximum(jnp.abs(nd[:, dh:]), f_scr[di, c]))
            hsum = hval if hsum is None else hsum + hval
        hs_scr[chunk(c), :] = hsum

    hs = hs_scr[...]
    mu = jnp.sum(hs, axis=1, keepdims=True) * (1.0 / dh)
    cen = hs - mu
    var = jnp.sum(cen * cen, axis=1, keepdims=True) * (1.0 / dh)
    hn = cen * lax.rsqrt(var + EPS)
    out_ref[...] = (hn * gn_ref[...] * jax.nn.sigmoid(o_ref[...])).astype(out_ref.dtype)


def _mlstm(gate_bias, u, vm, o, grow, cw, cb, wq, wk, gn):
    B, S, MW = u.shape
    H, dh = MLSTM_HEADS, MLSTM_HEAD_DIM
    L = MLSTM_CHUNK
    assert L == dh == LANES
    nc = S // L
    tok = pl.BlockSpec((None, S, dh), lambda b, h: (b, 0, h))
    return pl.pallas_call(
        _mlstm_kernel,
        grid=(B, H),
        in_specs=[pl.BlockSpec(memory_space=pltpu.SMEM), tok, tok, tok,
                  pl.BlockSpec((None, None, 4, nc, L), lambda b, h: (b, h, 0, 0, 0)),
                  pl.BlockSpec((3, dh), lambda b, h: (0, h)), pl.BlockSpec((1, dh), lambda b, h: (0, h)),
                  pl.BlockSpec((None, dh, dh), lambda b, h: (h, 0, 0)),
                  pl.BlockSpec((None, dh, dh), lambda b, h: (h, 0, 0)),
                  pl.BlockSpec((1, dh), lambda b, h: (0, h))],
        out_specs=tok,
        out_shape=jax.ShapeDtypeStruct((B, S, MW), BF16),
        scratch_shapes=[pltpu.VMEM((S, dh), BF16), pltpu.VMEM((nc, dh, L), BF16), pltpu.VMEM((S, 2 * dh), BF16),
                        pltpu.VMEM((S, dh), F32)] + [pltpu.VMEM((2, nc, L), F32)] * 6
        + [pltpu.VMEM((2, nc, L, L), F32), pltpu.VMEM((2, nc, L, dh), F32), pltpu.VMEM((2, nc, L, dh), F32),
           pltpu.VMEM((2, nc, L, L), BF16), pltpu.VMEM((2, nc, L, 2 * dh), F32),
           pltpu.VMEM((2, nc, dh, 2 * dh), BF16)],
        compiler_params=pltpu.CompilerParams(dimension_semantics=("arbitrary", "arbitrary"),
                                             vmem_limit_bytes=VMEM_LIMIT),
        name="mlstm",
    )(gate_bias, u, vm, o, grow, cw, cb, wq, wk, gn)


HALO = 16
FFN_CHUNK = 256
FFN_AHEAD = 2


def _ffn_kernel(x_ref, xp_ref, xn_ref, a_ref, ap_ref, an_ref, m_ref, mp_ref, mn_ref, p_ref,
                wo_ref, g2_ref, wg_ref, wu_ref, wd_ref, cw_ref, cb_ref, gp_ref, wpg_ref, bpg_ref, wpp_ref, gf_ref,
                out_ref, mix_ref, hcat_ref, x1_ref, a_scr, acc_ref, *, tiles_per_seq):
    tm, aw = a_ref.shape
    ck = FFN_CHUNK
    nck = wg_ref.shape[1] // ck
    i = pl.program_id(0)
    first = (i % tiles_per_seq) == 0
    last = (i % tiles_per_seq) == tiles_per_seq - 1

    for lo, aa, mm in ((0, ap_ref, mp_ref), (HALO, a_ref, m_ref), (HALO + tm, an_ref, mn_ref)):
        mix_ref[lo:lo + aa.shape[0], :aw] = aa[...]
        mix_ref[lo:lo + aa.shape[0], aw:] = mm[...]
    y = jnp.dot(mix_ref[...], wo_ref[...], preferred_element_type=F32)
    x1 = x_ref[...] + y[HALO:HALO + tm]
    x1_ref[...] = x1
    g2 = g2_ref[...]
    hcat_ref[HALO:HALO + tm, :] = _rms(x1, g2).astype(BF16)
    hp = _rms(xp_ref[...] + y[:HALO], g2).astype(BF16)
    hn = _rms(xn_ref[...] + y[HALO + tm:], g2).astype(BF16)
    hcat_ref[0:HALO, :] = jnp.where(first, jnp.zeros_like(hp), hp)
    hcat_ref[HALO + tm:, :] = jnp.where(last, jnp.zeros_like(hn), hn)
    acc_ref[...] = jnp.zeros_like(acc_ref)

    nbuf = a_scr.shape[0]

    def gate_up(c):
        cols = slice(c * ck, (c + 1) * ck)
        a_scr[c % nbuf] = jnp.dot(hcat_ref[...], wg_ref[:, cols], preferred_element_type=F32)
        return jnp.dot(hcat_ref[HALO:HALO + tm, :], wu_ref[:, cols], preferred_element_type=F32)

    ups = [gate_up(c) for c in range(FFN_AHEAD)]
    for c in range(nck):
        if c + FFN_AHEAD < nck:
            ups.append(gate_up(c + FFN_AHEAD))
        cols = slice(c * ck, (c + 1) * ck)
        a_buf = a_scr.at[c % nbuf]
        cw = cw_ref[:, cols]
        a = a_buf[pl.ds(HALO - 1, tm), :] * cw[0:1] + a_buf[pl.ds(HALO, tm), :] * cw[1:2] \
            + a_buf[pl.ds(HALO + 1, tm), :] * cw[2:3] + cb_ref[:, cols]
        act = 0.5 * a * (1.0 + lax.erf(a * (2.0 ** -0.5))) * ups[c]
        acc_ref[...] += jnp.dot(act.astype(BF16), wd_ref[cols, :], preferred_element_type=F32)
    x2 = x1_ref[...] + acc_ref[...]
    h3 = _rms(x2, gp_ref[...]).astype(BF16)
    gate = jax.nn.sigmoid(jnp.dot(h3, wpg_ref[...], preferred_element_type=F32) + bpg_ref[...])
    pp = jnp.dot(p_ref[...].astype(BF16), wpp_ref[...], preferred_element_type=F32)
    out_ref[...] = _rms(x2 + pp * gate, gf_ref[...])


def _ffn(x2d, attn, mem, p2, wo, g2, wg, wu, wd, cw, cb, gp, wpg, bpg, wpp, gf, tm, S):
    T, D = x2d.shape
    row = lambda w: pl.BlockSpec((tm, w), lambda i: (i, 0))
    hb = tm // HALO
    nhb = T // HALO
    prev = lambda w: pl.BlockSpec((HALO, w), lambda i: (jnp.maximum(i * hb - 1, 0), 0))
    nxt = lambda w: pl.BlockSpec((HALO, w), lambda i: (jnp.minimum((i + 1) * hb, nhb - 1), 0))
    aw, mw = attn.shape[1], mem.shape[1]
    consts = (wo, g2, wg, wu, wd, cw, cb, gp, wpg, bpg, wpp, gf)
    return pl.pallas_call(
        functools.partial(_ffn_kernel, tiles_per_seq=S // tm),
        grid=(T // tm,),
        in_specs=[row(D), prev(D), nxt(D), row(aw), prev(aw), nxt(aw), row(mw), prev(mw), nxt(mw),
                  row(p2.shape[1])] + [_const_spec(c.shape) for c in consts],
        out_specs=row(D),
        out_shape=jax.ShapeDtypeStruct((T, D), F32),
        scratch_shapes=[pltpu.VMEM((tm + 2 * HALO, aw + mw), BF16), pltpu.VMEM((tm + 2 * HALO, D), BF16),
                        pltpu.VMEM((tm, D), F32), pltpu.VMEM((FFN_AHEAD + 1, tm + 2 * HALO, FFN_CHUNK), F32),
                        pltpu.VMEM((tm, D), F32)],
        compiler_params=pltpu.CompilerParams(dimension_semantics=("arbitrary",), vmem_limit_bytes=VMEM_LIMIT),
        name="outproj_ffn_ple",
    )(x2d, x2d, x2d, attn, attn, attn, mem, mem, mem, p2, *consts)


def kernel(x, p, positions, ln_mix_g, w_in, mlstm_conv_w, mlstm_conv_b, w_mq, w_mk, b_igate, b_fgate,
           mlstm_gn_g, w_out, ln_ffn_g, w_ffn_gate, ffn_conv_w, ffn_conv_b, w_ffn_up, w_ffn_down,
           ln_ple_g, w_ple_gate, b_ple_gate, w_ple_proj, ln_final_g):
    B, S, D = x.shape
    depth = w_in.shape[0]
    T = B * S
    H, dh = MLSTM_HEADS, MLSTM_HEAD_DIM
    aw = D // 2
    mw = D - aw
    tm = ROW_TILE
    L = MLSTM_CHUNK
    nc = S // L

    half = ROT_DIM // 2
    lane = jnp.arange(LANES) % ATTN_HEAD_DIM
    inv = jnp.power(ROPE_THETA, -jnp.arange(0, ROT_DIM, 2, dtype=F32) / ROT_DIM)
    invf = jnp.where(lane < ROT_DIM, inv[lane % half], 0.0).astype(F32)[None]
    sgn = jnp.where(lane < half, -1.0, jnp.where(lane < ROT_DIM, 1.0, 0.0)).astype(F32)[None]
    bias = _attn_bias()
    pos2 = positions.reshape(T, 1)

    assert depth == 1, "the final norm is fused into the layer's last kernel"
    xf = x.reshape(T, D)
    for i in range(depth):
        q, k, v, u, vm, o, gates = _inproj(xf, pos2, ln_mix_g[i][None], invf, sgn, w_in[i].astype(BF16), aw, mw, tm)

        attn = _attn(q.reshape(B, S, aw), k.reshape(B, S, aw), v.reshape(B, S, aw), bias)

        grow = gates.reshape(B, S, 4, H).transpose(0, 3, 2, 1).reshape(B, H, 4, nc, L)
        gate_bias = jnp.stack([b_igate[i], b_fgate[i]]).astype(F32)
        mem = _mlstm(gate_bias, u.reshape(B, S, mw), vm.reshape(B, S, mw), o.reshape(B, S, mw), grow,
                     mlstm_conv_w[i], mlstm_conv_b[i][None], w_mq[i].astype(BF16), w_mk[i].astype(BF16),
                     mlstm_gn_g[i][None])

        assert w_ffn_gate.shape[-1] % FFN_CHUNK == 0
        xf = _ffn(xf, attn.reshape(T, aw), mem.reshape(T, mw), p[i].reshape(T, -1),
                  w_out[i].astype(BF16), ln_ffn_g[i][None],
                  w_ffn_gate[i].astype(BF16), w_ffn_up[i].astype(BF16), w_ffn_down[i].astype(BF16),
                  ffn_conv_w[i], ffn_conv_b[i][None], ln_ple_g[i][None],
                  w_ple_gate[i].astype(BF16), b_ple_gate[i][None], w_ple_proj[i].astype(BF16), ln_final_g[None], tm, S)
    return xf.reshape(B, S, D)
```

```python
import functools
import math

import jax
import jax.numpy as jnp
import numpy as np
from jax import lax
from jax.experimental import pallas as pl
from jax.experimental.pallas import tpu as pltpu

F32 = jnp.float32
BF16 = jnp.bfloat16

EPS = 1e-6
ATTN_HEAD_DIM = 64
ATTN_PATTERNS = ((128, 1), (512, 4), (2048, 16))
ATTN_RADIUS = 64
ROPE_THETA = 500000.0
ROT_DIM = ATTN_HEAD_DIM // 4
MLSTM_HEADS = 4
MLSTM_HEAD_DIM = 128
LANES = 128
ATT_BLK = 128
ATT_REGION = 8
MLSTM_CHUNK = 128
VMEM_LIMIT = 56 * 1024 * 1024
ROW_TILE = 512

_NEG = float("-inf")
LOG2E = math.log2(math.e)


def _rms(x, g):
    ms = jnp.sum(x * x, axis=-1, keepdims=True) * (1.0 / x.shape[-1])
    return x * lax.rsqrt(ms + EPS) * g


def _const_spec(shape):
    nd = len(shape)
    return pl.BlockSpec(shape, lambda *_: (0,) * nd, pipeline_mode=pl.Buffered(1))


def _inproj_kernel(x_ref, pos_ref, g_ref, invc_ref, sel_ref, unrot_ref, w_ref,
                   q_ref, k_ref, v_ref, u_ref, vm_ref, o_ref, gate_ref):
    aw3 = 3 * q_ref.shape[1]
    mw3 = 3 * u_ref.shape[1]
    ang = invc_ref[...] * pos_ref[...].astype(F32)
    half_rows = lax.broadcasted_iota(jnp.int32, ang.shape, 0) < ROT_DIM // 2
    tab = jnp.where(half_rows, jnp.cos(ang), jnp.sin(ang))
    cs = sum(lax.dot_general(part, sel_ref[...], (((0,), (0,)), ((), ())), preferred_element_type=F32)
             for part in _split2(tab))
    cos = cs[:, :LANES] + unrot_ref[...]
    sin = cs[:, LANES:]
    x = x_ref[...]
    h = (x * g_ref[...]).astype(BF16)
    rs = lax.rsqrt(jnp.sum(x * x, axis=-1, keepdims=True) * (1.0 / x.shape[-1]) + EPS)
    lane = lax.broadcasted_iota(jnp.int32, (1, LANES), 1)
    first_half = (lane % ATTN_HEAD_DIM) < (ROT_DIM // 2)

    def rotary(z):
        up = pltpu.roll(z, LANES - ROT_DIM // 2, 1)
        dn = pltpu.roll(z, ROT_DIM // 2, 1)
        return z * cos + jnp.where(first_half, up, dn) * sin

    zqkv = jnp.dot(h, w_ref[:, :aw3], preferred_element_type=F32) * rs
    aw = zqkv.shape[1] // 3
    q_scale = ATTN_HEAD_DIM ** -0.5 * LOG2E
    for j in range(aw // LANES):
        sl = slice(j * LANES, (j + 1) * LANES)
        q_ref[:, sl] = rotary(zqkv[:, j * LANES:(j + 1) * LANES]) * q_scale
        k_ref[:, sl] = rotary(zqkv[:, aw + j * LANES:aw + (j + 1) * LANES])
    v_ref[...] = zqkv[:, 2 * aw:]
    zuvo = jnp.dot(h, w_ref[:, aw3:aw3 + mw3], preferred_element_type=F32) * rs
    mw = mw3 // 3
    u_ref[...] = zuvo[:, :mw]
    vm_ref[...] = zuvo[:, mw:2 * mw].astype(BF16)
    o_ref[...] = zuvo[:, 2 * mw:]
    gate_ref[...] = jnp.dot(h, w_ref[:, aw3 + mw3:], preferred_element_type=F32) * rs


def _rotary_tables():
    half = ROT_DIM // 2
    inv = jnp.power(ROPE_THETA, -jnp.arange(0, ROT_DIM, 2, dtype=F32) / ROT_DIM)
    invc = jnp.concatenate([inv, inv])[:, None]
    sel = np.zeros((2 * half, 2 * LANES), np.float32)
    unrot = np.ones((1, LANES), np.float32)
    for l in range(LANES):
        hl = l % ATTN_HEAD_DIM
        if hl < ROT_DIM:
            sel[hl % half, l] = 1.0
            sel[half + hl % half, LANES + l] = -1.0 if hl < half else 1.0
            unrot[0, l] = 0.0
    return invc, jnp.asarray(sel, BF16), jnp.asarray(unrot)


def _inproj(x2, pos3, g, w, aw, mw, tm):
    T, D = x2.shape
    ng = w.shape[1] - 3 * aw - 3 * mw
    row = lambda n: pl.BlockSpec((tm, n), lambda i: (i, 0))
    invc, sel, unrot = _rotary_tables()
    return pl.pallas_call(
        _inproj_kernel,
        grid=(T // tm,),
        in_specs=[row(D), pl.BlockSpec((None, 1, tm), lambda i: (i, 0, 0)), _const_spec((1, D)),
                  _const_spec(invc.shape), _const_spec(sel.shape), _const_spec(unrot.shape), _const_spec(w.shape)],
        out_specs=[row(aw), row(aw), row(aw), row(mw), row(mw), row(mw), row(ng)],
        out_shape=[jax.ShapeDtypeStruct((T, aw), F32)] * 3
        + [jax.ShapeDtypeStruct((T, mw), F32), jax.ShapeDtypeStruct((T, mw), BF16),
           jax.ShapeDtypeStruct((T, mw), F32), jax.ShapeDtypeStruct((T, ng), F32)],
        compiler_params=pltpu.CompilerParams(dimension_semantics=("arbitrary",), vmem_limit_bytes=VMEM_LIMIT),
        name="inproj",
    )(x2, pos3, g, invc, sel, unrot, w)


def _attn_bias():
    W, R, Q = 2 * ATT_BLK, ATTN_RADIUS, ATT_BLK
    kj = np.arange(W)[:, None]
    qi = np.arange(Q)[None, :]
    interior = (kj - qi >= 0) & (kj - qi <= 2 * R)
    near = np.abs(kj - qi) <= R
    edge = ((qi < Q // 2) & (kj < Q) & near) | ((qi >= Q // 2) & (kj >= Q) & (np.abs(kj - Q - qi) <= R))
    single = near & (kj < Q)
    return jnp.asarray(np.where(np.stack([interior, edge, single]), 0.0, _NEG).astype(np.float32))


def _attn_kernel(q_ref, k_ref, v_ref, bias_ref, out_ref, qc_ref, kc_ref, vt0_ref, vt1_ref, o_scr, l_scr):
    S = q_ref.shape[0]
    Q, HD = ATT_BLK, ATTN_HEAD_DIM
    lane = lax.broadcasted_iota(jnp.int32, (1, LANES), 1)
    head0 = lane < HD
    npat = len(ATTN_PATTERNS)
    vt0_ref[:, HD:, :] = jnp.ones((npat, HD, S), BF16)
    vt1_ref[:, :HD, :] = jnp.ones((npat, HD, S), BF16)

    def rows_of(r, start, size, d):
        return pl.ds(start, size) if d == 1 else pl.ds(r + d * start, size, stride=d)

    def cat(parts, axis):
        return parts[0] if len(parts) == 1 else jnp.concatenate(parts, axis=axis)

    blocks = []
    for pi, (_, d) in enumerate(ATTN_PATTERNS):
        M = S // d
        nb = M // Q
        for r in range(d):
            base = r * M
            tok = rows_of(r, 0, M, d)
            qc_ref[pi, base:base + M, :] = q_ref[tok, :].astype(BF16)
            kc_ref[pi, base:base + M, :] = k_ref[tok, :].astype(BF16)
            vt = v_ref[tok, :].T.astype(BF16)
            vt0_ref[pi, :HD, base:base + M] = vt[:HD]
            vt1_ref[pi, HD:, base:base + M] = vt[HD:]
            if nb == 1:
                blocks.append(dict(pi=pi, q=[slice(base, base + Q)], k=[slice(base, base + Q)], bias=2, W=Q,
                                   out=[(tok, slice(0, Q))]))
                continue
            for i in range(nb - 1):
                k0 = base + i * Q
                blocks.append(dict(pi=pi, q=[slice(k0 + Q // 2, k0 + Q // 2 + Q)], k=[slice(k0, k0 + 2 * Q)],
                                   bias=0, W=2 * Q, out=[(rows_of(r, i * Q + Q // 2, Q, d), slice(0, Q))]))
            blocks.append(dict(pi=pi, q=[slice(base, base + Q // 2), slice(base + M - Q // 2, base + M)],
                               k=[slice(base, base + Q), slice(base + M - Q, base + M)], bias=1, W=2 * Q,
                               out=[(rows_of(r, 0, Q // 2, d), slice(0, Q // 2)),
                                    (rows_of(r, M - Q // 2, Q // 2, d), slice(Q // 2, Q))]))

    def run_region(region):
        scores = []
        for b in region:
            qb = cat([qc_ref[b["pi"], s, :] for s in b["q"]], 0)
            kw = cat([kc_ref[b["pi"], s, :] for s in b["k"]], 0)
            bias = bias_ref[b["bias"]][:b["W"]]
            for hh in range(2):
                qm = jnp.where(head0 if hh == 0 else jnp.logical_not(head0), qb, jnp.zeros_like(qb))
                scores.append(lax.dot_general(kw, qm, (((1,), (1,)), ((), ())), preferred_element_type=F32) + bias)
        maxes = [jnp.max(s, axis=0, keepdims=True) for s in scores]
        probs = [jnp.exp2(s - m).astype(BF16) for s, m in zip(scores, maxes)]
        accs = []
        for bi, b in enumerate(region):
            for hh, vref in enumerate((vt0_ref, vt1_ref)):
                vw = cat([vref[b["pi"], :, s] for s in b["k"]], 1)
                accs.append(jnp.dot(vw, probs[2 * bi + hh], preferred_element_type=F32))
        for bi, b in enumerate(region):
            o_t, l_t = [], []
            for hh in range(2):
                a = accs[2 * bi + hh]
                l = a[HD:HD + 1] if hh == 0 else a[0:1]
                o_t.append((a[:HD] if hh == 0 else a[HD:]) * (1.0 / l))
                l_t.append(jnp.broadcast_to(maxes[2 * bi + hh] + jnp.log2(l), (HD, Q)))
            o = jnp.concatenate(o_t, axis=0).T
            lse = jnp.concatenate(l_t, axis=0).T
            for rows, sl in b["out"]:
                o_scr[b["pi"], rows, :] = o[sl]
                l_scr[b["pi"], rows, :] = lse[sl]

    for g in range(0, len(blocks), ATT_REGION):
        run_region(blocks[g:g + ATT_REGION])

    lse = l_scr[...]
    mx = jnp.max(lse, axis=0)
    w = jnp.exp2(lse - mx[None])
    out = jnp.sum(w * o_scr[...], axis=0) / jnp.sum(w, axis=0)
    out_ref[...] = out.astype(out_ref.dtype)


def _attn(q, k, v, bias):
    B, S, AW = q.shape
    blk = pl.BlockSpec((None, S, LANES), lambda b, j: (b, 0, j))
    npat = len(ATTN_PATTERNS)
    return pl.pallas_call(
        _attn_kernel,
        grid=(B, AW // LANES),
        in_specs=[blk, blk, blk, _const_spec(bias.shape)],
        out_specs=blk,
        out_shape=jax.ShapeDtypeStruct((B, S, AW), BF16),
        scratch_shapes=[pltpu.VMEM((npat, S, LANES), BF16), pltpu.VMEM((npat, S, LANES), BF16),
                        pltpu.VMEM((npat, LANES, S), BF16), pltpu.VMEM((npat, LANES, S), BF16),
                        pltpu.VMEM((npat, S, LANES), F32), pltpu.VMEM((npat, S, LANES), F32)],
        compiler_params=pltpu.CompilerParams(dimension_semantics=("arbitrary", "arbitrary"),
                                             vmem_limit_bytes=VMEM_LIMIT),
        name="dilated_attn",
    )(q, k, v, bias)


def _split2(x):
    hi = x.astype(BF16)
    return hi, (x - hi.astype(F32)).astype(BF16)


def _log_sigmoid(x):
    return jnp.minimum(x, 0.0) - jnp.log1p(jnp.exp(-jnp.abs(x)))


def _mlstm_kernel(bias_ref, u_ref, vm_ref, o_ref, grow_ref, cw_ref, cb_ref, wq_ref, wk_ref, gn_ref,
                  out_ref, q_scr, kt_scr, va_scr, hs_scr, lf_scr, b_scr, i_scr, w_scr, mp_scr, dc_scr,
                  e_scr, a_scr, f_scr, p_scr, nd_scr, cp_scr):
    S, dh = u_ref.shape
    L = MLSTM_CHUNK
    nc = S // L
    hd = pl.program_id(1)

    u = u_ref[...]
    rows = lax.broadcasted_iota(jnp.int32, (S, 1), 0)
    u_prev = jnp.where(rows == 0, 0.0, pltpu.roll(u, 1, 0))
    u_next = jnp.where(rows == S - 1, 0.0, pltpu.roll(u, S - 1, 0))
    cw = cw_ref[...]
    a = u_prev * cw[0:1] + u * cw[1:2] + u_next * cw[2:3] + cb_ref[...]
    uc = (a * jax.nn.sigmoid(a)).astype(BF16)
    q_scr[...] = jnp.dot(uc, wq_ref[...], preferred_element_type=F32).astype(BF16)
    k = jnp.dot(uc, wk_ref[...], preferred_element_type=F32) * (dh ** -0.5)
    for c in range(nc):
        kt_scr[c] = k[c * L:(c + 1) * L].T.astype(BF16)
    va_scr[:, :dh] = vm_ref[...]
    va_scr[:, dh:] = jnp.ones((S, dh), BF16)

    ri = lax.broadcasted_iota(jnp.int32, (L, L), 0)
    ci = lax.broadcasted_iota(jnp.int32, (L, L), 1)
    lower = ri >= ci
    upper = ri <= ci

    for di in range(2):
        b_i = bias_ref[0, di, hd]
        b_f = bias_ref[1, di, hd]
        irow = (grow_ref[di] + b_i) * LOG2E
        lf = _log_sigmoid(grow_ref[2 + di] + b_f) * LOG2E
        tri = (upper if di == 0 else lower).astype(BF16)
        brow = sum(jnp.dot(part, tri, preferred_element_type=F32) for part in _split2(lf))
        g = jnp.sum(lf, axis=1, keepdims=True)
        logw = g - brow + irow
        gb = jnp.broadcast_to(g, (nc, dh))
        mxb = jnp.broadcast_to(jnp.max(logw, axis=1, keepdims=True), (nc, dh))
        m = jnp.zeros((1, dh), F32)
        mprev, mnew = [None] * nc, [None] * nc
        for step in range(nc):
            c = step if di == 0 else nc - 1 - step
            mprev[c] = m
            m = jnp.maximum(gb[c:c + 1] + m, mxb[c:c + 1])
            mnew[c] = m
        mprev = jnp.concatenate(mprev, axis=0)
        mnew = jnp.concatenate(mnew, axis=0)
        lf_scr[di] = lf
        b_scr[di] = brow
        i_scr[di] = irow
        w_scr[di] = jnp.exp2(logw - mnew)
        mp_scr[di] = mprev
        dc_scr[di] = jnp.exp2(gb + mprev - mnew)

    ones_rhs = jnp.ones((2 * L, dh), BF16)
    chunk = lambda c: slice(c * L, (c + 1) * L)

    for c in range(nc):
        for di in range(2):
            lfr, brow, irow, mprev = (ref[di, c:c + 1, :] for ref in (lf_scr, b_scr, i_scr, mp_scr))
            causal = lower if di == 0 else upper
            hi, lo = _split2(jnp.where(causal, lfr, 0.0))
            bcol = jnp.dot(jnp.concatenate([hi, lo], axis=1), ones_rhs, preferred_element_type=F32)
            dmat = jnp.where(causal, bcol - brow + irow, _NEG)
            m_inter = bcol + mprev
            m_t = jnp.maximum(m_inter, jnp.max(dmat, axis=1, keepdims=True))
            e_scr[di, c] = jnp.exp2(dmat - m_t)
            a_scr[di, c] = jnp.exp2(m_inter - m_t)
            f_scr[di, c] = jnp.exp2(-m_t)

    for c in range(nc):
        qk = jnp.dot(q_scr[chunk(c), :], kt_scr[c], preferred_element_type=F32)
        for di in range(2):
            p_scr[di, c] = (qk * e_scr[di, c]).astype(BF16)

    states = [jnp.zeros((dh, 2 * dh), F32)] * 2
    for step in range(nc):
        for di in range(2):
            c = step if di == 0 else nc - 1 - step
            va = va_scr[chunk(c), :]
            nd_scr[di, c] = jnp.dot(p_scr[di, c], va, preferred_element_type=F32)
            cp_scr[di, c] = states[di].astype(BF16)
            ktw = (kt_scr[c].astype(F32) * w_scr[di, c:c + 1, :]).astype(BF16)
            decay = dc_scr[di, c:c + 1, :]
            states[di] = jnp.concatenate([decay, decay], axis=1) * states[di] \
                + jnp.dot(ktw, va, preferred_element_type=F32)

    for c in range(nc):
        qc = q_scr[chunk(c), :]
        hsum = None
        for di in range(2):
            a_in = a_scr[di, c]
            nd = jnp.concatenate([a_in, a_in], axis=1) * jnp.dot(qc, cp_scr[di, c], preferred_element_type=F32) \
                + nd_scr[di, c]
            hval = nd[:, :dh] * (1.0 / jnp.maximum(jnp.abs(nd[:, dh:]), f_scr[di, c]))
            hsum = hval if hsum is None else hsum + hval
        hs_scr[chunk(c), :] = hsum

    hs = hs_scr[...]
    mu = jnp.sum(hs, axis=1, keepdims=True) * (1.0 / dh)
    cen = hs - mu
    var = jnp.sum(cen * cen, axis=1, keepdims=True) * (1.0 / dh)
    hn = cen * lax.rsqrt(var + EPS)
    out_ref[...] = (hn * gn_ref[...] * jax.nn.sigmoid(o_ref[...])).astype(out_ref.dtype)


def _mlstm(gate_bias, u, vm, o, grow, cw, cb, wq, wk, gn):
    B, S, MW = u.shape
    H, dh = MLSTM_HEADS, MLSTM_HEAD_DIM
    L = MLSTM_CHUNK
    assert L == dh == LANES
    nc = S // L
    tok = pl.BlockSpec((None, S, dh), lambda b, h: (b, 0, h))
    return pl.pallas_call(
        _mlstm_kernel,
        grid=(B, H),
        in_specs=[pl.BlockSpec(memory_space=pltpu.SMEM), tok, tok, tok,
                  pl.BlockSpec((None, None, 4, nc, L), lambda b, h: (b, h, 0, 0, 0)),
                  pl.BlockSpec((3, dh), lambda b, h: (0, h)), pl.BlockSpec((1, dh), lambda b, h: (0, h)),
                  pl.BlockSpec((None, dh, dh), lambda b, h: (h, 0, 0)),
                  pl.BlockSpec((None, dh, dh), lambda b, h: (h, 0, 0)),
                  pl.BlockSpec((1, dh), lambda b, h: (0, h))],
        out_specs=tok,
        out_shape=jax.ShapeDtypeStruct((B, S, MW), BF16),
        scratch_shapes=[pltpu.VMEM((S, dh), BF16), pltpu.VMEM((nc, dh, L), BF16), pltpu.VMEM((S, 2 * dh), BF16),
                        pltpu.VMEM((S, dh), F32)] + [pltpu.VMEM((2, nc, L), F32)] * 6
        + [pltpu.VMEM((2, nc, L, L), F32), pltpu.VMEM((2, nc, L, dh), F32), pltpu.VMEM((2, nc, L, dh), F32),
           pltpu.VMEM((2, nc, L, L), BF16), pltpu.VMEM((2, nc, L, 2 * dh), F32),
           pltpu.VMEM((2, nc, dh, 2 * dh), BF16)],
        compiler_params=pltpu.CompilerParams(dimension_semantics=("arbitrary", "arbitrary"),
                                             vmem_limit_bytes=VMEM_LIMIT),
        name="mlstm",
    )(gate_bias, u, vm, o, grow, cw, cb, wq, wk, gn)


HALO = 16
FFN_CHUNK = 256


def _ffn_kernel(x_ref, xp_ref, xn_ref, a_ref, ap_ref, an_ref, m_ref, mp_ref, mn_ref, p_ref,
                wo_ref, g2_ref, wg_ref, wu_ref, wd_ref, cw_ref, cb_ref, gp_ref, wpg_ref, bpg_ref, wpp_ref, gf_ref,
                out_ref, mix_ref, hcat_ref, x1_ref, act_ref, *, tiles_per_seq):
    tm, aw = a_ref.shape
    ck = FFN_CHUNK
    nck = wg_ref.shape[1] // ck
    i = pl.program_id(0)
    first = (i % tiles_per_seq) == 0
    last = (i % tiles_per_seq) == tiles_per_seq - 1

    for lo, aa, mm in ((0, ap_ref, mp_ref), (HALO, a_ref, m_ref), (HALO + tm, an_ref, mn_ref)):
        mix_ref[lo:lo + aa.shape[0], :aw] = aa[...]
        mix_ref[lo:lo + aa.shape[0], aw:] = mm[...]
    y = jnp.dot(mix_ref[...], wo_ref[...], preferred_element_type=F32)
    x1 = x_ref[...] + y[HALO:HALO + tm]
    x1_ref[...] = x1
    g2 = g2_ref[...]
    hcat_ref[HALO:HALO + tm, :] = _rms(x1, g2).astype(BF16)
    hp = _rms(xp_ref[...] + y[:HALO], g2).astype(BF16)
    hn = _rms(xn_ref[...] + y[HALO + tm:], g2).astype(BF16)
    hcat_ref[0:HALO, :] = jnp.where(first, jnp.zeros_like(hp), hp)
    hcat_ref[HALO + tm:, :] = jnp.where(last, jnp.zeros_like(hn), hn)

    rows = tm + 2 * HALO
    for c in range(nck):
        cols = slice(c * ck, (c + 1) * ck)
        g = jnp.dot(hcat_ref[...], wg_ref[:, cols], preferred_element_type=F32)
        up = jnp.dot(hcat_ref[HALO:HALO + tm, :], wu_ref[:, cols], preferred_element_type=F32)
        cw = cw_ref[:, cols]
        a = pltpu.roll(g, 1, 0)[HALO:HALO + tm] * cw[0:1] + g[HALO:HALO + tm] * cw[1:2] \
            + pltpu.roll(g, rows - 1, 0)[HALO:HALO + tm] * cw[2:3] + cb_ref[:, cols]
        act_ref[:, cols] = (0.5 * a * (1.0 + lax.erf(a * (2.0 ** -0.5))) * up).astype(BF16)
    x2 = x1_ref[...] + jnp.dot(act_ref[...], wd_ref[...], preferred_element_type=F32)
    ms = jnp.sum(x2 * x2, axis=-1, keepdims=True) * (1.0 / x2.shape[-1])
    zg = jnp.dot((x2 * gp_ref[...]).astype(BF16), wpg_ref[...], preferred_element_type=F32)
    gate = jax.nn.sigmoid(zg * lax.rsqrt(ms + EPS) + bpg_ref[...])
    pp = jnp.dot(p_ref[...].astype(BF16), wpp_ref[...], preferred_element_type=F32)
    out_ref[...] = _rms(x2 + pp * gate, gf_ref[...])


def _ffn(x2d, attn, mem, p2, wo, g2, wg, wu, wd, cw, cb, gp, wpg, bpg, wpp, gf, tm, S):
    T, D = x2d.shape
    row = lambda w: pl.BlockSpec((tm, w), lambda i: (i, 0))
    hb = tm // HALO
    nhb = T // HALO
    prev = lambda w: pl.BlockSpec((HALO, w), lambda i: (jnp.maximum(i * hb - 1, 0), 0))
    nxt = lambda w: pl.BlockSpec((HALO, w), lambda i: (jnp.minimum((i + 1) * hb, nhb - 1), 0))
    aw, mw = attn.shape[1], mem.shape[1]
    consts = (wo, g2, wg, wu, wd, cw, cb, gp, wpg, bpg, wpp, gf)
    return pl.pallas_call(
        functools.partial(_ffn_kernel, tiles_per_seq=S // tm),
        grid=(T // tm,),
        in_specs=[row(D), prev(D), nxt(D), row(aw), prev(aw), nxt(aw), row(mw), prev(mw), nxt(mw),
                  row(p2.shape[1])] + [_const_spec(c.shape) for c in consts],
        out_specs=row(D),
        out_shape=jax.ShapeDtypeStruct((T, D), F32),
        scratch_shapes=[pltpu.VMEM((tm + 2 * HALO, aw + mw), BF16), pltpu.VMEM((tm + 2 * HALO, D), BF16),
                        pltpu.VMEM((tm, D), F32), pltpu.VMEM((tm, wg.shape[1]), BF16)],
        compiler_params=pltpu.CompilerParams(dimension_semantics=("arbitrary",), vmem_limit_bytes=VMEM_LIMIT),
        name="outproj_ffn_ple",
    )(x2d, x2d, x2d, attn, attn, attn, mem, mem, mem, p2, *consts)


def kernel(x, p, positions, ln_mix_g, w_in, mlstm_conv_w, mlstm_conv_b, w_mq, w_mk, b_igate, b_fgate,
           mlstm_gn_g, w_out, ln_ffn_g, w_ffn_gate, ffn_conv_w, ffn_conv_b, w_ffn_up, w_ffn_down,
           ln_ple_g, w_ple_gate, b_ple_gate, w_ple_proj, ln_final_g):
    B, S, D = x.shape
    depth = w_in.shape[0]
    T = B * S
    H, dh = MLSTM_HEADS, MLSTM_HEAD_DIM
    aw = D // 2
    mw = D - aw
    tm = ROW_TILE
    L = MLSTM_CHUNK
    nc = S // L

    bias = _attn_bias()
    pos3 = positions.reshape(T // tm, 1, tm)

    assert depth == 1, "the final norm is fused into the layer's last kernel"
    xf = x.reshape(T, D)
    for i in range(depth):
        q, k, v, u, vm, o, gates = _inproj(xf, pos3, ln_mix_g[i][None], w_in[i].astype(BF16), aw, mw, tm)

        attn = _attn(q.reshape(B, S, aw), k.reshape(B, S, aw), v.reshape(B, S, aw), bias)

        grow = gates.reshape(B, S, 4, H).transpose(0, 3, 2, 1).reshape(B, H, 4, nc, L)
        gate_bias = jnp.stack([b_igate[i], b_fgate[i]]).astype(F32)
        mem = _mlstm(gate_bias, u.reshape(B, S, mw), vm.reshape(B, S, mw), o.reshape(B, S, mw), grow,
                     mlstm_conv_w[i], mlstm_conv_b[i][None], w_mq[i].astype(BF16), w_mk[i].astype(BF16),
                     mlstm_gn_g[i][None])

        assert w_ffn_gate.shape[-1] % FFN_CHUNK == 0
        xf = _ffn(xf, attn.reshape(T, aw), mem.reshape(T, mw), p[i].reshape(T, -1),
                  w_out[i].astype(BF16), ln_ffn_g[i][None],
                  w_ffn_gate[i].astype(BF16), w_ffn_up[i].astype(BF16), w_ffn_down[i].astype(BF16),
                  ffn_conv_w[i], ffn_conv_b[i][None], ln_ple_g[i][None],
                  w_ple_gate[i].astype(BF16), b_ple_gate[i][None], w_ple_proj[i].astype(BF16), ln_final_g[None], tm, S)
    return xf.reshape(B, S, D)
```

```python
import functools
import math

import jax
import jax.numpy as jnp
import numpy as np
from jax import lax
from jax.experimental import pallas as pl
from jax.experimental.pallas import tpu as pltpu

F32 = jnp.float32
BF16 = jnp.bfloat16

EPS = 1e-6
ATTN_HEAD_DIM = 64
ATTN_PATTERNS = ((128, 1), (512, 4), (2048, 16))
ATTN_RADIUS = 64
ROPE_THETA = 500000.0
ROT_DIM = ATTN_HEAD_DIM // 4
MLSTM_HEADS = 4
MLSTM_HEAD_DIM = 128
LANES = 128
ATT_BLK = 128
ATT_REGION = 8
MLSTM_CHUNK = 128
VMEM_LIMIT = 56 * 1024 * 1024
ROW_TILE = 512

_NEG = float("-inf")
LOG2E = math.log2(math.e)


def _rms(x, g):
    ms = jnp.sum(x * x, axis=-1, keepdims=True) * (1.0 / x.shape[-1])
    return x * lax.rsqrt(ms + EPS) * g


def _const_spec(shape):
    nd = len(shape)
    return pl.BlockSpec(shape, lambda *_: (0,) * nd, pipeline_mode=pl.Buffered(1))


def _inproj_kernel(x_ref, pos_ref, g_ref, invc_ref, sel_ref, unrot_ref, w_ref, wgt_ref,
                   q_ref, k_ref, v_ref, u_ref, vm_ref, o_ref, gate_ref):
    aw3 = 3 * q_ref.shape[0] * LANES
    mw3 = 3 * u_ref.shape[1]
    ang = invc_ref[...] * pos_ref[...].astype(F32)
    half_rows = lax.broadcasted_iota(jnp.int32, ang.shape, 0) < ROT_DIM // 2
    tab = jnp.where(half_rows, jnp.cos(ang), jnp.sin(ang))
    cs = sum(lax.dot_general(part, sel_ref[...], (((0,), (0,)), ((), ())), preferred_element_type=F32)
             for part in _split2(tab))
    cos = cs[:, :LANES] + unrot_ref[...]
    sin = cs[:, LANES:]
    x = x_ref[...]
    h = (x * g_ref[...]).astype(BF16)
    rs = lax.rsqrt(jnp.sum(x * x, axis=-1, keepdims=True) * (1.0 / x.shape[-1]) + EPS)
    lane = lax.broadcasted_iota(jnp.int32, (1, LANES), 1)
    first_half = (lane % ATTN_HEAD_DIM) < (ROT_DIM // 2)

    def rotary(z):
        up = pltpu.roll(z, LANES - ROT_DIM // 2, 1)
        dn = pltpu.roll(z, ROT_DIM // 2, 1)
        return z * cos + jnp.where(first_half, up, dn) * sin

    zqkv = jnp.dot(h, w_ref[:, :aw3], preferred_element_type=F32) * rs
    aw = zqkv.shape[1] // 3
    q_scale = ATTN_HEAD_DIM ** -0.5 * LOG2E
    for j in range(aw // LANES):
        sl = slice(j * LANES, (j + 1) * LANES)
        q_ref[j] = rotary(zqkv[:, sl]) * q_scale
        k_ref[j] = rotary(zqkv[:, aw + j * LANES:aw + (j + 1) * LANES])
        v_ref[j] = zqkv[:, 2 * aw + j * LANES:2 * aw + (j + 1) * LANES]
    zuvo = jnp.dot(h, w_ref[:, aw3:aw3 + mw3], preferred_element_type=F32) * rs
    mw = mw3 // 3
    u_ref[...] = zuvo[:, :mw]
    vm_ref[...] = zuvo[:, mw:2 * mw].astype(BF16)
    o_ref[...] = zuvo[:, 2 * mw:]
    rs_row = jnp.broadcast_to(rs, (rs.shape[0], LANES)).T[0:1, :]
    gate_ref[...] = lax.dot_general(wgt_ref[...], h, (((1,), (1,)), ((), ())), preferred_element_type=F32) * rs_row


def _rotary_tables():
    half = ROT_DIM // 2
    inv = jnp.power(ROPE_THETA, -jnp.arange(0, ROT_DIM, 2, dtype=F32) / ROT_DIM)
    invc = jnp.concatenate([inv, inv])[:, None]
    sel = np.zeros((2 * half, 2 * LANES), np.float32)
    unrot = np.ones((1, LANES), np.float32)
    for l in range(LANES):
        hl = l % ATTN_HEAD_DIM
        if hl < ROT_DIM:
            sel[hl % half, l] = 1.0
            sel[half + hl % half, LANES + l] = -1.0 if hl < half else 1.0
            unrot[0, l] = 0.0
    return invc, jnp.asarray(sel, BF16), jnp.asarray(unrot)


def _inproj(x2, pos3, g, w, wgt, aw, mw, tm, S):
    T, D = x2.shape
    ng = wgt.shape[0]
    row = lambda n: pl.BlockSpec((tm, n), lambda i: (i, 0))
    tps = S // tm
    npair = aw // LANES
    pair_major = pl.BlockSpec((None, npair, tm, LANES), lambda i: (i // tps, 0, i % tps, 0))
    invc, sel, unrot = _rotary_tables()
    return pl.pallas_call(
        _inproj_kernel,
        grid=(T // tm,),
        in_specs=[row(D), pl.BlockSpec((None, 1, tm), lambda i: (i, 0, 0)), _const_spec((1, D)),
                  _const_spec(invc.shape), _const_spec(sel.shape), _const_spec(unrot.shape), _const_spec(w.shape),
                  _const_spec(wgt.shape)],
        out_specs=[pair_major, pair_major, pair_major, row(mw), row(mw), row(mw),
                   pl.BlockSpec((ng, tm), lambda i: (0, i))],
        out_shape=[jax.ShapeDtypeStruct((T // S, npair, S, LANES), F32)] * 3
        + [jax.ShapeDtypeStruct((T, mw), F32), jax.ShapeDtypeStruct((T, mw), BF16),
           jax.ShapeDtypeStruct((T, mw), F32), jax.ShapeDtypeStruct((ng, T), F32)],
        compiler_params=pltpu.CompilerParams(dimension_semantics=("arbitrary",), vmem_limit_bytes=VMEM_LIMIT),
        name="inproj",
    )(x2, pos3, g, invc, sel, unrot, w, wgt)


def _attn_bias():
    W, R, Q = 2 * ATT_BLK, ATTN_RADIUS, ATT_BLK
    kj = np.arange(W)[:, None]
    qi = np.arange(Q)[None, :]
    interior = (kj - qi >= 0) & (kj - qi <= 2 * R)
    near = np.abs(kj - qi) <= R
    edge = ((qi < Q // 2) & (kj < Q) & near) | ((qi >= Q // 2) & (kj >= Q) & (np.abs(kj - Q - qi) <= R))
    single = near & (kj < Q)
    return jnp.asarray(np.where(np.stack([interior, edge, single]), 0.0, _NEG).astype(np.float32), BF16)


def _attn_kernel(q_ref, k_ref, v_ref, bias_ref, out_ref, qc_ref, kc_ref, vt0_ref, vt1_ref, o_scr, l_scr):
    S = q_ref.shape[0]
    Q, HD = ATT_BLK, ATTN_HEAD_DIM
    lane = lax.broadcasted_iota(jnp.int32, (1, LANES), 1)
    head0 = lane < HD
    npat = len(ATTN_PATTERNS)
    vt0_ref[:, HD:, :] = jnp.ones((npat, HD, S), BF16)
    vt1_ref[:, :HD, :] = jnp.ones((npat, HD, S), BF16)

    def rows_of(r, start, size, d):
        return pl.ds(start, size) if d == 1 else pl.ds(r + d * start, size, stride=d)

    def cat(parts, axis):
        return parts[0] if len(parts) == 1 else jnp.concatenate(parts, axis=axis)

    blocks = []
    for pi, (_, d) in enumerate(ATTN_PATTERNS):
        M = S // d
        nb = M // Q
        for r in range(d):
            base = r * M
            tok = rows_of(r, 0, M, d)
            qc_ref[pi, base:base + M, :] = q_ref[tok, :].astype(BF16)
            kc_ref[pi, base:base + M, :] = k_ref[tok, :].astype(BF16)
            vt = v_ref[tok, :].T.astype(BF16)
            vt0_ref[pi, :HD, base:base + M] = vt[:HD]
            vt1_ref[pi, HD:, base:base + M] = vt[HD:]
            if nb == 1:
                blocks.append(dict(pi=pi, q=[slice(base, base + Q)], k=[slice(base, base + Q)], bias=2, W=Q,
                                   out=[(tok, slice(0, Q))]))
                continue
            for i in range(nb - 1):
                k0 = base + i * Q
                blocks.append(dict(pi=pi, q=[slice(k0 + Q // 2, k0 + Q // 2 + Q)], k=[slice(k0, k0 + 2 * Q)],
                                   bias=0, W=2 * Q, out=[(rows_of(r, i * Q + Q // 2, Q, d), slice(0, Q))]))
            blocks.append(dict(pi=pi, q=[slice(base, base + Q // 2), slice(base + M - Q // 2, base + M)],
                               k=[slice(base, base + Q), slice(base + M - Q, base + M)], bias=1, W=2 * Q,
                               out=[(rows_of(r, 0, Q // 2, d), slice(0, Q // 2)),
                                    (rows_of(r, M - Q // 2, Q // 2, d), slice(Q // 2, Q))]))

    def region_scores(region):
        scores = []
        for b in region:
            qb = cat([qc_ref[b["pi"], s, :] for s in b["q"]], 0)
            kw = cat([kc_ref[b["pi"], s, :] for s in b["k"]], 0)
            bias = bias_ref[b["bias"]][:b["W"]]
            for hh in range(2):
                qm = jnp.where(head0 if hh == 0 else jnp.logical_not(head0), qb, jnp.zeros_like(qb))
                s = lax.dot_general(kw, qm, (((1,), (1,)), ((), ())), preferred_element_type=F32)
                scores.append(s.astype(BF16) + bias)
        return scores

    def region_finish(region, scores):
        maxes = [jnp.max(s, axis=0, keepdims=True) for s in scores]
        probs = [jnp.exp2(s - m) for s, m in zip(scores, maxes)]
        maxes = [m.astype(F32) for m in maxes]
        accs = []
        for bi, b in enumerate(region):
            for hh, vref in enumerate((vt0_ref, vt1_ref)):
                vw = cat([vref[b["pi"], :, s] for s in b["k"]], 1)
                accs.append(jnp.dot(vw, probs[2 * bi + hh], preferred_element_type=F32))
        for bi, b in enumerate(region):
            o_t, l_t = [], []
            for hh in range(2):
                a = accs[2 * bi + hh]
                l = a[HD:HD + 1] if hh == 0 else a[0:1]
                o_t.append((a[:HD] if hh == 0 else a[HD:]) * (1.0 / l))
                l_t.append(jnp.broadcast_to(maxes[2 * bi + hh] + jnp.log2(l), (HD, Q)))
            o = jnp.concatenate(o_t, axis=0).T
            lse = jnp.concatenate(l_t, axis=0).T
            for rows, sl in b["out"]:
                o_scr[b["pi"], rows, :] = o[sl]
                l_scr[b["pi"], rows, :] = lse[sl]

    for g in range(0, len(blocks), ATT_REGION):
        region = blocks[g:g + ATT_REGION]
        region_finish(region, region_scores(region))

    lse = l_scr[...]
    mx = jnp.max(lse, axis=0)
    w = jnp.exp2(lse - mx[None])
    out = jnp.sum(w * o_scr[...], axis=0) * (1.0 / jnp.sum(w, axis=0))
    out_ref[...] = out.astype(out_ref.dtype)


def _attn(q, k, v, bias):
    B, npair, S, _ = q.shape
    blk = pl.BlockSpec((None, None, S, LANES), lambda b, j: (b, j, 0, 0))
    npat = len(ATTN_PATTERNS)
    return pl.pallas_call(
        _attn_kernel,
        grid=(B, npair),
        in_specs=[blk, blk, blk, _const_spec(bias.shape)],
        out_specs=pl.BlockSpec((None, S, LANES), lambda b, j: (b, 0, j)),
        out_shape=jax.ShapeDtypeStruct((B, S, npair * LANES), BF16),
        scratch_shapes=[pltpu.VMEM((npat, S, LANES), BF16), pltpu.VMEM((npat, S, LANES), BF16),
                        pltpu.VMEM((npat, LANES, S), BF16), pltpu.VMEM((npat, LANES, S), BF16),
                        pltpu.VMEM((npat, S, LANES), F32), pltpu.VMEM((npat, S, LANES), F32)],
        compiler_params=pltpu.CompilerParams(dimension_semantics=("arbitrary", "arbitrary"),
                                             vmem_limit_bytes=VMEM_LIMIT),
        name="dilated_attn",
    )(q, k, v, bias)


def _split2(x):
    hi = x.astype(BF16)
    return hi, (x - hi.astype(F32)).astype(BF16)


def _log_sigmoid(x):
    return jnp.minimum(x, 0.0) - jnp.log1p(jnp.exp(-jnp.abs(x)))


def _mlstm_kernel(bias_ref, u_ref, vm_ref, o_ref, grow_ref, cw_ref, cb_ref, wq_ref, wk_ref, gn_ref,
                  out_ref, q_scr, kt_scr, va_scr, hs_scr, lf_scr, b_scr, i_scr, w_scr, mp_scr, dc_scr,
                  e_scr, a_scr, f_scr, p_scr, nd_scr, cp_scr):
    S, dh = u_ref.shape
    L = MLSTM_CHUNK
    nc = S // L
    hd = pl.program_id(1)

    u = u_ref[...]
    rows = lax.broadcasted_iota(jnp.int32, (S, 1), 0)
    u_prev = jnp.where(rows == 0, 0.0, pltpu.roll(u, 1, 0))
    u_next = jnp.where(rows == S - 1, 0.0, pltpu.roll(u, S - 1, 0))
    cw = cw_ref[...]
    a = u_prev * cw[0:1] + u * cw[1:2] + u_next * cw[2:3] + cb_ref[...]
    uc = (a * jax.nn.sigmoid(a)).astype(BF16)
    q_scr[...] = jnp.dot(uc, wq_ref[...], preferred_element_type=F32).astype(BF16)
    k = jnp.dot(uc, wk_ref[...], preferred_element_type=F32) * (dh ** -0.5)
    for c in range(nc):
        kt_scr[c] = k[c * L:(c + 1) * L].T.astype(BF16)
    va_scr[:, :dh] = vm_ref[...]
    va_scr[:, dh:] = jnp.ones((S, dh), BF16)

    ri = lax.broadcasted_iota(jnp.int32, (L, L), 0)
    ci = lax.broadcasted_iota(jnp.int32, (L, L), 1)
    lower = ri >= ci
    upper = ri <= ci

    for di in range(2):
        b_i = bias_ref[0, di, hd]
        b_f = bias_ref[1, di, hd]
        irow = (grow_ref[di] + b_i) * LOG2E
        lf = _log_sigmoid(grow_ref[2 + di] + b_f) * LOG2E
        tri = (upper if di == 0 else lower).astype(BF16)
        brow = sum(jnp.dot(part, tri, preferred_element_type=F32) for part in _split2(lf))
        g = jnp.sum(lf, axis=1, keepdims=True)
        logw = g - brow + irow
        gb = jnp.broadcast_to(g, (nc, dh))
        mxb = jnp.broadcast_to(jnp.max(logw, axis=1, keepdims=True), (nc, dh))
        m = jnp.zeros((1, dh), F32)
        mprev, mnew = [None] * nc, [None] * nc
        for step in range(nc):
            c = step if di == 0 else nc - 1 - step
            mprev[c] = m
            m = jnp.maximum(gb[c:c + 1] + m, mxb[c:c + 1])
            mnew[c] = m
        mprev = jnp.concatenate(mprev, axis=0)
        mnew = jnp.concatenate(mnew, axis=0)
        lf_scr[di] = lf
        b_scr[di] = brow
        i_scr[di] = irow
        w_scr[di] = jnp.exp2(logw - mnew)
        mp_scr[di] = mprev
        dc_scr[di] = jnp.exp2(gb + mprev - mnew)

    ones_rhs = jnp.ones((2 * L, dh), BF16)
    chunk = lambda c: slice(c * L, (c + 1) * L)

    for c in range(nc):
        for di in range(2):
            lfr, brow, irow, mprev = (ref[di, c:c + 1, :] for ref in (lf_scr, b_scr, i_scr, mp_scr))
            causal = lower if di == 0 else upper
            hi, lo = _split2(jnp.where(causal, lfr, 0.0))
            bcol = jnp.dot(jnp.concatenate([hi, lo], axis=1), ones_rhs, preferred_element_type=F32)
            dmat = jnp.where(causal, bcol - brow + irow, _NEG)
            m_inter = bcol + mprev
            m_t = jnp.maximum(m_inter, jnp.max(dmat, axis=1, keepdims=True))
            e_scr[di, c] = jnp.exp2(dmat - m_t)
            a_scr[di, c] = jnp.exp2(m_inter - m_t)
            f_scr[di, c] = jnp.exp2(-m_t)

    for c in range(nc):
        qk = jnp.dot(q_scr[chunk(c), :], kt_scr[c], preferred_element_type=F32)
        for di in range(2):
            p_scr[di, c] = (qk * e_scr[di, c]).astype(BF16)

    states = [jnp.zeros((dh, 2 * dh), F32)] * 2
    for step in range(nc):
        for di in range(2):
            c = step if di == 0 else nc - 1 - step
            va = va_scr[chunk(c), :]
            nd_scr[di, c] = jnp.dot(p_scr[di, c], va, preferred_element_type=F32)
            cp_scr[di, c] = states[di].astype(BF16)
            ktw = (kt_scr[c].astype(F32) * w_scr[di, c:c + 1, :]).astype(BF16)
            decay = dc_scr[di, c:c + 1, :]
            states[di] = jnp.concatenate([decay, decay], axis=1) * states[di] \
                + jnp.dot(ktw, va, preferred_element_type=F32)

    for c in range(nc):
        qc = q_scr[chunk(c), :]
        hsum = None
        for di in range(2):
            a_in = a_scr[di, c]
            nd = jnp.concatenate([a_in, a_in], axis=1) * jnp.dot(qc, cp_scr[di, c], preferred_element_type=F32) \
                + nd_scr[di, c]
            hval = nd[:, :dh] * (1.0 / jnp.maximum(jnp.abs(nd[:, dh:]), f_scr[di, c]))
            hsum = hval if hsum is None else hsum + hval
        hs_scr[chunk(c), :] = hsum

    hs = hs_scr[...]
    mu = jnp.sum(hs, axis=1, keepdims=True) * (1.0 / dh)
    cen = hs - mu
    var = jnp.sum(cen * cen, axis=1, keepdims=True) * (1.0 / dh)
    hn = cen * lax.rsqrt(var + EPS)
    out_ref[...] = (hn * gn_ref[...] * jax.nn.sigmoid(o_ref[...])).astype(out_ref.dtype)


def _mlstm(gate_bias, u, vm, o, grow, cw, cb, wq, wk, gn):
    B, S, MW = u.shape
    H, dh = MLSTM_HEADS, MLSTM_HEAD_DIM
    L = MLSTM_CHUNK
    assert L == dh == LANES
    nc = S // L
    tok = pl.BlockSpec((None, S, dh), lambda b, h: (b, 0, h))
    return pl.pallas_call(
        _mlstm_kernel,
        grid=(B, H),
        in_specs=[pl.BlockSpec(memory_space=pltpu.SMEM), tok, tok, tok,
                  pl.BlockSpec((None, None, 4, nc, L), lambda b, h: (b, h, 0, 0, 0)),
                  pl.BlockSpec((3, dh), lambda b, h: (0, h)), pl.BlockSpec((1, dh), lambda b, h: (0, h)),
                  pl.BlockSpec((None, dh, dh), lambda b, h: (h, 0, 0)),
                  pl.BlockSpec((None, dh, dh), lambda b, h: (h, 0, 0)),
                  pl.BlockSpec((1, dh), lambda b, h: (0, h))],
        out_specs=tok,
        out_shape=jax.ShapeDtypeStruct((B, S, MW), BF16),
        scratch_shapes=[pltpu.VMEM((S, dh), BF16), pltpu.VMEM((nc, dh, L), BF16), pltpu.VMEM((S, 2 * dh), BF16),
                        pltpu.VMEM((S, dh), F32)] + [pltpu.VMEM((2, nc, L), F32)] * 6
        + [pltpu.VMEM((2, nc, L, L), F32), pltpu.VMEM((2, nc, L, dh), F32), pltpu.VMEM((2, nc, L, dh), F32),
           pltpu.VMEM((2, nc, L, L), BF16), pltpu.VMEM((2, nc, L, 2 * dh), F32),
           pltpu.VMEM((2, nc, dh, 2 * dh), BF16)],
        compiler_params=pltpu.CompilerParams(dimension_semantics=("arbitrary", "arbitrary"),
                                             vmem_limit_bytes=VMEM_LIMIT),
        name="mlstm",
    )(gate_bias, u, vm, o, grow, cw, cb, wq, wk, gn)


HALO = 16
FFN_CHUNK = 256


def _ffn_kernel(x_ref, xp_ref, xn_ref, a_ref, ap_ref, an_ref, m_ref, mp_ref, mn_ref, p_ref,
                wo_ref, g2_ref, wg_ref, wu_ref, wd_ref, cw_ref, cb_ref, gp_ref, wpg_ref, bpg_ref, wpp_ref, gf_ref,
                out_ref, mix_ref, hcat_ref, x1_ref, act_ref, *, tiles_per_seq):
    tm, aw = a_ref.shape
    ck = FFN_CHUNK
    nck = wg_ref.shape[1] // ck
    i = pl.program_id(0)
    first = (i % tiles_per_seq) == 0
    last = (i % tiles_per_seq) == tiles_per_seq - 1

    for lo, aa, mm in ((0, ap_ref, mp_ref), (HALO, a_ref, m_ref), (HALO + tm, an_ref, mn_ref)):
        mix_ref[lo:lo + aa.shape[0], :aw] = aa[...]
        mix_ref[lo:lo + aa.shape[0], aw:] = mm[...]
    y = jnp.dot(mix_ref[...], wo_ref[...], preferred_element_type=F32)
    x1 = x_ref[...] + y[HALO:HALO + tm]
    x1_ref[...] = x1
    g2 = g2_ref[...]
    hcat_ref[HALO:HALO + tm, :] = _rms(x1, g2).astype(BF16)
    hp = _rms(xp_ref[...] + y[:HALO], g2).astype(BF16)
    hn = _rms(xn_ref[...] + y[HALO + tm:], g2).astype(BF16)
    hcat_ref[0:HALO, :] = jnp.where(first, jnp.zeros_like(hp), hp)
    hcat_ref[HALO + tm:, :] = jnp.where(last, jnp.zeros_like(hn), hn)

    rows = tm + 2 * HALO
    for c in range(nck):
        cols = slice(c * ck, (c + 1) * ck)
        g = jnp.dot(hcat_ref[...], wg_ref[:, cols], preferred_element_type=F32)
        up = jnp.dot(hcat_ref[HALO:HALO + tm, :], wu_ref[:, cols], preferred_element_type=F32)
        cw = cw_ref[:, cols]
        a = pltpu.roll(g, 1, 0)[HALO:HALO + tm] * cw[0:1] + g[HALO:HALO + tm] * cw[1:2] \
            + pltpu.roll(g, rows - 1, 0)[HALO:HALO + tm] * cw[2:3] + cb_ref[:, cols]
        act_ref[:, cols] = (0.5 * a * (1.0 + lax.erf(a * (2.0 ** -0.5))) * up).astype(BF16)
    x2 = x1_ref[...] + jnp.dot(act_ref[...], wd_ref[...], preferred_element_type=F32)
    ms = jnp.sum(x2 * x2, axis=-1, keepdims=True) * (1.0 / x2.shape[-1])
    zg = jnp.dot((x2 * gp_ref[...]).astype(BF16), wpg_ref[...], preferred_element_type=F32)
    gate = jax.nn.sigmoid(zg * lax.rsqrt(ms + EPS) + bpg_ref[...])
    pp = jnp.dot(p_ref[...].astype(BF16), wpp_ref[...], preferred_element_type=F32)
    out_ref[...] = _rms(x2 + pp * gate, gf_ref[...])


def _ffn(x2d, attn, mem, p2, wo, g2, wg, wu, wd, cw, cb, gp, wpg, bpg, wpp, gf, tm, S):
    T, D = x2d.shape
    row = lambda w: pl.BlockSpec((tm, w), lambda i: (i, 0))
    hb = tm // HALO
    nhb = T // HALO
    prev = lambda w: pl.BlockSpec((HALO, w), lambda i: (jnp.maximum(i * hb - 1, 0), 0))
    nxt = lambda w: pl.BlockSpec((HALO, w), lambda i: (jnp.minimum((i + 1) * hb, nhb - 1), 0))
    aw, mw = attn.shape[1], mem.shape[1]
    consts = (wo, g2, wg, wu, wd, cw, cb, gp, wpg, bpg, wpp, gf)
    return pl.pallas_call(
        functools.partial(_ffn_kernel, tiles_per_seq=S // tm),
        grid=(T // tm,),
        in_specs=[row(D), prev(D), nxt(D), row(aw), prev(aw), nxt(aw), row(mw), prev(mw), nxt(mw),
                  row(p2.shape[1])] + [_const_spec(c.shape) for c in consts],
        out_specs=row(D),
        out_shape=jax.ShapeDtypeStruct((T, D), F32),
        scratch_shapes=[pltpu.VMEM((tm + 2 * HALO, aw + mw), BF16), pltpu.VMEM((tm + 2 * HALO, D), BF16),
                        pltpu.VMEM((tm, D), F32), pltpu.VMEM((tm, wg.shape[1]), BF16)],
        compiler_params=pltpu.CompilerParams(dimension_semantics=("arbitrary",), vmem_limit_bytes=VMEM_LIMIT),
        name="outproj_ffn_ple",
    )(x2d, x2d, x2d, attn, attn, attn, mem, mem, mem, p2, *consts)


def kernel(x, p, positions, ln_mix_g, w_in, mlstm_conv_w, mlstm_conv_b, w_mq, w_mk, b_igate, b_fgate,
           mlstm_gn_g, w_out, ln_ffn_g, w_ffn_gate, ffn_conv_w, ffn_conv_b, w_ffn_up, w_ffn_down,
           ln_ple_g, w_ple_gate, b_ple_gate, w_ple_proj, ln_final_g):
    B, S, D = x.shape
    depth = w_in.shape[0]
    T = B * S
    H, dh = MLSTM_HEADS, MLSTM_HEAD_DIM
    aw = D // 2
    mw = D - aw
    tm = ROW_TILE
    L = MLSTM_CHUNK
    nc = S // L

    bias = _attn_bias()
    pos3 = positions.reshape(T // tm, 1, tm)

    assert depth == 1, "the final norm is fused into the layer's last kernel"
    xf = x.reshape(T, D)
    for i in range(depth):
        nproj = 3 * aw + 3 * mw
        q, k, v, u, vm, o, gates_t = _inproj(xf, pos3, ln_mix_g[i][None], w_in[i][:, :nproj].astype(BF16),
                                             w_in[i][:, nproj:].T.astype(BF16), aw, mw, tm, S)

        attn = _attn(q, k, v, bias)

        grow = gates_t.reshape(4, H, B, nc, L).transpose(2, 1, 0, 3, 4)
        gate_bias = jnp.stack([b_igate[i], b_fgate[i]]).astype(F32)
        mem = _mlstm(gate_bias, u.reshape(B, S, mw), vm.reshape(B, S, mw), o.reshape(B, S, mw), grow,
                     mlstm_conv_w[i], mlstm_conv_b[i][None], w_mq[i].astype(BF16), w_mk[i].astype(BF16),
                     mlstm_gn_g[i][None])

        assert w_ffn_gate.shape[-1] % FFN_CHUNK == 0
        xf = _ffn(xf, attn.reshape(T, aw), mem.reshape(T, mw), p[i].reshape(T, -1),
                  w_out[i].astype(BF16), ln_ffn_g[i][None],
                  w_ffn_gate[i].astype(BF16), w_ffn_up[i].astype(BF16), w_ffn_down[i].astype(BF16),
                  ffn_conv_w[i], ffn_conv_b[i][None], ln_ple_g[i][None],
                  w_ple_gate[i].astype(BF16), b_ple_gate[i][None], w_ple_proj[i].astype(BF16), ln_final_g[None], tm, S)
    return xf.reshape(B, S, D)
```

```python
import functools
import math

import jax
import jax.numpy as jnp
import numpy as np
from jax import lax
from jax.experimental import pallas as pl
from jax.experimental.pallas import tpu as pltpu

F32 = jnp.float32
BF16 = jnp.bfloat16

EPS = 1e-6
ATTN_HEAD_DIM = 64
ATTN_PATTERNS = ((128, 1), (512, 4), (2048, 16))
ATTN_RADIUS = 64
ROPE_THETA = 500000.0
ROT_DIM = ATTN_HEAD_DIM // 4
MLSTM_HEADS = 4
MLSTM_HEAD_DIM = 128
LANES = 128
ATT_BLK = 128
ATT_REGION = 8
MLSTM_CHUNK = 128
VMEM_LIMIT = 56 * 1024 * 1024
ROW_TILE = 512
INPROJ_TILE = 1024

_NEG = float("-inf")
LOG2E = math.log2(math.e)


def _rms(x, g):
    ms = jnp.sum(x * x, axis=-1, keepdims=True) * (1.0 / x.shape[-1])
    return x * lax.rsqrt(ms + EPS) * g


def _const_spec(shape):
    nd = len(shape)
    return pl.BlockSpec(shape, lambda *_: (0,) * nd, pipeline_mode=pl.Buffered(1))


def _inproj_kernel(x_ref, pos_ref, g_ref, invc_ref, sel_ref, unrot_ref, w_ref,
                   q_ref, k_ref, v_ref, u_ref, vm_ref, o_ref, gate_ref):
    aw3 = 3 * q_ref.shape[0] * LANES
    mw3 = 3 * u_ref.shape[1]
    ang = invc_ref[...] * pos_ref[...].astype(F32)
    half_rows = lax.broadcasted_iota(jnp.int32, ang.shape, 0) < ROT_DIM // 2
    tab = jnp.where(half_rows, jnp.cos(ang), jnp.sin(ang))
    cs = sum(lax.dot_general(part, sel_ref[...], (((0,), (0,)), ((), ())), preferred_element_type=F32)
             for part in _split2(tab))
    cos = cs[:, :LANES] + unrot_ref[...]
    sin = cs[:, LANES:]
    x = x_ref[...]
    h = (x * g_ref[...]).astype(BF16)
    rs = lax.rsqrt(jnp.sum(x * x, axis=-1, keepdims=True) * (1.0 / x.shape[-1]) + EPS)
    lane = lax.broadcasted_iota(jnp.int32, (1, LANES), 1)
    first_half = (lane % ATTN_HEAD_DIM) < (ROT_DIM // 2)

    def rotary(z):
        up = pltpu.roll(z, LANES - ROT_DIM // 2, 1)
        dn = pltpu.roll(z, ROT_DIM // 2, 1)
        return z * cos + jnp.where(first_half, up, dn) * sin

    zqkv = jnp.dot(h, w_ref[:, :aw3], preferred_element_type=F32) * rs
    aw = zqkv.shape[1] // 3
    q_scale = ATTN_HEAD_DIM ** -0.5 * LOG2E
    for j in range(aw // LANES):
        sl = slice(j * LANES, (j + 1) * LANES)
        q_ref[j] = rotary(zqkv[:, sl]) * q_scale
        k_ref[j] = rotary(zqkv[:, aw + j * LANES:aw + (j + 1) * LANES])
        v_ref[j] = zqkv[:, 2 * aw + j * LANES:2 * aw + (j + 1) * LANES]
    zuvo = jnp.dot(h, w_ref[:, aw3:aw3 + mw3], preferred_element_type=F32) * rs
    mw = mw3 // 3
    u_ref[...] = zuvo[:, :mw]
    vm_ref[...] = zuvo[:, mw:2 * mw].astype(BF16)
    o_ref[...] = zuvo[:, 2 * mw:]
    gates = jnp.dot(h, w_ref[:, aw3 + mw3:], preferred_element_type=F32) * rs
    gate_ref[...] = gates.T[:gate_ref.shape[0]]


def _rotary_tables():
    half = ROT_DIM // 2
    inv = jnp.power(ROPE_THETA, -jnp.arange(0, ROT_DIM, 2, dtype=F32) / ROT_DIM)
    invc = jnp.concatenate([inv, inv])[:, None]
    sel = np.zeros((2 * half, 2 * LANES), np.float32)
    unrot = np.ones((1, LANES), np.float32)
    for l in range(LANES):
        hl = l % ATTN_HEAD_DIM
        if hl < ROT_DIM:
            sel[hl % half, l] = 1.0
            sel[half + hl % half, LANES + l] = -1.0 if hl < half else 1.0
            unrot[0, l] = 0.0
    return invc, jnp.asarray(sel, BF16), jnp.asarray(unrot)


def _inproj(x2, pos3, g, w, ng, aw, mw, tm, S):
    T, D = x2.shape
    assert w.shape[1] == 3 * aw + 3 * mw + LANES
    row = lambda n: pl.BlockSpec((tm, n), lambda i: (i, 0))
    tps = S // tm
    npair = aw // LANES
    pair_major = pl.BlockSpec((None, npair, tm, LANES), lambda i: (i // tps, 0, i % tps, 0))
    invc, sel, unrot = _rotary_tables()
    return pl.pallas_call(
        _inproj_kernel,
        grid=(T // tm,),
        in_specs=[row(D), pl.BlockSpec((None, 1, tm), lambda i: (i, 0, 0)), _const_spec((1, D)),
                  _const_spec(invc.shape), _const_spec(sel.shape), _const_spec(unrot.shape), _const_spec(w.shape)],
        out_specs=[pair_major, pair_major, pair_major, row(mw), row(mw), row(mw),
                   pl.BlockSpec((ng, tm), lambda i: (0, i))],
        out_shape=[jax.ShapeDtypeStruct((T // S, npair, S, LANES), F32)] * 3
        + [jax.ShapeDtypeStruct((T, mw), F32), jax.ShapeDtypeStruct((T, mw), BF16),
           jax.ShapeDtypeStruct((T, mw), F32), jax.ShapeDtypeStruct((ng, T), F32)],
        compiler_params=pltpu.CompilerParams(dimension_semantics=("arbitrary",), vmem_limit_bytes=VMEM_LIMIT),
        name="inproj",
    )(x2, pos3, g, invc, sel, unrot, w)


def _attn_bias():
    W, R, Q = 2 * ATT_BLK, ATTN_RADIUS, ATT_BLK
    kj = np.arange(W)[:, None]
    qi = np.arange(Q)[None, :]
    interior = (kj - qi >= 0) & (kj - qi <= 2 * R)
    near = np.abs(kj - qi) <= R
    edge = ((qi < Q // 2) & (kj < Q) & near) | ((qi >= Q // 2) & (kj >= Q) & (np.abs(kj - Q - qi) <= R))
    single = near & (kj < Q)
    return jnp.asarray(np.where(np.stack([interior, edge, single]), 0.0, _NEG).astype(np.float32), BF16)


def _attn_kernel(q_ref, k_ref, v_ref, bias_ref, out_ref, qc_ref, kc_ref, vt0_ref, vt1_ref, o_scr, l_scr,
                 f4_scr, outf_scr):
    S = q_ref.shape[0]
    Q, HD = ATT_BLK, ATTN_HEAD_DIM
    lane = lax.broadcasted_iota(jnp.int32, (1, LANES), 1)
    head0 = lane < HD
    npat = len(ATTN_PATTERNS)
    vt0_ref[:, HD:, :] = jnp.ones((npat, HD, S), BF16)
    vt1_ref[:, :HD, :] = jnp.ones((npat, HD, S), BF16)

    def cat(parts, axis):
        return parts[0] if len(parts) == 1 else jnp.concatenate(parts, axis=axis)

    dm = ATTN_PATTERNS[1][1]
    assert [d for _, d in ATTN_PATTERNS] == [1, dm, dm * dm]
    Mm = S // dm

    def put(pi, base, xq, xk, xv):
        n = xq.shape[0]
        qc_ref[pi, base:base + n, :] = xq.astype(BF16)
        kc_ref[pi, base:base + n, :] = xk.astype(BF16)
        vt = xv.T.astype(BF16)
        vt0_ref[pi, :HD, base:base + n] = vt[:HD]
        vt1_ref[pi, HD:, base:base + n] = vt[HD:]

    put(0, 0, q_ref[...], k_ref[...], v_ref[...])
    for r in range(dm):
        xs = [src[pl.ds(r, Mm, stride=dm), :] for src in (q_ref, k_ref, v_ref)]
        for a, x in enumerate(xs):
            f4_scr[a, r * Mm:(r + 1) * Mm, :] = x
        put(1, r * Mm, *xs)
    for r in range(dm * dm):
        put(2, r * (Mm // dm), *[f4_scr[a, pl.ds((r % dm) * Mm + r // dm, Mm // dm, stride=dm), :] for a in range(3)])

    def out_rows(pi, r, start, size):
        if pi == 0:
            return pl.ds(start, size)
        if pi == 1:
            return pl.ds(r * Mm + start, size)
        return pl.ds((r % dm) * Mm + r // dm + dm * start, size, stride=dm)

    blocks = []
    for pi, (_, d) in enumerate(ATTN_PATTERNS):
        M = S // d
        nb = M // Q
        for r in range(d):
            base = r * M
            if nb == 1:
                blocks.append(dict(pi=pi, q=[slice(base, base + Q)], k=[slice(base, base + Q)], bias=2, W=Q,
                                   out=[(out_rows(pi, r, 0, Q), slice(0, Q))]))
                continue
            for i in range(nb - 1):
                k0 = base + i * Q
                blocks.append(dict(pi=pi, q=[slice(k0 + Q // 2, k0 + Q // 2 + Q)], k=[slice(k0, k0 + 2 * Q)],
                                   bias=0, W=2 * Q, out=[(out_rows(pi, r, i * Q + Q // 2, Q), slice(0, Q))]))
            blocks.append(dict(pi=pi, q=[slice(base, base + Q // 2), slice(base + M - Q // 2, base + M)],
                               k=[slice(base, base + Q), slice(base + M - Q, base + M)], bias=1, W=2 * Q,
                               out=[(out_rows(pi, r, 0, Q // 2), slice(0, Q // 2)),
                                    (out_rows(pi, r, M - Q // 2, Q // 2), slice(Q // 2, Q))]))

    def region_scores(region):
        scores = []
        for b in region:
            qb = cat([qc_ref[b["pi"], s, :] for s in b["q"]], 0)
            kw = cat([kc_ref[b["pi"], s, :] for s in b["k"]], 0)
            bias = bias_ref[b["bias"]][:b["W"]]
            for hh in range(2):
                qm = jnp.where(head0 if hh == 0 else jnp.logical_not(head0), qb, jnp.zeros_like(qb))
                s = lax.dot_general(kw, qm, (((1,), (1,)), ((), ())), preferred_element_type=F32)
                scores.append(s.astype(BF16) + bias)
        return scores

    def region_finish(region, scores):
        maxes = [jnp.max(s, axis=0, keepdims=True) for s in scores]
        probs = [jnp.exp2(s - m) for s, m in zip(scores, maxes)]
        maxes = [m.astype(F32) for m in maxes]
        accs = []
        for bi, b in enumerate(region):
            for hh, vref in enumerate((vt0_ref, vt1_ref)):
                vw = cat([vref[b["pi"], :, s] for s in b["k"]], 1)
                accs.append(jnp.dot(vw, probs[2 * bi + hh], preferred_element_type=F32))
        for bi, b in enumerate(region):
            o_t, l_t = [], []
            for hh in range(2):
                a = accs[2 * bi + hh]
                l = a[HD:HD + 1] if hh == 0 else a[0:1]
                o_t.append((a[:HD] if hh == 0 else a[HD:]) * (1.0 / l))
                l_t.append(jnp.broadcast_to(maxes[2 * bi + hh] + jnp.log2(l), (HD, Q)))
            o = jnp.concatenate(o_t, axis=0).T
            lse = jnp.concatenate(l_t, axis=0).T
            for rows, sl in b["out"]:
                o_scr[b["pi"], rows, :] = o[sl]
                l_scr[b["pi"], rows, :] = lse[sl]

    for g in range(0, len(blocks), ATT_REGION):
        region = blocks[g:g + ATT_REGION]
        region_finish(region, region_scores(region))

    for r in range(dm):
        tok = pl.ds(r, Mm, stride=dm)
        cls = slice(r * Mm, (r + 1) * Mm)
        os = (o_scr[0, tok, :], o_scr[1, cls, :], o_scr[2, cls, :])
        ls = (l_scr[0, tok, :], l_scr[1, cls, :], l_scr[2, cls, :])
        mx = jnp.maximum(jnp.maximum(ls[0], ls[1]), ls[2])
        ws = [jnp.exp2(l - mx) for l in ls]
        outf_scr[tok, :] = (ws[0] * os[0] + ws[1] * os[1] + ws[2] * os[2]) * (1.0 / (ws[0] + ws[1] + ws[2]))
    out_ref[...] = outf_scr[...].astype(out_ref.dtype)


def _attn(q, k, v, bias):
    B, npair, S, _ = q.shape
    blk = pl.BlockSpec((None, None, S, LANES), lambda b, j: (b, j, 0, 0))
    npat = len(ATTN_PATTERNS)
    return pl.pallas_call(
        _attn_kernel,
        grid=(B, npair),
        in_specs=[blk, blk, blk, _const_spec(bias.shape)],
        out_specs=pl.BlockSpec((None, S, LANES), lambda b, j: (b, 0, j)),
        out_shape=jax.ShapeDtypeStruct((B, S, npair * LANES), BF16),
        scratch_shapes=[pltpu.VMEM((npat, S, LANES), BF16), pltpu.VMEM((npat, S, LANES), BF16),
                        pltpu.VMEM((npat, LANES, S), BF16), pltpu.VMEM((npat, LANES, S), BF16),
                        pltpu.VMEM((npat, S, LANES), F32), pltpu.VMEM((npat, S, LANES), F32),
                        pltpu.VMEM((3, S, LANES), F32), pltpu.VMEM((S, LANES), F32)],
        compiler_params=pltpu.CompilerParams(dimension_semantics=("arbitrary", "arbitrary"),
                                             vmem_limit_bytes=VMEM_LIMIT),
        name="dilated_attn",
    )(q, k, v, bias)


def _split2(x):
    hi = x.astype(BF16)
    return hi, (x - hi.astype(F32)).astype(BF16)


def _log_sigmoid(x):
    return jnp.minimum(x, 0.0) - jnp.log1p(jnp.exp(-jnp.abs(x)))


def _mlstm_kernel(bias_ref, u_ref, vm_ref, o_ref, grow_ref, cw_ref, cb_ref, wq_ref, wk_ref, gn_ref,
                  out_ref, q_scr, kt_scr, va_scr, hs_scr, lf_scr, b_scr, i_scr, w_scr, mp_scr, dc_scr,
                  e_scr, a_scr, f_scr, p_scr, nd_scr, cp_scr):
    S, dh = u_ref.shape
    L = MLSTM_CHUNK
    nc = S // L
    hd = pl.program_id(1)

    u = u_ref[...]
    rows = lax.broadcasted_iota(jnp.int32, (S, 1), 0)
    u_prev = jnp.where(rows == 0, 0.0, pltpu.roll(u, 1, 0))
    u_next = jnp.where(rows == S - 1, 0.0, pltpu.roll(u, S - 1, 0))
    cw = cw_ref[...]
    a = u_prev * cw[0:1] + u * cw[1:2] + u_next * cw[2:3] + cb_ref[...]
    uc = (a * jax.nn.sigmoid(a)).astype(BF16)
    q_scr[...] = jnp.dot(uc, wq_ref[...], preferred_element_type=F32).astype(BF16)
    k = jnp.dot(uc, wk_ref[...], preferred_element_type=F32) * (dh ** -0.5)
    for c in range(nc):
        kt_scr[c] = k[c * L:(c + 1) * L].T.astype(BF16)
    va_scr[:, :dh] = vm_ref[...]
    va_scr[:, dh:] = jnp.ones((S, dh), BF16)

    ri = lax.broadcasted_iota(jnp.int32, (L, L), 0)
    ci = lax.broadcasted_iota(jnp.int32, (L, L), 1)
    lower = ri >= ci
    upper = ri <= ci

    for di in range(2):
        b_i = bias_ref[0, di, hd]
        b_f = bias_ref[1, di, hd]
        irow = (grow_ref[di] + b_i) * LOG2E
        lf = _log_sigmoid(grow_ref[2 + di] + b_f) * LOG2E
        tri = (upper if di == 0 else lower).astype(BF16)
        brow = sum(jnp.dot(part, tri, preferred_element_type=F32) for part in _split2(lf))
        g = jnp.sum(lf, axis=1, keepdims=True)
        logw = g - brow + irow
        gb = jnp.broadcast_to(g, (nc, dh))
        mxb = jnp.broadcast_to(jnp.max(logw, axis=1, keepdims=True), (nc, dh))
        m = jnp.zeros((1, dh), F32)
        mprev, mnew = [None] * nc, [None] * nc
        for step in range(nc):
            c = step if di == 0 else nc - 1 - step
            mprev[c] = m
            m = jnp.maximum(gb[c:c + 1] + m, mxb[c:c + 1])
            mnew[c] = m
        mprev = jnp.concatenate(mprev, axis=0)
        mnew = jnp.concatenate(mnew, axis=0)
        lf_scr[di] = lf
        b_scr[di] = brow
        i_scr[di] = irow
        w_scr[di] = jnp.exp2(logw - mnew)
        mp_scr[di] = mprev
        dc_scr[di] = jnp.exp2(gb + mprev - mnew)

    ones_rhs = jnp.ones((2 * L, dh), BF16)
    chunk = lambda c: slice(c * L, (c + 1) * L)

    for c in range(nc):
        for di in range(2):
            lfr, brow, irow, mprev = (ref[di, c:c + 1, :] for ref in (lf_scr, b_scr, i_scr, mp_scr))
            causal = lower if di == 0 else upper
            hi, lo = _split2(jnp.where(causal, lfr, 0.0))
            bcol = jnp.dot(jnp.concatenate([hi, lo], axis=1), ones_rhs, preferred_element_type=F32)
            dmat = jnp.where(causal, bcol - brow + irow, _NEG)
            m_inter = bcol + mprev
            m_t = jnp.maximum(m_inter, jnp.max(dmat, axis=1, keepdims=True))
            e_scr[di, c] = jnp.exp2(dmat - m_t)
            a_scr[di, c] = jnp.exp2(m_inter - m_t)
            f_scr[di, c] = jnp.exp2(-m_t)

    for c in range(nc):
        qk = jnp.dot(q_scr[chunk(c), :], kt_scr[c], preferred_element_type=F32)
        for di in range(2):
            p_scr[di, c] = (qk * e_scr[di, c]).astype(BF16)

    states = [jnp.zeros((dh, 2 * dh), F32)] * 2
    for step in range(nc):
        for di in range(2):
            c = step if di == 0 else nc - 1 - step
            va = va_scr[chunk(c), :]
            nd_scr[di, c] = jnp.dot(p_scr[di, c], va, preferred_element_type=F32)
            cp_scr[di, c] = states[di].astype(BF16)
            ktw = (kt_scr[c].astype(F32) * w_scr[di, c:c + 1, :]).astype(BF16)
            decay = dc_scr[di, c:c + 1, :]
            states[di] = jnp.concatenate([decay, decay], axis=1) * states[di] \
                + jnp.dot(ktw, va, preferred_element_type=F32)

    for c in range(nc):
        qc = q_scr[chunk(c), :]
        hsum = None
        for di in range(2):
            a_in = a_scr[di, c]
            nd = jnp.concatenate([a_in, a_in], axis=1) * jnp.dot(qc, cp_scr[di, c], preferred_element_type=F32) \
                + nd_scr[di, c]
            hval = nd[:, :dh] * (1.0 / jnp.maximum(jnp.abs(nd[:, dh:]), f_scr[di, c]))
            hsum = hval if hsum is None else hsum + hval
        hs_scr[chunk(c), :] = hsum

    hs = hs_scr[...]
    mu = jnp.sum(hs, axis=1, keepdims=True) * (1.0 / dh)
    cen = hs - mu
    var = jnp.sum(cen * cen, axis=1, keepdims=True) * (1.0 / dh)
    hn = cen * lax.rsqrt(var + EPS)
    out_ref[...] = (hn * gn_ref[...] * jax.nn.sigmoid(o_ref[...])).astype(out_ref.dtype)


def _mlstm(gate_bias, u, vm, o, grow, cw, cb, wq, wk, gn):
    B, S, MW = u.shape
    H, dh = MLSTM_HEADS, MLSTM_HEAD_DIM
    L = MLSTM_CHUNK
    assert L == dh == LANES
    nc = S // L
    tok = pl.BlockSpec((None, S, dh), lambda b, h: (b, 0, h))
    return pl.pallas_call(
        _mlstm_kernel,
        grid=(B, H),
        in_specs=[pl.BlockSpec(memory_space=pltpu.SMEM), tok, tok, tok,
                  pl.BlockSpec((None, None, 4, nc, L), lambda b, h: (b, h, 0, 0, 0)),
                  pl.BlockSpec((3, dh), lambda b, h: (0, h)), pl.BlockSpec((1, dh), lambda b, h: (0, h)),
                  pl.BlockSpec((None, dh, dh), lambda b, h: (h, 0, 0)),
                  pl.BlockSpec((None, dh, dh), lambda b, h: (h, 0, 0)),
                  pl.BlockSpec((1, dh), lambda b, h: (0, h))],
        out_specs=tok,
        out_shape=jax.ShapeDtypeStruct((B, S, MW), BF16),
        scratch_shapes=[pltpu.VMEM((S, dh), BF16), pltpu.VMEM((nc, dh, L), BF16), pltpu.VMEM((S, 2 * dh), BF16),
                        pltpu.VMEM((S, dh), F32)] + [pltpu.VMEM((2, nc, L), F32)] * 6
        + [pltpu.VMEM((2, nc, L, L), F32), pltpu.VMEM((2, nc, L, dh), F32), pltpu.VMEM((2, nc, L, dh), F32),
           pltpu.VMEM((2, nc, L, L), BF16), pltpu.VMEM((2, nc, L, 2 * dh), F32),
           pltpu.VMEM((2, nc, dh, 2 * dh), BF16)],
        compiler_params=pltpu.CompilerParams(dimension_semantics=("arbitrary", "arbitrary"),
                                             vmem_limit_bytes=VMEM_LIMIT),
        name="mlstm",
    )(gate_bias, u, vm, o, grow, cw, cb, wq, wk, gn)


HALO = 16
FFN_CHUNK = 256


def _ffn_kernel(x_ref, xp_ref, xn_ref, a_ref, ap_ref, an_ref, m_ref, mp_ref, mn_ref, p_ref,
                wo_ref, g2_ref, wg_ref, wu_ref, wd_ref, cw_ref, cb_ref, gp_ref, wpg_ref, bpg_ref, wpp_ref, gf_ref,
                out_ref, mix_ref, hcat_ref, x1_ref, act_ref, *, tiles_per_seq):
    tm, aw = a_ref.shape
    ck = FFN_CHUNK
    nck = wg_ref.shape[1] // ck
    i = pl.program_id(0)
    first = (i % tiles_per_seq) == 0
    last = (i % tiles_per_seq) == tiles_per_seq - 1

    for lo, aa, mm in ((0, ap_ref, mp_ref), (HALO, a_ref, m_ref), (HALO + tm, an_ref, mn_ref)):
        mix_ref[lo:lo + aa.shape[0], :aw] = aa[...]
        mix_ref[lo:lo + aa.shape[0], aw:] = mm[...]
    y = jnp.dot(mix_ref[...], wo_ref[...], preferred_element_type=F32)
    x1 = x_ref[...] + y[HALO:HALO + tm]
    x1_ref[...] = x1
    g2 = g2_ref[...]
    hcat_ref[HALO:HALO + tm, :] = _rms(x1, g2).astype(BF16)
    hp = _rms(xp_ref[...] + y[:HALO], g2).astype(BF16)
    hn = _rms(xn_ref[...] + y[HALO + tm:], g2).astype(BF16)
    hcat_ref[0:HALO, :] = jnp.where(first, jnp.zeros_like(hp), hp)
    hcat_ref[HALO + tm:, :] = jnp.where(last, jnp.zeros_like(hn), hn)

    rows = tm + 2 * HALO
    for c in range(nck):
        cols = slice(c * ck, (c + 1) * ck)
        g = jnp.dot(hcat_ref[...], wg_ref[:, cols], preferred_element_type=F32)
        up = jnp.dot(hcat_ref[HALO:HALO + tm, :], wu_ref[:, cols], preferred_element_type=F32)
        cw = cw_ref[:, cols]
        a = pltpu.roll(g, 1, 0)[HALO:HALO + tm] * cw[0:1] + g[HALO:HALO + tm] * cw[1:2] \
            + pltpu.roll(g, rows - 1, 0)[HALO:HALO + tm] * cw[2:3] + cb_ref[:, cols]
        act_ref[:, cols] = (0.5 * a * (1.0 + lax.erf(a * (2.0 ** -0.5))) * up).astype(BF16)
    x2 = x1_ref[...] + jnp.dot(act_ref[...], wd_ref[...], preferred_element_type=F32)
    ms = jnp.sum(x2 * x2, axis=-1, keepdims=True) * (1.0 / x2.shape[-1])
    zg = jnp.dot((x2 * gp_ref[...]).astype(BF16), wpg_ref[...], preferred_element_type=F32)
    gate = jax.nn.sigmoid(zg * lax.rsqrt(ms + EPS) + bpg_ref[...])
    pp = jnp.dot(p_ref[...].astype(BF16), wpp_ref[...], preferred_element_type=F32)
    out_ref[...] = _rms(x2 + pp * gate, gf_ref[...])


def _ffn(x2d, attn, mem, p2, wo, g2, wg, wu, wd, cw, cb, gp, wpg, bpg, wpp, gf, tm, S):
    T, D = x2d.shape
    row = lambda w: pl.BlockSpec((tm, w), lambda i: (i, 0))
    hb = tm // HALO
    nhb = T // HALO
    prev = lambda w: pl.BlockSpec((HALO, w), lambda i: (jnp.maximum(i * hb - 1, 0), 0))
    nxt = lambda w: pl.BlockSpec((HALO, w), lambda i: (jnp.minimum((i + 1) * hb, nhb - 1), 0))
    aw, mw = attn.shape[1], mem.shape[1]
    consts = (wo, g2, wg, wu, wd, cw, cb, gp, wpg, bpg, wpp, gf)
    return pl.pallas_call(
        functools.partial(_ffn_kernel, tiles_per_seq=S // tm),
        grid=(T // tm,),
        in_specs=[row(D), prev(D), nxt(D), row(aw), prev(aw), nxt(aw), row(mw), prev(mw), nxt(mw),
                  row(p2.shape[1])] + [_const_spec(c.shape) for c in consts],
        out_specs=row(D),
        out_shape=jax.ShapeDtypeStruct((T, D), F32),
        scratch_shapes=[pltpu.VMEM((tm + 2 * HALO, aw + mw), BF16), pltpu.VMEM((tm + 2 * HALO, D), BF16),
                        pltpu.VMEM((tm, D), F32), pltpu.VMEM((tm, wg.shape[1]), BF16)],
        compiler_params=pltpu.CompilerParams(dimension_semantics=("arbitrary",), vmem_limit_bytes=VMEM_LIMIT),
        name="outproj_ffn_ple",
    )(x2d, x2d, x2d, attn, attn, attn, mem, mem, mem, p2, *consts)


def kernel(x, p, positions, ln_mix_g, w_in, mlstm_conv_w, mlstm_conv_b, w_mq, w_mk, b_igate, b_fgate,
           mlstm_gn_g, w_out, ln_ffn_g, w_ffn_gate, ffn_conv_w, ffn_conv_b, w_ffn_up, w_ffn_down,
           ln_ple_g, w_ple_gate, b_ple_gate, w_ple_proj, ln_final_g):
    B, S, D = x.shape
    depth = w_in.shape[0]
    T = B * S
    H, dh = MLSTM_HEADS, MLSTM_HEAD_DIM
    aw = D // 2
    mw = D - aw
    tm = ROW_TILE
    tm_in = INPROJ_TILE
    L = MLSTM_CHUNK
    nc = S // L

    bias = _attn_bias()
    pos3 = positions.reshape(T // tm_in, 1, tm_in)

    assert depth == 1, "the final norm is fused into the layer's last kernel"
    xf = x.reshape(T, D)
    for i in range(depth):
        nproj = 3 * aw + 3 * mw
        ng = w_in.shape[-1] - nproj
        w_pad = jnp.pad(w_in[i], ((0, 0), (0, LANES - ng))).astype(BF16)
        q, k, v, u, vm, o, gates_t = _inproj(xf, pos3, ln_mix_g[i][None], w_pad, ng, aw, mw, tm_in, S)

        attn = _attn(q, k, v, bias)

        grow = gates_t.reshape(4, H, B, nc, L).transpose(2, 1, 0, 3, 4)
        gate_bias = jnp.stack([b_igate[i], b_fgate[i]]).astype(F32)
        mem = _mlstm(gate_bias, u.reshape(B, S, mw), vm.reshape(B, S, mw), o.reshape(B, S, mw), grow,
                     mlstm_conv_w[i], mlstm_conv_b[i][None], w_mq[i].astype(BF16), w_mk[i].astype(BF16),
                     mlstm_gn_g[i][None])

        assert w_ffn_gate.shape[-1] % FFN_CHUNK == 0
        xf = _ffn(xf, attn.reshape(T, aw), mem.reshape(T, mw), p[i].reshape(T, -1),
                  w_out[i].astype(BF16), ln_ffn_g[i][None],
                  w_ffn_gate[i].astype(BF16), w_ffn_up[i].astype(BF16), w_ffn_down[i].astype(BF16),
                  ffn_conv_w[i], ffn_conv_b[i][None], ln_ple_g[i][None],
                  w_ple_gate[i].astype(BF16), b_ple_gate[i][None], w_ple_proj[i].astype(BF16), ln_final_g[None], tm, S)
    return xf.reshape(B, S, D)
```

```python
import functools
import math

import jax
import jax.numpy as jnp
import numpy as np
from jax import lax
from jax.experimental import pallas as pl
from jax.experimental.pallas import tpu as pltpu

F32 = jnp.float32
BF16 = jnp.bfloat16

EPS = 1e-6
ATTN_HEAD_DIM = 64
ATTN_PATTERNS = ((128, 1), (512, 4), (2048, 16))
ATTN_RADIUS = 64
ROPE_THETA = 500000.0
ROT_DIM = ATTN_HEAD_DIM // 4
MLSTM_HEADS = 4
MLSTM_HEAD_DIM = 128
LANES = 128
ATT_BLK = 128
ATT_REGION = 8
MLSTM_CHUNK = 128
VMEM_LIMIT = 56 * 1024 * 1024
ROW_TILE = 1024
INPROJ_TILE = 1024

_NEG = float("-inf")
LOG2E = math.log2(math.e)


def _rms(x, g):
    ms = jnp.sum(x * x, axis=-1, keepdims=True) * (1.0 / x.shape[-1])
    return x * lax.rsqrt(ms + EPS) * g


def _const_spec(shape):
    nd = len(shape)
    return pl.BlockSpec(shape, lambda *_: (0,) * nd, pipeline_mode=pl.Buffered(1))


def _inproj_kernel(x_ref, pos_ref, g_ref, invc_ref, sel_ref, unrot_ref, w_ref,
                   q_ref, k_ref, v_ref, u_ref, vm_ref, o_ref, gate_ref):
    aw3 = 3 * q_ref.shape[0] * LANES
    mw3 = 3 * u_ref.shape[1]
    ang = invc_ref[...] * pos_ref[...].astype(F32)
    half_rows = lax.broadcasted_iota(jnp.int32, ang.shape, 0) < ROT_DIM // 2
    tab = jnp.where(half_rows, jnp.cos(ang), jnp.sin(ang))
    cs = sum(lax.dot_general(part, sel_ref[...], (((0,), (0,)), ((), ())), preferred_element_type=F32)
             for part in _split2(tab))
    cos = cs[:, :LANES] + unrot_ref[...]
    sin = cs[:, LANES:]
    x = x_ref[...]
    h = (x * g_ref[...]).astype(BF16)
    rs = lax.rsqrt(jnp.sum(x * x, axis=-1, keepdims=True) * (1.0 / x.shape[-1]) + EPS)
    lane = lax.broadcasted_iota(jnp.int32, (1, LANES), 1)
    first_half = (lane % ATTN_HEAD_DIM) < (ROT_DIM // 2)

    def rotary(z):
        up = pltpu.roll(z, LANES - ROT_DIM // 2, 1)
        dn = pltpu.roll(z, ROT_DIM // 2, 1)
        return z * cos + jnp.where(first_half, up, dn) * sin

    zqkv = jnp.dot(h, w_ref[:, :aw3], preferred_element_type=F32) * rs
    aw = zqkv.shape[1] // 3
    q_scale = ATTN_HEAD_DIM ** -0.5 * LOG2E
    for j in range(aw // LANES):
        sl = slice(j * LANES, (j + 1) * LANES)
        q_ref[j] = rotary(zqkv[:, sl]) * q_scale
        k_ref[j] = rotary(zqkv[:, aw + j * LANES:aw + (j + 1) * LANES])
        v_ref[j] = zqkv[:, 2 * aw + j * LANES:2 * aw + (j + 1) * LANES]
    zuvo = jnp.dot(h, w_ref[:, aw3:aw3 + mw3], preferred_element_type=F32) * rs
    mw = mw3 // 3
    u_ref[...] = zuvo[:, :mw]
    vm_ref[...] = zuvo[:, mw:2 * mw].astype(BF16)
    o_ref[...] = zuvo[:, 2 * mw:]
    gates = jnp.dot(h, w_ref[:, aw3 + mw3:], preferred_element_type=F32) * rs
    gate_ref[...] = gates.T[:gate_ref.shape[0]]


def _rotary_tables():
    half = ROT_DIM // 2
    inv = jnp.power(ROPE_THETA, -jnp.arange(0, ROT_DIM, 2, dtype=F32) / ROT_DIM)
    invc = jnp.concatenate([inv, inv])[:, None]
    sel = np.zeros((2 * half, 2 * LANES), np.float32)
    unrot = np.ones((1, LANES), np.float32)
    for l in range(LANES):
        hl = l % ATTN_HEAD_DIM
        if hl < ROT_DIM:
            sel[hl % half, l] = 1.0
            sel[half + hl % half, LANES + l] = -1.0 if hl < half else 1.0
            unrot[0, l] = 0.0
    return invc, jnp.asarray(sel, BF16), jnp.asarray(unrot)


def _inproj(x2, pos3, g, w, ng, aw, mw, tm, S):
    T, D = x2.shape
    assert w.shape[1] == 3 * aw + 3 * mw + LANES
    row = lambda n: pl.BlockSpec((tm, n), lambda i: (i, 0))
    tps = S // tm
    npair = aw // LANES
    pair_major = pl.BlockSpec((None, npair, tm, LANES), lambda i: (i // tps, 0, i % tps, 0))
    invc, sel, unrot = _rotary_tables()
    return pl.pallas_call(
        _inproj_kernel,
        grid=(T // tm,),
        in_specs=[row(D), pl.BlockSpec((None, 1, tm), lambda i: (i, 0, 0)), _const_spec((1, D)),
                  _const_spec(invc.shape), _const_spec(sel.shape), _const_spec(unrot.shape), _const_spec(w.shape)],
        out_specs=[pair_major, pair_major, pair_major, row(mw), row(mw), row(mw),
                   pl.BlockSpec((ng, tm), lambda i: (0, i))],
        out_shape=[jax.ShapeDtypeStruct((T // S, npair, S, LANES), F32)] * 3
        + [jax.ShapeDtypeStruct((T, mw), F32), jax.ShapeDtypeStruct((T, mw), BF16),
           jax.ShapeDtypeStruct((T, mw), F32), jax.ShapeDtypeStruct((ng, T), F32)],
        compiler_params=pltpu.CompilerParams(dimension_semantics=("arbitrary",), vmem_limit_bytes=VMEM_LIMIT),
        name="inproj",
    )(x2, pos3, g, invc, sel, unrot, w)


def _attn_bias():
    W, R, Q = 2 * ATT_BLK, ATTN_RADIUS, ATT_BLK
    kj = np.arange(W)[:, None]
    qi = np.arange(Q)[None, :]
    interior = (kj - qi >= 0) & (kj - qi <= 2 * R)
    near = np.abs(kj - qi) <= R
    edge = ((qi < Q // 2) & (kj < Q) & near) | ((qi >= Q // 2) & (kj >= Q) & (np.abs(kj - Q - qi) <= R))
    single = near & (kj < Q)
    return jnp.asarray(np.where(np.stack([interior, edge, single]), 0.0, _NEG).astype(np.float32), BF16)


def _attn_kernel(q_ref, k_ref, v_ref, bias_ref, out_ref, qc_ref, kc_ref, vt0_ref, vt1_ref, o_scr, l_scr,
                 f4_scr, outf_scr):
    S = q_ref.shape[0]
    Q, HD = ATT_BLK, ATTN_HEAD_DIM
    lane = lax.broadcasted_iota(jnp.int32, (1, LANES), 1)
    head0 = lane < HD
    npat = len(ATTN_PATTERNS)
    vt0_ref[:, HD:, :] = jnp.ones((npat, HD, S), BF16)
    vt1_ref[:, :HD, :] = jnp.ones((npat, HD, S), BF16)

    def cat(parts, axis):
        return parts[0] if len(parts) == 1 else jnp.concatenate(parts, axis=axis)

    dm = ATTN_PATTERNS[1][1]
    assert [d for _, d in ATTN_PATTERNS] == [1, dm, dm * dm]
    Mm = S // dm

    def put(pi, base, xq, xk, xv):
        n = xq.shape[0]
        qc_ref[pi, base:base + n, :] = xq.astype(BF16)
        kc_ref[pi, base:base + n, :] = xk.astype(BF16)
        vt = xv.T.astype(BF16)
        vt0_ref[pi, :HD, base:base + n] = vt[:HD]
        vt1_ref[pi, HD:, base:base + n] = vt[HD:]

    put(0, 0, q_ref[...], k_ref[...], v_ref[...])
    for r in range(dm):
        xs = [src[pl.ds(r, Mm, stride=dm), :] for src in (q_ref, k_ref, v_ref)]
        for a, x in enumerate(xs):
            f4_scr[a, r * Mm:(r + 1) * Mm, :] = x
        put(1, r * Mm, *xs)
    for r in range(dm * dm):
        put(2, r * (Mm // dm), *[f4_scr[a, pl.ds((r % dm) * Mm + r // dm, Mm // dm, stride=dm), :] for a in range(3)])

    def out_rows(pi, r, start, size):
        if pi == 0:
            return pl.ds(start, size)
        if pi == 1:
            return pl.ds(r * Mm + start, size)
        return pl.ds((r % dm) * Mm + r // dm + dm * start, size, stride=dm)

    blocks = []
    for pi, (_, d) in enumerate(ATTN_PATTERNS):
        M = S // d
        nb = M // Q
        for r in range(d):
            base = r * M
            if nb == 1:
                blocks.append(dict(pi=pi, q=[slice(base, base + Q)], k=[slice(base, base + Q)], bias=2, W=Q,
                                   out=[(out_rows(pi, r, 0, Q), slice(0, Q))]))
                continue
            for i in range(nb - 1):
                k0 = base + i * Q
                blocks.append(dict(pi=pi, q=[slice(k0 + Q // 2, k0 + Q // 2 + Q)], k=[slice(k0, k0 + 2 * Q)],
                                   bias=0, W=2 * Q, out=[(out_rows(pi, r, i * Q + Q // 2, Q), slice(0, Q))]))
            blocks.append(dict(pi=pi, q=[slice(base, base + Q // 2), slice(base + M - Q // 2, base + M)],
                               k=[slice(base, base + Q), slice(base + M - Q, base + M)], bias=1, W=2 * Q,
                               out=[(out_rows(pi, r, 0, Q // 2), slice(0, Q // 2)),
                                    (out_rows(pi, r, M - Q // 2, Q // 2), slice(Q // 2, Q))]))

    def region_scores(region):
        scores = []
        for b in region:
            qb = cat([qc_ref[b["pi"], s, :] for s in b["q"]], 0)
            kw = cat([kc_ref[b["pi"], s, :] for s in b["k"]], 0)
            bias = bias_ref[b["bias"]][:b["W"]]
            for hh in range(2):
                qm = jnp.where(head0 if hh == 0 else jnp.logical_not(head0), qb, jnp.zeros_like(qb))
                s = lax.dot_general(kw, qm, (((1,), (1,)), ((), ())), preferred_element_type=F32)
                scores.append(s.astype(BF16) + bias)
        return scores

    def region_finish(region, scores):
        maxes = [jnp.max(s, axis=0, keepdims=True) for s in scores]
        probs = [jnp.exp2(s - m) for s, m in zip(scores, maxes)]
        maxes = [m.astype(F32) for m in maxes]
        accs = []
        for bi, b in enumerate(region):
            for hh, vref in enumerate((vt0_ref, vt1_ref)):
                vw = cat([vref[b["pi"], :, s] for s in b["k"]], 1)
                accs.append(jnp.dot(vw, probs[2 * bi + hh], preferred_element_type=F32))
        for bi, b in enumerate(region):
            o_t, l_t = [], []
            for hh in range(2):
                a = accs[2 * bi + hh]
                l = a[HD:HD + 1] if hh == 0 else a[0:1]
                o_t.append((a[:HD] if hh == 0 else a[HD:]) * (1.0 / l))
                l_t.append(jnp.broadcast_to(maxes[2 * bi + hh] + jnp.log2(l), (HD, Q)))
            o = jnp.concatenate(o_t, axis=0).T
            lse = jnp.concatenate(l_t, axis=0).T
            for rows, sl in b["out"]:
                o_scr[b["pi"], rows, :] = o[sl]
                l_scr[b["pi"], rows, :] = lse[sl]

    for g in range(0, len(blocks), ATT_REGION):
        region = blocks[g:g + ATT_REGION]
        region_finish(region, region_scores(region))

    for r in range(dm):
        tok = pl.ds(r, Mm, stride=dm)
        cls = slice(r * Mm, (r + 1) * Mm)
        os = (o_scr[0, tok, :], o_scr[1, cls, :], o_scr[2, cls, :])
        ls = (l_scr[0, tok, :], l_scr[1, cls, :], l_scr[2, cls, :])
        mx = jnp.maximum(jnp.maximum(ls[0], ls[1]), ls[2])
        ws = [jnp.exp2(l - mx) for l in ls]
        outf_scr[tok, :] = (ws[0] * os[0] + ws[1] * os[1] + ws[2] * os[2]) * (1.0 / (ws[0] + ws[1] + ws[2]))
    out_ref[...] = outf_scr[...].astype(out_ref.dtype)


def _attn(q, k, v, bias):
    B, npair, S, _ = q.shape
    blk = pl.BlockSpec((None, None, S, LANES), lambda b, j: (b, j, 0, 0))
    npat = len(ATTN_PATTERNS)
    return pl.pallas_call(
        _attn_kernel,
        grid=(B, npair),
        in_specs=[blk, blk, blk, _const_spec(bias.shape)],
        out_specs=pl.BlockSpec((None, S, LANES), lambda b, j: (b, 0, j)),
        out_shape=jax.ShapeDtypeStruct((B, S, npair * LANES), BF16),
        scratch_shapes=[pltpu.VMEM((npat, S, LANES), BF16), pltpu.VMEM((npat, S, LANES), BF16),
                        pltpu.VMEM((npat, LANES, S), BF16), pltpu.VMEM((npat, LANES, S), BF16),
                        pltpu.VMEM((npat, S, LANES), F32), pltpu.VMEM((npat, S, LANES), F32),
                        pltpu.VMEM((3, S, LANES), F32), pltpu.VMEM((S, LANES), F32)],
        compiler_params=pltpu.CompilerParams(dimension_semantics=("arbitrary", "arbitrary"),
                                             vmem_limit_bytes=VMEM_LIMIT),
        name="dilated_attn",
    )(q, k, v, bias)


def _split2(x):
    hi = x.astype(BF16)
    return hi, (x - hi.astype(F32)).astype(BF16)


def _log_sigmoid(x):
    return jnp.minimum(x, 0.0) - jnp.log1p(jnp.exp(-jnp.abs(x)))


def _mlstm_kernel(bias_ref, u_ref, vm_ref, o_ref, grow_ref, cw_ref, cb_ref, wq_ref, wk_ref, gn_ref,
                  out_ref, q_scr, kt_scr, va_scr, hs_scr, lf_scr, b_scr, i_scr, w_scr, mp_scr, dc_scr,
                  e_scr, a_scr, f_scr, p_scr, nd_scr, cp_scr):
    S, dh = u_ref.shape
    L = MLSTM_CHUNK
    nc = S // L
    hd = pl.program_id(1)

    u = u_ref[...]
    rows = lax.broadcasted_iota(jnp.int32, (S, 1), 0)
    u_prev = jnp.where(rows == 0, 0.0, pltpu.roll(u, 1, 0))
    u_next = jnp.where(rows == S - 1, 0.0, pltpu.roll(u, S - 1, 0))
    cw = cw_ref[...]
    a = u_prev * cw[0:1] + u * cw[1:2] + u_next * cw[2:3] + cb_ref[...]
    uc = (a * jax.nn.sigmoid(a)).astype(BF16)
    q_scr[...] = jnp.dot(uc, wq_ref[...], preferred_element_type=F32).astype(BF16)
    k = jnp.dot(uc, wk_ref[...], preferred_element_type=F32) * (dh ** -0.5)
    for c in range(nc):
        kt_scr[c] = k[c * L:(c + 1) * L].T.astype(BF16)
    va_scr[:, :dh] = vm_ref[...]
    va_scr[:, dh:] = jnp.ones((S, dh), BF16)

    ri = lax.broadcasted_iota(jnp.int32, (L, L), 0)
    ci = lax.broadcasted_iota(jnp.int32, (L, L), 1)
    lower = ri >= ci
    upper = ri <= ci

    for di in range(2):
        b_i = bias_ref[0, di, hd]
        b_f = bias_ref[1, di, hd]
        irow = (grow_ref[di] + b_i) * LOG2E
        lf = _log_sigmoid(grow_ref[2 + di] + b_f) * LOG2E
        tri = (upper if di == 0 else lower).astype(BF16)
        brow = sum(jnp.dot(part, tri, preferred_element_type=F32) for part in _split2(lf))
        g = jnp.sum(lf, axis=1, keepdims=True)
        logw = g - brow + irow
        gb = jnp.broadcast_to(g, (nc, dh))
        mxb = jnp.broadcast_to(jnp.max(logw, axis=1, keepdims=True), (nc, dh))
        m = jnp.zeros((1, dh), F32)
        mprev, mnew = [None] * nc, [None] * nc
        for step in range(nc):
            c = step if di == 0 else nc - 1 - step
            mprev[c] = m
            m = jnp.maximum(gb[c:c + 1] + m, mxb[c:c + 1])
            mnew[c] = m
        mprev = jnp.concatenate(mprev, axis=0)
        mnew = jnp.concatenate(mnew, axis=0)
        lf_scr[di] = lf
        b_scr[di] = brow
        i_scr[di] = irow
        w_scr[di] = jnp.exp2(logw - mnew)
        mp_scr[di] = mprev
        dc_scr[di] = jnp.exp2(gb + mprev - mnew)

    ones_rhs = jnp.ones((2 * L, dh), BF16)
    chunk = lambda c: slice(c * L, (c + 1) * L)

    for c in range(nc):
        for di in range(2):
            lfr, brow, irow, mprev = (ref[di, c:c + 1, :] for ref in (lf_scr, b_scr, i_scr, mp_scr))
            causal = lower if di == 0 else upper
            hi, lo = _split2(jnp.where(causal, lfr, 0.0))
            bcol = jnp.dot(jnp.concatenate([hi, lo], axis=1), ones_rhs, preferred_element_type=F32)
            dmat = jnp.where(causal, bcol - brow + irow, _NEG)
            m_inter = bcol + mprev
            m_t = jnp.maximum(m_inter, jnp.max(dmat, axis=1, keepdims=True))
            e_scr[di, c] = jnp.exp2(dmat - m_t)
            a_scr[di, c] = jnp.exp2(m_inter - m_t)
            f_scr[di, c] = jnp.exp2(-m_t)

    for c in range(nc):
        qk = jnp.dot(q_scr[chunk(c), :], kt_scr[c], preferred_element_type=F32)
        for di in range(2):
            p_scr[di, c] = (qk * e_scr[di, c]).astype(BF16)

    states = [jnp.zeros((dh, 2 * dh), F32)] * 2
    for step in range(nc):
        for di in range(2):
            c = step if di == 0 else nc - 1 - step
            va = va_scr[chunk(c), :]
            nd_scr[di, c] = jnp.dot(p_scr[di, c], va, preferred_element_type=F32)
            cp_scr[di, c] = states[di].astype(BF16)
            ktw = (kt_scr[c].astype(F32) * w_scr[di, c:c + 1, :]).astype(BF16)
            decay = dc_scr[di, c:c + 1, :]
            states[di] = jnp.concatenate([decay, decay], axis=1) * states[di] \
                + jnp.dot(ktw, va, preferred_element_type=F32)

    for c in range(nc):
        qc = q_scr[chunk(c), :]
        hsum = None
        for di in range(2):
            a_in = a_scr[di, c]
            nd = jnp.concatenate([a_in, a_in], axis=1) * jnp.dot(qc, cp_scr[di, c], preferred_element_type=F32) \
                + nd_scr[di, c]
            hval = nd[:, :dh] * (1.0 / jnp.maximum(jnp.abs(nd[:, dh:]), f_scr[di, c]))
            hsum = hval if hsum is None else hsum + hval
        hs_scr[chunk(c), :] = hsum

    hs = hs_scr[...]
    mu = jnp.sum(hs, axis=1, keepdims=True) * (1.0 / dh)
    cen = hs - mu
    var = jnp.sum(cen * cen, axis=1, keepdims=True) * (1.0 / dh)
    hn = cen * lax.rsqrt(var + EPS)
    out_ref[...] = (hn * gn_ref[...] * jax.nn.sigmoid(o_ref[...])).astype(out_ref.dtype)


def _mlstm(gate_bias, u, vm, o, grow, cw, cb, wq, wk, gn):
    B, S, MW = u.shape
    H, dh = MLSTM_HEADS, MLSTM_HEAD_DIM
    L = MLSTM_CHUNK
    assert L == dh == LANES
    nc = S // L
    tok = pl.BlockSpec((None, S, dh), lambda b, h: (b, 0, h))
    return pl.pallas_call(
        _mlstm_kernel,
        grid=(B, H),
        in_specs=[pl.BlockSpec(memory_space=pltpu.SMEM), tok, tok, tok,
                  pl.BlockSpec((None, None, 4, nc, L), lambda b, h: (b, h, 0, 0, 0)),
                  pl.BlockSpec((3, dh), lambda b, h: (0, h)), pl.BlockSpec((1, dh), lambda b, h: (0, h)),
                  pl.BlockSpec((None, dh, dh), lambda b, h: (h, 0, 0)),
                  pl.BlockSpec((None, dh, dh), lambda b, h: (h, 0, 0)),
                  pl.BlockSpec((1, dh), lambda b, h: (0, h))],
        out_specs=tok,
        out_shape=jax.ShapeDtypeStruct((B, S, MW), BF16),
        scratch_shapes=[pltpu.VMEM((S, dh), BF16), pltpu.VMEM((nc, dh, L), BF16), pltpu.VMEM((S, 2 * dh), BF16),
                        pltpu.VMEM((S, dh), F32)] + [pltpu.VMEM((2, nc, L), F32)] * 6
        + [pltpu.VMEM((2, nc, L, L), F32), pltpu.VMEM((2, nc, L, dh), F32), pltpu.VMEM((2, nc, L, dh), F32),
           pltpu.VMEM((2, nc, L, L), BF16), pltpu.VMEM((2, nc, L, 2 * dh), F32),
           pltpu.VMEM((2, nc, dh, 2 * dh), BF16)],
        compiler_params=pltpu.CompilerParams(dimension_semantics=("arbitrary", "arbitrary"),
                                             vmem_limit_bytes=VMEM_LIMIT),
        name="mlstm",
    )(gate_bias, u, vm, o, grow, cw, cb, wq, wk, gn)


HALO = 16
FFN_CHUNK = 256


def _ffn_kernel(x_ref, xp_ref, xn_ref, a_ref, ap_ref, an_ref, m_ref, mp_ref, mn_ref, p_ref,
                wo_ref, g2_ref, wg_ref, wu_ref, wd_ref, cw_ref, cb_ref, gp_ref, wpg_ref, bpg_ref, wpp_ref, gf_ref,
                out_ref, mix_ref, hcat_ref, act_ref, *, tiles_per_seq):
    tm, aw = a_ref.shape
    ck = FFN_CHUNK
    nck = wg_ref.shape[1] // ck
    i = pl.program_id(0)
    first = (i % tiles_per_seq) == 0
    last = (i % tiles_per_seq) == tiles_per_seq - 1

    for lo, aa, mm in ((0, ap_ref, mp_ref), (HALO, a_ref, m_ref), (HALO + tm, an_ref, mn_ref)):
        mix_ref[lo:lo + aa.shape[0], :aw] = aa[...]
        mix_ref[lo:lo + aa.shape[0], aw:] = mm[...]
    y = jnp.dot(mix_ref[...], wo_ref[...], preferred_element_type=F32)
    x1 = x_ref[...] + y[HALO:HALO + tm]
    out_ref[...] = x1
    g2 = g2_ref[...]
    hcat_ref[HALO:HALO + tm, :] = _rms(x1, g2).astype(BF16)
    hp = _rms(xp_ref[...] + y[:HALO], g2).astype(BF16)
    hn = _rms(xn_ref[...] + y[HALO + tm:], g2).astype(BF16)
    hcat_ref[0:HALO, :] = jnp.where(first, jnp.zeros_like(hp), hp)
    hcat_ref[HALO + tm:, :] = jnp.where(last, jnp.zeros_like(hn), hn)

    rows = tm + 2 * HALO
    for c in range(nck):
        cols = slice(c * ck, (c + 1) * ck)
        g = jnp.dot(hcat_ref[...], wg_ref[:, cols], preferred_element_type=F32)
        up = jnp.dot(hcat_ref[HALO:HALO + tm, :], wu_ref[:, cols], preferred_element_type=F32)
        cw = cw_ref[:, cols]
        a = pltpu.roll(g, 1, 0)[HALO:HALO + tm] * cw[0:1] + g[HALO:HALO + tm] * cw[1:2] \
            + pltpu.roll(g, rows - 1, 0)[HALO:HALO + tm] * cw[2:3] + cb_ref[:, cols]
        act_ref[:, cols] = (0.5 * a * (1.0 + lax.erf(a * (2.0 ** -0.5))) * up).astype(BF16)
    x2 = out_ref[...] + jnp.dot(act_ref[...], wd_ref[...], preferred_element_type=F32)
    ms = jnp.sum(x2 * x2, axis=-1, keepdims=True) * (1.0 / x2.shape[-1])
    zg = jnp.dot((x2 * gp_ref[...]).astype(BF16), wpg_ref[...], preferred_element_type=F32)
    gate = jax.nn.sigmoid(zg * lax.rsqrt(ms + EPS) + bpg_ref[...])
    pp = jnp.dot(p_ref[...].astype(BF16), wpp_ref[...], preferred_element_type=F32)
    out_ref[...] = _rms(x2 + pp * gate, gf_ref[...])


def _ffn(x2d, attn, mem, p2, wo, g2, wg, wu, wd, cw, cb, gp, wpg, bpg, wpp, gf, tm, S):
    T, D = x2d.shape
    row = lambda w: pl.BlockSpec((tm, w), lambda i: (i, 0))
    hb = tm // HALO
    nhb = T // HALO
    prev = lambda w: pl.BlockSpec((HALO, w), lambda i: (jnp.maximum(i * hb - 1, 0), 0))
    nxt = lambda w: pl.BlockSpec((HALO, w), lambda i: (jnp.minimum((i + 1) * hb, nhb - 1), 0))
    aw, mw = attn.shape[1], mem.shape[1]
    consts = (wo, g2, wg, wu, wd, cw, cb, gp, wpg, bpg, wpp, gf)
    return pl.pallas_call(
        functools.partial(_ffn_kernel, tiles_per_seq=S // tm),
        grid=(T // tm,),
        in_specs=[row(D), prev(D), nxt(D), row(aw), prev(aw), nxt(aw), row(mw), prev(mw), nxt(mw),
                  row(p2.shape[1])] + [_const_spec(c.shape) for c in consts],
        out_specs=row(D),
        out_shape=jax.ShapeDtypeStruct((T, D), F32),
        scratch_shapes=[pltpu.VMEM((tm + 2 * HALO, aw + mw), BF16), pltpu.VMEM((tm + 2 * HALO, D), BF16),
                        pltpu.VMEM((tm, wg.shape[1]), BF16)],
        compiler_params=pltpu.CompilerParams(dimension_semantics=("arbitrary",), vmem_limit_bytes=VMEM_LIMIT),
        name="outproj_ffn_ple",
    )(x2d, x2d, x2d, attn, attn, attn, mem, mem, mem, p2, *consts)


def kernel(x, p, positions, ln_mix_g, w_in, mlstm_conv_w, mlstm_conv_b, w_mq, w_mk, b_igate, b_fgate,
           mlstm_gn_g, w_out, ln_ffn_g, w_ffn_gate, ffn_conv_w, ffn_conv_b, w_ffn_up, w_ffn_down,
           ln_ple_g, w_ple_gate, b_ple_gate, w_ple_proj, ln_final_g):
    B, S, D = x.shape
    depth = w_in.shape[0]
    T = B * S
    H, dh = MLSTM_HEADS, MLSTM_HEAD_DIM
    aw = D // 2
    mw = D - aw
    tm = ROW_TILE
    tm_in = INPROJ_TILE
    L = MLSTM_CHUNK
    nc = S // L

    bias = _attn_bias()
    pos3 = positions.reshape(T // tm_in, 1, tm_in)

    assert depth == 1, "the final norm is fused into the layer's last kernel"
    xf = x.reshape(T, D)
    for i in range(depth):
        nproj = 3 * aw + 3 * mw
        ng = w_in.shape[-1] - nproj
        w_pad = jnp.pad(w_in[i], ((0, 0), (0, LANES - ng))).astype(BF16)
        q, k, v, u, vm, o, gates_t = _inproj(xf, pos3, ln_mix_g[i][None], w_pad, ng, aw, mw, tm_in, S)

        attn = _attn(q, k, v, bias)

        grow = gates_t.reshape(4, H, B, nc, L).transpose(2, 1, 0, 3, 4)
        gate_bias = jnp.stack([b_igate[i], b_fgate[i]]).astype(F32)
        mem = _mlstm(gate_bias, u.reshape(B, S, mw), vm.reshape(B, S, mw), o.reshape(B, S, mw), grow,
                     mlstm_conv_w[i], mlstm_conv_b[i][None], w_mq[i].astype(BF16), w_mk[i].astype(BF16),
                     mlstm_gn_g[i][None])

        assert w_ffn_gate.shape[-1] % FFN_CHUNK == 0
        xf = _ffn(xf, attn.reshape(T, aw), mem.reshape(T, mw), p[i].reshape(T, -1),
                  w_out[i].astype(BF16), ln_ffn_g[i][None],
                  w_ffn_gate[i].astype(BF16), w_ffn_up[i].astype(BF16), w_ffn_down[i].astype(BF16),
                  ffn_conv_w[i], ffn_conv_b[i][None], ln_ple_g[i][None],
                  w_ple_gate[i].astype(BF16), b_ple_gate[i][None], w_ple_proj[i].astype(BF16), ln_final_g[None], tm, S)
    return xf.reshape(B, S, D)
```

```python
import functools
import math

import jax
import jax.numpy as jnp
import numpy as np
from jax import lax
from jax.experimental import pallas as pl
from jax.experimental.pallas import tpu as pltpu

F32 = jnp.float32
BF16 = jnp.bfloat16

EPS = 1e-6
ATTN_HEAD_DIM = 64
ATTN_PATTERNS = ((128, 1), (512, 4), (2048, 16))
ATTN_RADIUS = 64
ROPE_THETA = 500000.0
ROT_DIM = ATTN_HEAD_DIM // 4
MLSTM_HEADS = 4
MLSTM_HEAD_DIM = 128
LANES = 128
ATT_BLK = 128
ATT_REGION = 8
MLSTM_CHUNK = 128
VMEM_LIMIT = 56 * 1024 * 1024
ROW_TILE = 1024
INPROJ_TILE = 1024
HALO = 16

_NEG = float("-inf")
LOG2E = math.log2(math.e)


def _rms(x, g):
    ms = jnp.sum(x * x, axis=-1, keepdims=True) * (1.0 / x.shape[-1])
    return x * lax.rsqrt(ms + EPS) * g


def _const_spec(shape):
    nd = len(shape)
    return pl.BlockSpec(shape, lambda *_: (0,) * nd, pipeline_mode=pl.Buffered(1))


def _inproj_kernel(x_ref, xp_ref, xn_ref, pos_ref, g_ref, invc_ref, sel_ref, unrot_ref, w_ref, cw_ref, cb_ref,
                   q_ref, k_ref, v_ref, uc_ref, vm_ref, og_ref, gate_ref, hcat_ref, *, tiles_per_seq):
    tm = x_ref.shape[0]
    aw3 = 3 * q_ref.shape[0] * LANES
    mw = uc_ref.shape[1]
    i = pl.program_id(0)
    first = (i % tiles_per_seq) == 0
    last = (i % tiles_per_seq) == tiles_per_seq - 1
    ang = invc_ref[...] * pos_ref[...].astype(F32)
    half_rows = lax.broadcasted_iota(jnp.int32, ang.shape, 0) < ROT_DIM // 2
    tab = jnp.where(half_rows, jnp.cos(ang), jnp.sin(ang))
    cs = sum(lax.dot_general(part, sel_ref[...], (((0,), (0,)), ((), ())), preferred_element_type=F32)
             for part in _split2(tab))
    cos = cs[:, :LANES] + unrot_ref[...]
    sin = cs[:, LANES:]
    def prep(x):
        scale = lax.rsqrt(jnp.sum(x * x, axis=-1, keepdims=True) * (1.0 / x.shape[-1]) + EPS)
        return (x * g_ref[...]).astype(BF16), scale

    h, rs = prep(x_ref[...])
    hp, rsp = prep(xp_ref[...])
    hn, rsn = prep(xn_ref[...])
    hcat_ref[0:HALO, :] = hp
    hcat_ref[HALO:HALO + tm, :] = h
    hcat_ref[HALO + tm:, :] = hn
    lane = lax.broadcasted_iota(jnp.int32, (1, LANES), 1)
    first_half = (lane % ATTN_HEAD_DIM) < (ROT_DIM // 2)

    def rotary(z):
        up = pltpu.roll(z, LANES - ROT_DIM // 2, 1)
        dn = pltpu.roll(z, ROT_DIM // 2, 1)
        return z * cos + jnp.where(first_half, up, dn) * sin

    aw = aw3 // 3
    npair = aw // LANES

    def proj(c0, width):
        return jnp.dot(h, w_ref[:, c0:c0 + width], preferred_element_type=F32) * rs

    zqkv = proj(0, aw3)
    q_scale = ATTN_HEAD_DIM ** -0.5 * LOG2E
    for j in range(npair):
        sl = slice(j * LANES, (j + 1) * LANES)
        q_ref[j] = rotary(zqkv[:, sl]) * q_scale
        k_ref[j] = rotary(zqkv[:, aw + j * LANES:aw + (j + 1) * LANES])
        v_ref[j] = zqkv[:, 2 * aw + j * LANES:2 * aw + (j + 1) * LANES]
    rows = tm + 2 * HALO
    ridx = lax.broadcasted_iota(jnp.int32, (rows, 1), 0)
    outside = ((ridx < HALO) & first) | ((ridx >= HALO + tm) & last)
    zu = jnp.dot(hcat_ref[...], w_ref[:, aw3:aw3 + mw], preferred_element_type=F32) \
        * jnp.concatenate([rsp, rs, rsn], axis=0)
    zu = jnp.where(outside, 0.0, zu)
    cw = cw_ref[...]
    a = pltpu.roll(zu, 1, 0)[HALO:HALO + tm] * cw[0:1] + zu[HALO:HALO + tm] * cw[1:2] \
        + pltpu.roll(zu, rows - 1, 0)[HALO:HALO + tm] * cw[2:3] + cb_ref[...]
    uc_ref[...] = (a * jax.nn.sigmoid(a)).astype(BF16)
    zvo = proj(aw3 + mw, 2 * mw)
    vm_ref[...] = zvo[:, :mw].astype(BF16)
    og_ref[...] = jax.nn.sigmoid(zvo[:, mw:])
    gates = proj(aw3 + 3 * mw, LANES)
    gate_ref[...] = gates.T[:gate_ref.shape[0]]


def _rotary_tables():
    half = ROT_DIM // 2
    inv = jnp.power(ROPE_THETA, -jnp.arange(0, ROT_DIM, 2, dtype=F32) / ROT_DIM)
    invc = jnp.concatenate([inv, inv])[:, None]
    sel = np.zeros((2 * half, 2 * LANES), np.float32)
    unrot = np.ones((1, LANES), np.float32)
    for l in range(LANES):
        hl = l % ATTN_HEAD_DIM
        if hl < ROT_DIM:
            sel[hl % half, l] = 1.0
            sel[half + hl % half, LANES + l] = -1.0 if hl < half else 1.0
            unrot[0, l] = 0.0
    return invc, jnp.asarray(sel, BF16), jnp.asarray(unrot)


def _inproj(x2, pos3, g, w, cw, cb, ng, aw, mw, tm, S):
    T, D = x2.shape
    assert w.shape[1] == 3 * aw + 3 * mw + LANES
    row = lambda n: pl.BlockSpec((tm, n), lambda i: (i, 0))
    hb = tm // HALO
    nhb = T // HALO
    prev = pl.BlockSpec((HALO, D), lambda i: (jnp.maximum(i * hb - 1, 0), 0))
    nxt = pl.BlockSpec((HALO, D), lambda i: (jnp.minimum((i + 1) * hb, nhb - 1), 0))
    tps = S // tm
    npair = aw // LANES
    pair_major = pl.BlockSpec((None, npair, tm, LANES), lambda i: (i // tps, 0, i % tps, 0))
    invc, sel, unrot = _rotary_tables()
    consts = (g, invc, sel, unrot, w, cw, cb)
    return pl.pallas_call(
        functools.partial(_inproj_kernel, tiles_per_seq=tps),
        grid=(T // tm,),
        in_specs=[row(D), prev, nxt, pl.BlockSpec((None, 1, tm), lambda i: (i, 0, 0))]
        + [_const_spec(c.shape) for c in consts],
        out_specs=[pair_major, pair_major, pair_major, row(mw), row(mw), row(mw),
                   pl.BlockSpec((ng, tm), lambda i: (0, i))],
        out_shape=[jax.ShapeDtypeStruct((T // S, npair, S, LANES), F32)] * 3
        + [jax.ShapeDtypeStruct((T, mw), BF16), jax.ShapeDtypeStruct((T, mw), BF16),
           jax.ShapeDtypeStruct((T, mw), F32), jax.ShapeDtypeStruct((ng, T), F32)],
        scratch_shapes=[pltpu.VMEM((tm + 2 * HALO, D), BF16)],
        compiler_params=pltpu.CompilerParams(dimension_semantics=("arbitrary",), vmem_limit_bytes=VMEM_LIMIT),
        name="inproj",
    )(x2, x2, x2, pos3, *consts)


def _attn_bias():
    W, R, Q = 2 * ATT_BLK, ATTN_RADIUS, ATT_BLK
    kj = np.arange(W)[:, None]
    qi = np.arange(Q)[None, :]
    interior = (kj - qi >= 0) & (kj - qi <= 2 * R)
    near = np.abs(kj - qi) <= R
    edge = ((qi < Q // 2) & (kj < Q) & near) | ((qi >= Q // 2) & (kj >= Q) & (np.abs(kj - Q - qi) <= R))
    single = near & (kj < Q)
    return jnp.asarray(np.where(np.stack([interior, edge, single]), 0.0, _NEG).astype(np.float32), BF16)


def _attn_kernel(q_ref, k_ref, v_ref, bias_ref, out_ref, qc_ref, kc_ref, vt0_ref, vt1_ref, o_scr, l_scr,
                 f4_scr, outf_scr):
    S = q_ref.shape[0]
    Q, HD = ATT_BLK, ATTN_HEAD_DIM
    lane = lax.broadcasted_iota(jnp.int32, (1, LANES), 1)
    head0 = lane < HD
    npat = len(ATTN_PATTERNS)
    vt0_ref[:, HD:, :] = jnp.ones((npat, HD, S), BF16)
    vt1_ref[:, :HD, :] = jnp.ones((npat, HD, S), BF16)

    def cat(parts, axis):
        return parts[0] if len(parts) == 1 else jnp.concatenate(parts, axis=axis)

    dm = ATTN_PATTERNS[1][1]
    assert [d for _, d in ATTN_PATTERNS] == [1, dm, dm * dm]
    Mm = S // dm

    def put(pi, base, xq, xk, xv):
        n = xq.shape[0]
        qc_ref[pi, base:base + n, :] = xq.astype(BF16)
        kc_ref[pi, base:base + n, :] = xk.astype(BF16)
        vt = xv.T.astype(BF16)
        vt0_ref[pi, :HD, base:base + n] = vt[:HD]
        vt1_ref[pi, HD:, base:base + n] = vt[HD:]

    put(0, 0, q_ref[...], k_ref[...], v_ref[...])
    for r in range(dm):
        xs = [src[pl.ds(r, Mm, stride=dm), :] for src in (q_ref, k_ref, v_ref)]
        for a, x in enumerate(xs):
            f4_scr[a, r * Mm:(r + 1) * Mm, :] = x
        put(1, r * Mm, *xs)
    for r in range(dm * dm):
        put(2, r * (Mm // dm), *[f4_scr[a, pl.ds((r % dm) * Mm + r // dm, Mm // dm, stride=dm), :] for a in range(3)])

    def out_rows(pi, r, start, size):
        if pi == 0:
            return pl.ds(start, size)
        if pi == 1:
            return pl.ds(r * Mm + start, size)
        return pl.ds((r % dm) * Mm + r // dm + dm * start, size, stride=dm)

    blocks = []
    for pi, (_, d) in enumerate(ATTN_PATTERNS):
        M = S // d
        nb = M // Q
        for r in range(d):
            base = r * M
            if nb == 1:
                blocks.append(dict(pi=pi, q=[slice(base, base + Q)], k=[slice(base, base + Q)], bias=2, W=Q,
                                   out=[(out_rows(pi, r, 0, Q), slice(0, Q))]))
                continue
            for i in range(nb - 1):
                k0 = base + i * Q
                blocks.append(dict(pi=pi, q=[slice(k0 + Q // 2, k0 + Q // 2 + Q)], k=[slice(k0, k0 + 2 * Q)],
                                   bias=0, W=2 * Q, out=[(out_rows(pi, r, i * Q + Q // 2, Q), slice(0, Q))]))
            blocks.append(dict(pi=pi, q=[slice(base, base + Q // 2), slice(base + M - Q // 2, base + M)],
                               k=[slice(base, base + Q), slice(base + M - Q, base + M)], bias=1, W=2 * Q,
                               out=[(out_rows(pi, r, 0, Q // 2), slice(0, Q // 2)),
                                    (out_rows(pi, r, M - Q // 2, Q // 2), slice(Q // 2, Q))]))

    def region_scores(region):
        scores = []
        for b in region:
            qb = cat([qc_ref[b["pi"], s, :] for s in b["q"]], 0)
            kw = cat([kc_ref[b["pi"], s, :] for s in b["k"]], 0)
            bias = bias_ref[b["bias"]][:b["W"]]
            for hh in range(2):
                qm = jnp.where(head0 if hh == 0 else jnp.logical_not(head0), qb, jnp.zeros_like(qb))
                s = lax.dot_general(kw, qm, (((1,), (1,)), ((), ())), preferred_element_type=F32)
                scores.append(s.astype(BF16) + bias)
        return scores

    def region_finish(region, scores):
        maxes = [jnp.max(s, axis=0, keepdims=True) for s in scores]
        probs = [jnp.exp2(s - m) for s, m in zip(scores, maxes)]
        maxes = [m.astype(F32) for m in maxes]
        accs = []
        for bi, b in enumerate(region):
            for hh, vref in enumerate((vt0_ref, vt1_ref)):
                vw = cat([vref[b["pi"], :, s] for s in b["k"]], 1)
                accs.append(jnp.dot(vw, probs[2 * bi + hh], preferred_element_type=F32))
        for bi, b in enumerate(region):
            o_t, l_t = [], []
            for hh in range(2):
                a = accs[2 * bi + hh]
                l = a[HD:HD + 1] if hh == 0 else a[0:1]
                o_t.append((a[:HD] if hh == 0 else a[HD:]) * (1.0 / l))
                l_t.append(jnp.broadcast_to(maxes[2 * bi + hh] + jnp.log2(l), (HD, Q)))
            o = jnp.concatenate(o_t, axis=0).T
            lse = jnp.concatenate(l_t, axis=0).T
            for rows, sl in b["out"]:
                o_scr[b["pi"], rows, :] = o[sl]
                l_scr[b["pi"], rows, :] = lse[sl]

    for g in range(0, len(blocks), ATT_REGION):
        region = blocks[g:g + ATT_REGION]
        region_finish(region, region_scores(region))

    for r in range(dm):
        tok = pl.ds(r, Mm, stride=dm)
        cls = slice(r * Mm, (r + 1) * Mm)
        os = (o_scr[0, tok, :], o_scr[1, cls, :], o_scr[2, cls, :])
        ls = (l_scr[0, tok, :], l_scr[1, cls, :], l_scr[2, cls, :])
        mx = jnp.maximum(jnp.maximum(ls[0], ls[1]), ls[2])
        ws = [jnp.exp2(l - mx) for l in ls]
        outf_scr[tok, :] = (ws[0] * os[0] + ws[1] * os[1] + ws[2] * os[2]) * (1.0 / (ws[0] + ws[1] + ws[2]))
    out_ref[...] = outf_scr[...].astype(out_ref.dtype)


def _attn(q, k, v, bias):
    B, npair, S, _ = q.shape
    blk = pl.BlockSpec((None, None, S, LANES), lambda b, j: (b, j, 0, 0))
    npat = len(ATTN_PATTERNS)
    return pl.pallas_call(
        _attn_kernel,
        grid=(B, npair),
        in_specs=[blk, blk, blk, _const_spec(bias.shape)],
        out_specs=pl.BlockSpec((None, S, LANES), lambda b, j: (b, 0, j)),
        out_shape=jax.ShapeDtypeStruct((B, S, npair * LANES), BF16),
        scratch_shapes=[pltpu.VMEM((npat, S, LANES), BF16), pltpu.VMEM((npat, S, LANES), BF16),
                        pltpu.VMEM((npat, LANES, S), BF16), pltpu.VMEM((npat, LANES, S), BF16),
                        pltpu.VMEM((npat, S, LANES), F32), pltpu.VMEM((npat, S, LANES), F32),
                        pltpu.VMEM((3, S, LANES), F32), pltpu.VMEM((S, LANES), F32)],
        compiler_params=pltpu.CompilerParams(dimension_semantics=("arbitrary", "arbitrary"),
                                             vmem_limit_bytes=VMEM_LIMIT),
        name="dilated_attn",
    )(q, k, v, bias)


def _split2(x):
    hi = x.astype(BF16)
    return hi, (x - hi.astype(F32)).astype(BF16)


def _log_sigmoid(x):
    return jnp.minimum(x, 0.0) - jnp.log1p(jnp.exp(-jnp.abs(x)))


def _mlstm_kernel(bias_ref, uc_ref, vm_ref, og_ref, grow_ref, wq_ref, wk_ref, gn_ref,
                  out_ref, q_scr, kt_scr, va_scr, hs_scr, lf_scr, b_scr, i_scr, w_scr, mp_scr, dc_scr,
                  e_scr, a_scr, f_scr, p_scr, nd_scr, cp_scr):
    S, dh = uc_ref.shape
    L = MLSTM_CHUNK
    nc = S // L
    hd = pl.program_id(1)

    uc = uc_ref[...]
    q_scr[...] = jnp.dot(uc, wq_ref[...], preferred_element_type=F32).astype(BF16)
    k = jnp.dot(uc, wk_ref[...], preferred_element_type=F32) * (dh ** -0.5)
    for c in range(nc):
        kt_scr[c] = k[c * L:(c + 1) * L].T.astype(BF16)
    va_scr[:, :dh] = vm_ref[...]
    va_scr[:, dh:] = jnp.ones((S, dh), BF16)

    ri = lax.broadcasted_iota(jnp.int32, (L, L), 0)
    ci = lax.broadcasted_iota(jnp.int32, (L, L), 1)
    lower = ri >= ci
    upper = ri <= ci

    for di in range(2):
        b_i = bias_ref[0, di, hd]
        b_f = bias_ref[1, di, hd]
        irow = (grow_ref[di] + b_i) * LOG2E
        lf = _log_sigmoid(grow_ref[2 + di] + b_f) * LOG2E
        tri = (upper if di == 0 else lower).astype(BF16)
        brow = sum(jnp.dot(part, tri, preferred_element_type=F32) for part in _split2(lf))
        g = jnp.sum(lf, axis=1, keepdims=True)
        logw = g - brow + irow
        gb = jnp.broadcast_to(g, (nc, dh))
        mxb = jnp.broadcast_to(jnp.max(logw, axis=1, keepdims=True), (nc, dh))
        m = jnp.zeros((1, dh), F32)
        mprev, mnew = [None] * nc, [None] * nc
        for step in range(nc):
            c = step if di == 0 else nc - 1 - step
            mprev[c] = m
            m = jnp.maximum(gb[c:c + 1] + m, mxb[c:c + 1])
            mnew[c] = m
        mprev = jnp.concatenate(mprev, axis=0)
        mnew = jnp.concatenate(mnew, axis=0)
        lf_scr[di] = lf
        b_scr[di] = brow
        i_scr[di] = irow
        w_scr[di] = jnp.exp2(logw - mnew)
        mp_scr[di] = mprev
        dc_scr[di] = jnp.exp2(gb + mprev - mnew)

    ones_rhs = jnp.ones((2 * L, dh), BF16)
    chunk = lambda c: slice(c * L, (c + 1) * L)

    for c in range(nc):
        for di in range(2):
            lfr, brow, irow, mprev = (ref[di, c:c + 1, :] for ref in (lf_scr, b_scr, i_scr, mp_scr))
            causal = lower if di == 0 else upper
            hi, lo = _split2(jnp.where(causal, lfr, 0.0))
            bcol = jnp.dot(jnp.concatenate([hi, lo], axis=1), ones_rhs, preferred_element_type=F32)
            dmat = jnp.where(causal, bcol - brow + irow, _NEG)
            m_inter = bcol + mprev
            m_t = jnp.maximum(m_inter, jnp.max(dmat, axis=1, keepdims=True))
            e_scr[di, c] = jnp.exp2(dmat - m_t)
            a_scr[di, c] = jnp.exp2(m_inter - m_t)
            f_scr[di, c] = jnp.exp2(-m_t)

    for c in range(nc):
        qk = jnp.dot(q_scr[chunk(c), :], kt_scr[c], preferred_element_type=F32)
        for di in range(2):
            p_scr[di, c] = (qk * e_scr[di, c]).astype(BF16)

    states = [jnp.zeros((dh, 2 * dh), F32)] * 2
    for step in range(nc):
        for di in range(2):
            c = step if di == 0 else nc - 1 - step
            va = va_scr[chunk(c), :]
            nd_scr[di, c] = jnp.dot(p_scr[di, c], va, preferred_element_type=F32)
            cp_scr[di, c] = states[di].astype(BF16)
            ktw = (kt_scr[c].astype(F32) * w_scr[di, c:c + 1, :]).astype(BF16)
            decay = dc_scr[di, c:c + 1, :]
            states[di] = jnp.concatenate([decay, decay], axis=1) * states[di] \
                + jnp.dot(ktw, va, preferred_element_type=F32)

    for c in range(nc):
        qc = q_scr[chunk(c), :]
        hsum = None
        for di in range(2):
            a_in = a_scr[di, c]
            nd = jnp.concatenate([a_in, a_in], axis=1) * jnp.dot(qc, cp_scr[di, c], preferred_element_type=F32) \
                + nd_scr[di, c]
            hval = nd[:, :dh] * (1.0 / jnp.maximum(jnp.abs(nd[:, dh:]), f_scr[di, c]))
            hsum = hval if hsum is None else hsum + hval
        hs_scr[chunk(c), :] = hsum

    hs = hs_scr[...]
    mu = jnp.sum(hs, axis=1, keepdims=True) * (1.0 / dh)
    cen = hs - mu
    var = jnp.sum(cen * cen, axis=1, keepdims=True) * (1.0 / dh)
    hn = cen * lax.rsqrt(var + EPS)
    out_ref[...] = (hn * gn_ref[...] * og_ref[...]).astype(out_ref.dtype)


def _mlstm(gate_bias, uc, vm, og, grow, wq, wk, gn):
    B, S, MW = uc.shape
    H, dh = MLSTM_HEADS, MLSTM_HEAD_DIM
    L = MLSTM_CHUNK
    assert L == dh == LANES
    nc = S // L
    tok = pl.BlockSpec((None, S, dh), lambda b, h: (b, 0, h))
    return pl.pallas_call(
        _mlstm_kernel,
        grid=(B, H),
        in_specs=[pl.BlockSpec(memory_space=pltpu.SMEM), tok, tok, tok,
                  pl.BlockSpec((None, None, 4, nc, L), lambda b, h: (b, h, 0, 0, 0)),
                  pl.BlockSpec((None, dh, dh), lambda b, h: (h, 0, 0)),
                  pl.BlockSpec((None, dh, dh), lambda b, h: (h, 0, 0)),
                  pl.BlockSpec((1, dh), lambda b, h: (0, h))],
        out_specs=tok,
        out_shape=jax.ShapeDtypeStruct((B, S, MW), BF16),
        scratch_shapes=[pltpu.VMEM((S, dh), BF16), pltpu.VMEM((nc, dh, L), BF16), pltpu.VMEM((S, 2 * dh), BF16),
                        pltpu.VMEM((S, dh), F32)] + [pltpu.VMEM((2, nc, L), F32)] * 6
        + [pltpu.VMEM((2, nc, L, L), F32), pltpu.VMEM((2, nc, L, dh), F32), pltpu.VMEM((2, nc, L, dh), F32),
           pltpu.VMEM((2, nc, L, L), BF16), pltpu.VMEM((2, nc, L, 2 * dh), F32),
           pltpu.VMEM((2, nc, dh, 2 * dh), BF16)],
        compiler_params=pltpu.CompilerParams(dimension_semantics=("arbitrary", "arbitrary"),
                                             vmem_limit_bytes=VMEM_LIMIT),
        name="mlstm",
    )(gate_bias, uc, vm, og, grow, wq, wk, gn)


FFN_CHUNK = 256


def _ffn_kernel(x_ref, xp_ref, xn_ref, a_ref, ap_ref, an_ref, m_ref, mp_ref, mn_ref, p_ref,
                wo_ref, g2_ref, wg_ref, wu_ref, wd_ref, cw_ref, cb_ref, gp_ref, wpg_ref, bpg_ref, wpp_ref, gf_ref,
                out_ref, mix_ref, hcat_ref, act_ref, *, tiles_per_seq):
    tm, aw = a_ref.shape
    ck = FFN_CHUNK
    nck = wg_ref.shape[1] // ck
    i = pl.program_id(0)
    first = (i % tiles_per_seq) == 0
    last = (i % tiles_per_seq) == tiles_per_seq - 1

    for lo, aa, mm in ((0, ap_ref, mp_ref), (HALO, a_ref, m_ref), (HALO + tm, an_ref, mn_ref)):
        mix_ref[lo:lo + aa.shape[0], :aw] = aa[...]
        mix_ref[lo:lo + aa.shape[0], aw:] = mm[...]
    y = jnp.dot(mix_ref[...], wo_ref[...], preferred_element_type=F32)
    x1 = x_ref[...] + y[HALO:HALO + tm]
    out_ref[...] = x1
    g2 = g2_ref[...]
    hcat_ref[HALO:HALO + tm, :] = _rms(x1, g2).astype(BF16)
    hp = _rms(xp_ref[...] + y[:HALO], g2).astype(BF16)
    hn = _rms(xn_ref[...] + y[HALO + tm:], g2).astype(BF16)
    hcat_ref[0:HALO, :] = jnp.where(first, jnp.zeros_like(hp), hp)
    hcat_ref[HALO + tm:, :] = jnp.where(last, jnp.zeros_like(hn), hn)

    rows = tm + 2 * HALO
    for c in range(nck):
        cols = slice(c * ck, (c + 1) * ck)
        g = jnp.dot(hcat_ref[...], wg_ref[:, cols], preferred_element_type=F32)
        up = jnp.dot(hcat_ref[HALO:HALO + tm, :], wu_ref[:, cols], preferred_element_type=F32)
        cw = cw_ref[:, cols]
        a = pltpu.roll(g, 1, 0)[HALO:HALO + tm] * cw[0:1] + g[HALO:HALO + tm] * cw[1:2] \
            + pltpu.roll(g, rows - 1, 0)[HALO:HALO + tm] * cw[2:3] + cb_ref[:, cols]
        act_ref[:, cols] = (0.5 * a * (1.0 + lax.erf(a * (2.0 ** -0.5))) * up).astype(BF16)
    x2 = out_ref[...] + jnp.dot(act_ref[...], wd_ref[...], preferred_element_type=F32)
    ms = jnp.sum(x2 * x2, axis=-1, keepdims=True) * (1.0 / x2.shape[-1])
    zg = jnp.dot((x2 * gp_ref[...]).astype(BF16), wpg_ref[...], preferred_element_type=F32)
    gate = jax.nn.sigmoid(zg * lax.rsqrt(ms + EPS) + bpg_ref[...])
    pp = jnp.dot(p_ref[...].astype(BF16), wpp_ref[...], preferred_element_type=F32)
    out_ref[...] = _rms(x2 + pp * gate, gf_ref[...])


def _ffn(x2d, attn, mem, p2, wo, g2, wg, wu, wd, cw, cb, gp, wpg, bpg, wpp, gf, tm, S):
    T, D = x2d.shape
    row = lambda w: pl.BlockSpec((tm, w), lambda i: (i, 0))
    hb = tm // HALO
    nhb = T // HALO
    prev = lambda w: pl.BlockSpec((HALO, w), lambda i: (jnp.maximum(i * hb - 1, 0), 0))
    nxt = lambda w: pl.BlockSpec((HALO, w), lambda i: (jnp.minimum((i + 1) * hb, nhb - 1), 0))
    aw, mw = attn.shape[1], mem.shape[1]
    consts = (wo, g2, wg, wu, wd, cw, cb, gp, wpg, bpg, wpp, gf)
    return pl.pallas_call(
        functools.partial(_ffn_kernel, tiles_per_seq=S // tm),
        grid=(T // tm,),
        in_specs=[row(D), prev(D), nxt(D), row(aw), prev(aw), nxt(aw), row(mw), prev(mw), nxt(mw),
                  row(p2.shape[1])] + [_const_spec(c.shape) for c in consts],
        out_specs=row(D),
        out_shape=jax.ShapeDtypeStruct((T, D), F32),
        scratch_shapes=[pltpu.VMEM((tm + 2 * HALO, aw + mw), BF16), pltpu.VMEM((tm + 2 * HALO, D), BF16),
                        pltpu.VMEM((tm, wg.shape[1]), BF16)],
        compiler_params=pltpu.CompilerParams(dimension_semantics=("arbitrary",), vmem_limit_bytes=VMEM_LIMIT),
        name="outproj_ffn_ple",
    )(x2d, x2d, x2d, attn, attn, attn, mem, mem, mem, p2, *consts)


def kernel(x, p, positions, ln_mix_g, w_in, mlstm_conv_w, mlstm_conv_b, w_mq, w_mk, b_igate, b_fgate,
           mlstm_gn_g, w_out, ln_ffn_g, w_ffn_gate, ffn_conv_w, ffn_conv_b, w_ffn_up, w_ffn_down,
           ln_ple_g, w_ple_gate, b_ple_gate, w_ple_proj, ln_final_g):
    B, S, D = x.shape
    depth = w_in.shape[0]
    T = B * S
    H, dh = MLSTM_HEADS, MLSTM_HEAD_DIM
    aw = D // 2
    mw = D - aw
    tm = ROW_TILE
    tm_in = INPROJ_TILE
    L = MLSTM_CHUNK
    nc = S // L

    bias = _attn_bias()
    pos3 = positions.reshape(T // tm_in, 1, tm_in)

    assert depth == 1, "the final norm is fused into the layer's last kernel"
    xf = x.reshape(T, D)
    for i in range(depth):
        nproj = 3 * aw + 3 * mw
        ng = w_in.shape[-1] - nproj
        w_pad = jnp.pad(w_in[i], ((0, 0), (0, LANES - ng))).astype(BF16)
        q, k, v, uc, vm, og, gates_t = _inproj(xf, pos3, ln_mix_g[i][None], w_pad, mlstm_conv_w[i],
                                               mlstm_conv_b[i][None], ng, aw, mw, tm_in, S)

        attn = _attn(q, k, v, bias)

        grow = gates_t.reshape(4, H, B, nc, L).transpose(2, 1, 0, 3, 4)
        gate_bias = jnp.stack([b_igate[i], b_fgate[i]]).astype(F32)
        mem = _mlstm(gate_bias, uc.reshape(B, S, mw), vm.reshape(B, S, mw), og.reshape(B, S, mw), grow,
                     w_mq[i].astype(BF16), w_mk[i].astype(BF16), mlstm_gn_g[i][None])

        assert w_ffn_gate.shape[-1] % FFN_CHUNK == 0
        xf = _ffn(xf, attn.reshape(T, aw), mem.reshape(T, mw), p[i].reshape(T, -1),
                  w_out[i].astype(BF16), ln_ffn_g[i][None],
                  w_ffn_gate[i].astype(BF16), w_ffn_up[i].astype(BF16), w_ffn_down[i].astype(BF16),
                  ffn_conv_w[i], ffn_conv_b[i][None], ln_ple_g[i][None],
                  w_ple_gate[i].astype(BF16), b_ple_gate[i][None], w_ple_proj[i].astype(BF16), ln_final_g[None], tm, S)
    return xf.reshape(B, S, D)
```

```python
import functools
import math

import jax
import jax.numpy as jnp
import numpy as np
from jax import lax
from jax.experimental import pallas as pl
from jax.experimental.pallas import tpu as pltpu

F32 = jnp.float32
BF16 = jnp.bfloat16

EPS = 1e-6
ATTN_HEAD_DIM = 64
ATTN_PATTERNS = ((128, 1), (512, 4), (2048, 16))
ATTN_RADIUS = 64
ROPE_THETA = 500000.0
ROT_DIM = ATTN_HEAD_DIM // 4
MLSTM_HEADS = 4
MLSTM_HEAD_DIM = 128
LANES = 128
ATT_BLK = 128
ATT_REGION = 8
ATT_PAIRS_PER_STEP = 4
MLSTM_CHUNK = 128
VMEM_LIMIT = 56 * 1024 * 1024
ROW_TILE = 1024
INPROJ_TILE = 1024
HALO = 16

_NEG = float("-inf")
LOG2E = math.log2(math.e)


def _rms(x, g):
    ms = jnp.sum(x * x, axis=-1, keepdims=True) * (1.0 / x.shape[-1])
    return x * lax.rsqrt(ms + EPS) * g


def _const_spec(shape):
    nd = len(shape)
    return pl.BlockSpec(shape, lambda *_: (0,) * nd, pipeline_mode=pl.Buffered(1))


def _inproj_kernel(x_ref, xp_ref, xn_ref, pos_ref, g_ref, invc_ref, sel_ref, unrot_ref, w_ref, cw_ref, cb_ref,
                   q_ref, k_ref, v_ref, uc_ref, vm_ref, og_ref, gate_ref, hcat_ref, *, tiles_per_seq):
    tm = x_ref.shape[0]
    aw3 = 3 * q_ref.shape[0] * LANES
    mw = uc_ref.shape[1]
    i = pl.program_id(0)
    first = (i % tiles_per_seq) == 0
    last = (i % tiles_per_seq) == tiles_per_seq - 1
    ang = invc_ref[...] * pos_ref[...].astype(F32)
    half_rows = lax.broadcasted_iota(jnp.int32, ang.shape, 0) < ROT_DIM // 2
    tab = jnp.where(half_rows, jnp.cos(ang), jnp.sin(ang))
    cs = sum(lax.dot_general(part, sel_ref[...], (((0,), (0,)), ((), ())), preferred_element_type=F32)
             for part in _split2(tab))
    cos = cs[:, :LANES] + unrot_ref[...]
    sin = cs[:, LANES:]
    def prep(x):
        scale = lax.rsqrt(jnp.sum(x * x, axis=-1, keepdims=True) * (1.0 / x.shape[-1]) + EPS)
        return (x * g_ref[...]).astype(BF16), scale

    h, rs = prep(x_ref[...])
    hp, rsp = prep(xp_ref[...])
    hn, rsn = prep(xn_ref[...])
    hcat_ref[0:HALO, :] = hp
    hcat_ref[HALO:HALO + tm, :] = h
    hcat_ref[HALO + tm:, :] = hn
    lane = lax.broadcasted_iota(jnp.int32, (1, LANES), 1)
    first_half = (lane % ATTN_HEAD_DIM) < (ROT_DIM // 2)

    def rotary(z):
        up = pltpu.roll(z, LANES - ROT_DIM // 2, 1)
        dn = pltpu.roll(z, ROT_DIM // 2, 1)
        return z * cos + jnp.where(first_half, up, dn) * sin

    aw = aw3 // 3
    npair = aw // LANES

    def proj(c0, width):
        return jnp.dot(h, w_ref[:, c0:c0 + width], preferred_element_type=F32) * rs

    zqkv = proj(0, aw3)
    q_scale = ATTN_HEAD_DIM ** -0.5 * LOG2E
    for j in range(npair):
        sl = slice(j * LANES, (j + 1) * LANES)
        q_ref[j] = rotary(zqkv[:, sl]) * q_scale
        k_ref[j] = rotary(zqkv[:, aw + j * LANES:aw + (j + 1) * LANES])
        v_ref[j] = zqkv[:, 2 * aw + j * LANES:2 * aw + (j + 1) * LANES]
    rows = tm + 2 * HALO
    ridx = lax.broadcasted_iota(jnp.int32, (rows, 1), 0)
    outside = ((ridx < HALO) & first) | ((ridx >= HALO + tm) & last)
    zu = jnp.dot(hcat_ref[...], w_ref[:, aw3:aw3 + mw], preferred_element_type=F32) \
        * jnp.concatenate([rsp, rs, rsn], axis=0)
    zu = jnp.where(outside, 0.0, zu)
    cw = cw_ref[...]
    a = pltpu.roll(zu, 1, 0)[HALO:HALO + tm] * cw[0:1] + zu[HALO:HALO + tm] * cw[1:2] \
        + pltpu.roll(zu, rows - 1, 0)[HALO:HALO + tm] * cw[2:3] + cb_ref[...]
    uc_ref[...] = (a * jax.nn.sigmoid(a)).astype(BF16)
    zvo = proj(aw3 + mw, 2 * mw)
    vm_ref[...] = zvo[:, :mw].astype(BF16)
    og_ref[...] = jax.nn.sigmoid(zvo[:, mw:])
    gates = proj(aw3 + 3 * mw, LANES)
    gate_ref[...] = gates.T[:gate_ref.shape[0]]


def _rotary_tables():
    half = ROT_DIM // 2
    inv = jnp.power(ROPE_THETA, -jnp.arange(0, ROT_DIM, 2, dtype=F32) / ROT_DIM)
    invc = jnp.concatenate([inv, inv])[:, None]
    sel = np.zeros((2 * half, 2 * LANES), np.float32)
    unrot = np.ones((1, LANES), np.float32)
    for l in range(LANES):
        hl = l % ATTN_HEAD_DIM
        if hl < ROT_DIM:
            sel[hl % half, l] = 1.0
            sel[half + hl % half, LANES + l] = -1.0 if hl < half else 1.0
            unrot[0, l] = 0.0
    return invc, jnp.asarray(sel, BF16), jnp.asarray(unrot)


def _inproj(x2, pos3, g, w, cw, cb, ng, aw, mw, tm, S):
    T, D = x2.shape
    assert w.shape[1] == 3 * aw + 3 * mw + LANES
    row = lambda n: pl.BlockSpec((tm, n), lambda i: (i, 0))
    hb = tm // HALO
    nhb = T // HALO
    prev = pl.BlockSpec((HALO, D), lambda i: (jnp.maximum(i * hb - 1, 0), 0))
    nxt = pl.BlockSpec((HALO, D), lambda i: (jnp.minimum((i + 1) * hb, nhb - 1), 0))
    tps = S // tm
    npair = aw // LANES
    pair_major = pl.BlockSpec((None, npair, tm, LANES), lambda i: (i // tps, 0, i % tps, 0))
    invc, sel, unrot = _rotary_tables()
    consts = (g, invc, sel, unrot, w, cw, cb)
    return pl.pallas_call(
        functools.partial(_inproj_kernel, tiles_per_seq=tps),
        grid=(T // tm,),
        in_specs=[row(D), prev, nxt, pl.BlockSpec((None, 1, tm), lambda i: (i, 0, 0))]
        + [_const_spec(c.shape) for c in consts],
        out_specs=[pair_major, pair_major, pair_major, row(mw), row(mw), row(mw),
                   pl.BlockSpec((ng, tm), lambda i: (0, i))],
        out_shape=[jax.ShapeDtypeStruct((T // S, npair, S, LANES), F32)] * 3
        + [jax.ShapeDtypeStruct((T, mw), BF16), jax.ShapeDtypeStruct((T, mw), BF16),
           jax.ShapeDtypeStruct((T, mw), F32), jax.ShapeDtypeStruct((ng, T), F32)],
        scratch_shapes=[pltpu.VMEM((tm + 2 * HALO, D), BF16)],
        compiler_params=pltpu.CompilerParams(dimension_semantics=("arbitrary",), vmem_limit_bytes=VMEM_LIMIT),
        name="inproj",
    )(x2, x2, x2, pos3, *consts)


def _attn_bias():
    W, R, Q = 2 * ATT_BLK, ATTN_RADIUS, ATT_BLK
    kj = np.arange(W)[:, None]
    qi = np.arange(Q)[None, :]
    interior = (kj - qi >= 0) & (kj - qi <= 2 * R)
    near = np.abs(kj - qi) <= R
    edge = ((qi < Q // 2) & (kj < Q) & near) | ((qi >= Q // 2) & (kj >= Q) & (np.abs(kj - Q - qi) <= R))
    single = near & (kj < Q)
    return jnp.asarray(np.where(np.stack([interior, edge, single]), 0.0, _NEG).astype(np.float32), BF16)


def _attn_kernel(q_ref, k_ref, v_ref, bias_ref, out_ref, outf_scr, *scratch):
    def body(p, carry):
        _attn_pair(q_ref.at[p], k_ref.at[p], v_ref.at[p], bias_ref, outf_scr.at[p], *scratch)
        return carry

    lax.fori_loop(0, q_ref.shape[0], body, 0)
    for p in range(q_ref.shape[0]):
        out_ref[:, p * LANES:(p + 1) * LANES] = outf_scr[p].astype(out_ref.dtype)


def _attn_pair(q_ref, k_ref, v_ref, bias_ref, outf_scr, qc_ref, kc_ref, vt0_ref, vt1_ref, o_scr, l_scr, f4_scr):
    S = q_ref.shape[0]
    Q, HD = ATT_BLK, ATTN_HEAD_DIM
    lane = lax.broadcasted_iota(jnp.int32, (1, LANES), 1)
    head0 = lane < HD
    npat = len(ATTN_PATTERNS)
    vt0_ref[:, HD:, :] = jnp.ones((npat, HD, S), BF16)
    vt1_ref[:, :HD, :] = jnp.ones((npat, HD, S), BF16)

    def cat(parts, axis):
        return parts[0] if len(parts) == 1 else jnp.concatenate(parts, axis=axis)

    dm = ATTN_PATTERNS[1][1]
    assert [d for _, d in ATTN_PATTERNS] == [1, dm, dm * dm]
    Mm = S // dm

    def put(pi, base, xq, xk, xv):
        n = xq.shape[0]
        qc_ref[pi, base:base + n, :] = xq.astype(BF16)
        kc_ref[pi, base:base + n, :] = xk.astype(BF16)
        vt = xv.T.astype(BF16)
        vt0_ref[pi, :HD, base:base + n] = vt[:HD]
        vt1_ref[pi, HD:, base:base + n] = vt[HD:]

    put(0, 0, q_ref[...], k_ref[...], v_ref[...])
    for r in range(dm):
        xs = [src[pl.ds(r, Mm, stride=dm), :] for src in (q_ref, k_ref, v_ref)]
        for a, x in enumerate(xs):
            f4_scr[a, r * Mm:(r + 1) * Mm, :] = x
        put(1, r * Mm, *xs)
    for r in range(dm * dm):
        put(2, r * (Mm // dm), *[f4_scr[a, pl.ds((r % dm) * Mm + r // dm, Mm // dm, stride=dm), :] for a in range(3)])

    def out_rows(pi, r, start, size):
        if pi == 0:
            return pl.ds(start, size)
        if pi == 1:
            return pl.ds(r * Mm + start, size)
        return pl.ds((r % dm) * Mm + r // dm + dm * start, size, stride=dm)

    blocks = []
    for pi, (_, d) in enumerate(ATTN_PATTERNS):
        M = S // d
        nb = M // Q
        for r in range(d):
            base = r * M
            if nb == 1:
                blocks.append(dict(pi=pi, q=[slice(base, base + Q)], k=[slice(base, base + Q)], bias=2, W=Q,
                                   out=[(out_rows(pi, r, 0, Q), slice(0, Q))]))
                continue
            for i in range(nb - 1):
                k0 = base + i * Q
                blocks.append(dict(pi=pi, q=[slice(k0 + Q // 2, k0 + Q // 2 + Q)], k=[slice(k0, k0 + 2 * Q)],
                                   bias=0, W=2 * Q, out=[(out_rows(pi, r, i * Q + Q // 2, Q), slice(0, Q))]))
            blocks.append(dict(pi=pi, q=[slice(base, base + Q // 2), slice(base + M - Q // 2, base + M)],
                               k=[slice(base, base + Q), slice(base + M - Q, base + M)], bias=1, W=2 * Q,
                               out=[(out_rows(pi, r, 0, Q // 2), slice(0, Q // 2)),
                                    (out_rows(pi, r, M - Q // 2, Q // 2), slice(Q // 2, Q))]))

    def region_scores(region):
        scores = []
        for b in region:
            qb = cat([qc_ref[b["pi"], s, :] for s in b["q"]], 0)
            kw = cat([kc_ref[b["pi"], s, :] for s in b["k"]], 0)
            bias = bias_ref[b["bias"]][:b["W"]]
            for hh in range(2):
                qm = jnp.where(head0 if hh == 0 else jnp.logical_not(head0), qb, jnp.zeros_like(qb))
                s = lax.dot_general(kw, qm, (((1,), (1,)), ((), ())), preferred_element_type=F32)
                scores.append(s.astype(BF16) + bias)
        return scores

    def region_finish(region, scores):
        maxes = [jnp.max(s, axis=0, keepdims=True) for s in scores]
        probs = [jnp.exp2(s - m) for s, m in zip(scores, maxes)]
        maxes = [m.astype(F32) for m in maxes]
        accs = []
        for bi, b in enumerate(region):
            for hh, vref in enumerate((vt0_ref, vt1_ref)):
                vw = cat([vref[b["pi"], :, s] for s in b["k"]], 1)
                accs.append(jnp.dot(vw, probs[2 * bi + hh], preferred_element_type=F32))
        for bi, b in enumerate(region):
            o_t, l_t = [], []
            for hh in range(2):
                a = accs[2 * bi + hh]
                l = a[HD:HD + 1] if hh == 0 else a[0:1]
                o_t.append((a[:HD] if hh == 0 else a[HD:]) * (1.0 / l))
                l_t.append(jnp.broadcast_to(maxes[2 * bi + hh] + jnp.log2(l), (HD, Q)))
            o = jnp.concatenate(o_t, axis=0).T
            lse = jnp.concatenate(l_t, axis=0).T
            for rows, sl in b["out"]:
                o_scr[b["pi"], rows, :] = o[sl]
                l_scr[b["pi"], rows, :] = lse[sl]

    for g in range(0, len(blocks), ATT_REGION):
        region = blocks[g:g + ATT_REGION]
        region_finish(region, region_scores(region))

    for r in range(dm):
        tok = pl.ds(r, Mm, stride=dm)
        cls = slice(r * Mm, (r + 1) * Mm)
        os = (o_scr[0, tok, :], o_scr[1, cls, :], o_scr[2, cls, :])
        ls = (l_scr[0, tok, :], l_scr[1, cls, :], l_scr[2, cls, :])
        mx = jnp.maximum(jnp.maximum(ls[0], ls[1]), ls[2])
        ws = [jnp.exp2(l - mx) for l in ls]
        outf_scr[tok, :] = (ws[0] * os[0] + ws[1] * os[1] + ws[2] * os[2]) * (1.0 / (ws[0] + ws[1] + ws[2]))


def _attn(q, k, v, bias):
    B, npair, S, _ = q.shape
    pp = ATT_PAIRS_PER_STEP
    blk = pl.BlockSpec((None, pp, S, LANES), lambda b, j: (b, j, 0, 0))
    npat = len(ATTN_PATTERNS)
    return pl.pallas_call(
        _attn_kernel,
        grid=(B, npair // pp),
        in_specs=[blk, blk, blk, _const_spec(bias.shape)],
        out_specs=pl.BlockSpec((None, S, pp * LANES), lambda b, j: (b, 0, j)),
        out_shape=jax.ShapeDtypeStruct((B, S, npair * LANES), BF16),
        scratch_shapes=[pltpu.VMEM((pp, S, LANES), F32),
                        pltpu.VMEM((npat, S, LANES), BF16), pltpu.VMEM((npat, S, LANES), BF16),
                        pltpu.VMEM((npat, LANES, S), BF16), pltpu.VMEM((npat, LANES, S), BF16),
                        pltpu.VMEM((npat, S, LANES), F32), pltpu.VMEM((npat, S, LANES), F32),
                        pltpu.VMEM((3, S, LANES), F32)],
        compiler_params=pltpu.CompilerParams(dimension_semantics=("arbitrary", "arbitrary"),
                                             vmem_limit_bytes=VMEM_LIMIT),
        name="dilated_attn",
    )(q, k, v, bias)


def _split2(x):
    hi = x.astype(BF16)
    return hi, (x - hi.astype(F32)).astype(BF16)


def _log_sigmoid(x):
    return jnp.minimum(x, 0.0) - jnp.log1p(jnp.exp(-jnp.abs(x)))


def _mlstm_kernel(bias_ref, uc_ref, vm_ref, og_ref, grow_ref, wq_ref, wk_ref, gn_ref,
                  out_ref, q_scr, kt_scr, va_scr, hs_scr, lf_scr, b_scr, i_scr, w_scr, mp_scr, dc_scr,
                  e_scr, a_scr, f_scr, p_scr, nd_scr, cp_scr):
    S, dh = uc_ref.shape
    L = MLSTM_CHUNK
    nc = S // L
    hd = pl.program_id(1)

    uc = uc_ref[...]
    q_scr[...] = jnp.dot(uc, wq_ref[...], preferred_element_type=F32).astype(BF16)
    k = jnp.dot(uc, wk_ref[...], preferred_element_type=F32) * (dh ** -0.5)
    for c in range(nc):
        kt_scr[c] = k[c * L:(c + 1) * L].T.astype(BF16)
    va_scr[:, :dh] = vm_ref[...]
    va_scr[:, dh:] = jnp.ones((S, dh), BF16)

    ri = lax.broadcasted_iota(jnp.int32, (L, L), 0)
    ci = lax.broadcasted_iota(jnp.int32, (L, L), 1)
    lower = ri >= ci
    upper = ri <= ci

    for di in range(2):
        b_i = bias_ref[0, di, hd]
        b_f = bias_ref[1, di, hd]
        irow = (grow_ref[di] + b_i) * LOG2E
        lf = _log_sigmoid(grow_ref[2 + di] + b_f) * LOG2E
        tri = (upper if di == 0 else lower).astype(BF16)
        brow = sum(jnp.dot(part, tri, preferred_element_type=F32) for part in _split2(lf))
        g = jnp.sum(lf, axis=1, keepdims=True)
        logw = g - brow + irow
        gb = jnp.broadcast_to(g, (nc, dh))
        mxb = jnp.broadcast_to(jnp.max(logw, axis=1, keepdims=True), (nc, dh))
        m = jnp.zeros((1, dh), F32)
        mprev, mnew = [None] * nc, [None] * nc
        for step in range(nc):
            c = step if di == 0 else nc - 1 - step
            mprev[c] = m
            m = jnp.maximum(gb[c:c + 1] + m, mxb[c:c + 1])
            mnew[c] = m
        mprev = jnp.concatenate(mprev, axis=0)
        mnew = jnp.concatenate(mnew, axis=0)
        lf_scr[di] = lf
        b_scr[di] = brow
        i_scr[di] = irow
        w_scr[di] = jnp.exp2(logw - mnew)
        mp_scr[di] = mprev
        dc_scr[di] = jnp.exp2(gb + mprev - mnew)

    ones_rhs = jnp.ones((2 * L, dh), BF16)
    chunk = lambda c: slice(c * L, (c + 1) * L)

    for c in range(nc):
        for di in range(2):
            lfr, brow, irow, mprev = (ref[di, c:c + 1, :] for ref in (lf_scr, b_scr, i_scr, mp_scr))
            causal = lower if di == 0 else upper
            hi, lo = _split2(jnp.where(causal, lfr, 0.0))
            bcol = jnp.dot(jnp.concatenate([hi, lo], axis=1), ones_rhs, preferred_element_type=F32)
            dmat = jnp.where(causal, bcol - brow + irow, _NEG)
            m_inter = bcol + mprev
            m_t = jnp.maximum(m_inter, jnp.max(dmat, axis=1, keepdims=True))
            e_scr[di, c] = jnp.exp2(dmat - m_t)
            a_scr[di, c] = jnp.exp2(m_inter - m_t)
            f_scr[di, c] = jnp.exp2(-m_t)

    for c in range(nc):
        qk = jnp.dot(q_scr[chunk(c), :], kt_scr[c], preferred_element_type=F32)
        for di in range(2):
            p_scr[di, c] = (qk * e_scr[di, c]).astype(BF16)

    states = [jnp.zeros((dh, 2 * dh), F32)] * 2
    for step in range(nc):
        for di in range(2):
            c = step if di == 0 else nc - 1 - step
            va = va_scr[chunk(c), :]
            nd_scr[di, c] = jnp.dot(p_scr[di, c], va, preferred_element_type=F32)
            cp_scr[di, c] = states[di].astype(BF16)
            ktw = (kt_scr[c].astype(F32) * w_scr[di, c:c + 1, :]).astype(BF16)
            decay = dc_scr[di, c:c + 1, :]
            states[di] = jnp.concatenate([decay, decay], axis=1) * states[di] \
                + jnp.dot(ktw, va, preferred_element_type=F32)

    for c in range(nc):
        qc = q_scr[chunk(c), :]
        hsum = None
        for di in range(2):
            a_in = a_scr[di, c]
            nd = jnp.concatenate([a_in, a_in], axis=1) * jnp.dot(qc, cp_scr[di, c], preferred_element_type=F32) \
                + nd_scr[di, c]
            hval = nd[:, :dh] * (1.0 / jnp.maximum(jnp.abs(nd[:, dh:]), f_scr[di, c]))
            hsum = hval if hsum is None else hsum + hval
        hs_scr[chunk(c), :] = hsum

    hs = hs_scr[...]
    mu = jnp.sum(hs, axis=1, keepdims=True) * (1.0 / dh)
    cen = hs - mu
    var = jnp.sum(cen * cen, axis=1, keepdims=True) * (1.0 / dh)
    hn = cen * lax.rsqrt(var + EPS)
    out_ref[...] = (hn * gn_ref[...] * og_ref[...]).astype(out_ref.dtype)


def _mlstm(gate_bias, uc, vm, og, grow, wq, wk, gn):
    B, S, MW = uc.shape
    H, dh = MLSTM_HEADS, MLSTM_HEAD_DIM
    L = MLSTM_CHUNK
    assert L == dh == LANES
    nc = S // L
    tok = pl.BlockSpec((None, S, dh), lambda b, h: (b, 0, h))
    return pl.pallas_call(
        _mlstm_kernel,
        grid=(B, H),
        in_specs=[pl.BlockSpec(memory_space=pltpu.SMEM), tok, tok, tok,
                  pl.BlockSpec((None, None, 4, nc, L), lambda b, h: (b, h, 0, 0, 0)),
                  pl.BlockSpec((None, dh, dh), lambda b, h: (h, 0, 0)),
                  pl.BlockSpec((None, dh, dh), lambda b, h: (h, 0, 0)),
                  pl.BlockSpec((1, dh), lambda b, h: (0, h))],
        out_specs=tok,
        out_shape=jax.ShapeDtypeStruct((B, S, MW), BF16),
        scratch_shapes=[pltpu.VMEM((S, dh), BF16), pltpu.VMEM((nc, dh, L), BF16), pltpu.VMEM((S, 2 * dh), BF16),
                        pltpu.VMEM((S, dh), F32)] + [pltpu.VMEM((2, nc, L), F32)] * 6
        + [pltpu.VMEM((2, nc, L, L), F32), pltpu.VMEM((2, nc, L, dh), F32), pltpu.VMEM((2, nc, L, dh), F32),
           pltpu.VMEM((2, nc, L, L), BF16), pltpu.VMEM((2, nc, L, 2 * dh), F32),
           pltpu.VMEM((2, nc, dh, 2 * dh), BF16)],
        compiler_params=pltpu.CompilerParams(dimension_semantics=("arbitrary", "arbitrary"),
                                             vmem_limit_bytes=VMEM_LIMIT),
        name="mlstm",
    )(gate_bias, uc, vm, og, grow, wq, wk, gn)


FFN_CHUNK = 256


def _ffn_kernel(x_ref, xp_ref, xn_ref, a_ref, ap_ref, an_ref, m_ref, mp_ref, mn_ref, p_ref,
                wo_ref, g2_ref, wg_ref, wu_ref, wd_ref, cw_ref, cb_ref, gp_ref, wpg_ref, bpg_ref, wpp_ref, gf_ref,
                out_ref, mix_ref, hcat_ref, act_ref, *, tiles_per_seq):
    tm, aw = a_ref.shape
    ck = FFN_CHUNK
    nck = wg_ref.shape[1] // ck
    i = pl.program_id(0)
    first = (i % tiles_per_seq) == 0
    last = (i % tiles_per_seq) == tiles_per_seq - 1

    for lo, aa, mm in ((0, ap_ref, mp_ref), (HALO, a_ref, m_ref), (HALO + tm, an_ref, mn_ref)):
        mix_ref[lo:lo + aa.shape[0], :aw] = aa[...]
        mix_ref[lo:lo + aa.shape[0], aw:] = mm[...]
    y = jnp.dot(mix_ref[...], wo_ref[...], preferred_element_type=F32)
    x1 = x_ref[...] + y[HALO:HALO + tm]
    out_ref[...] = x1
    g2 = g2_ref[...]
    hcat_ref[HALO:HALO + tm, :] = _rms(x1, g2).astype(BF16)
    hp = _rms(xp_ref[...] + y[:HALO], g2).astype(BF16)
    hn = _rms(xn_ref[...] + y[HALO + tm:], g2).astype(BF16)
    hcat_ref[0:HALO, :] = jnp.where(first, jnp.zeros_like(hp), hp)
    hcat_ref[HALO + tm:, :] = jnp.where(last, jnp.zeros_like(hn), hn)

    rows = tm + 2 * HALO
    for c in range(nck):
        cols = slice(c * ck, (c + 1) * ck)
        g = jnp.dot(hcat_ref[...], wg_ref[:, cols], preferred_element_type=F32)
        up = jnp.dot(hcat_ref[HALO:HALO + tm, :], wu_ref[:, cols], preferred_element_type=F32)
        cw = cw_ref[:, cols]
        a = pltpu.roll(g, 1, 0)[HALO:HALO + tm] * cw[0:1] + g[HALO:HALO + tm] * cw[1:2] \
            + pltpu.roll(g, rows - 1, 0)[HALO:HALO + tm] * cw[2:3] + cb_ref[:, cols]
        act_ref[:, cols] = (0.5 * a * (1.0 + lax.erf(a * (2.0 ** -0.5))) * up).astype(BF16)
    x2 = out_ref[...] + jnp.dot(act_ref[...], wd_ref[...], preferred_element_type=F32)
    ms = jnp.sum(x2 * x2, axis=-1, keepdims=True) * (1.0 / x2.shape[-1])
    zg = jnp.dot((x2 * gp_ref[...]).astype(BF16), wpg_ref[...], preferred_element_type=F32)
    gate = jax.nn.sigmoid(zg * lax.rsqrt(ms + EPS) + bpg_ref[...])
    pp = jnp.dot(p_ref[...].astype(BF16), wpp_ref[...], preferred_element_type=F32)
    out_ref[...] = _rms(x2 + pp * gate, gf_ref[...])


def _ffn(x2d, attn, mem, p2, wo, g2, wg, wu, wd, cw, cb, gp, wpg, bpg, wpp, gf, tm, S):
    T, D = x2d.shape
    row = lambda w: pl.BlockSpec((tm, w), lambda i: (i, 0))
    hb = tm // HALO
    nhb = T // HALO
    prev = lambda w: pl.BlockSpec((HALO, w), lambda i: (jnp.maximum(i * hb - 1, 0), 0))
    nxt = lambda w: pl.BlockSpec((HALO, w), lambda i: (jnp.minimum((i + 1) * hb, nhb - 1), 0))
    aw, mw = attn.shape[1], mem.shape[1]
    consts = (wo, g2, wg, wu, wd, cw, cb, gp, wpg, bpg, wpp, gf)
    return pl.pallas_call(
        functools.partial(_ffn_kernel, tiles_per_seq=S // tm),
        grid=(T // tm,),
        in_specs=[row(D), prev(D), nxt(D), row(aw), prev(aw), nxt(aw), row(mw), prev(mw), nxt(mw),
                  row(p2.shape[1])] + [_const_spec(c.shape) for c in consts],
        out_specs=row(D),
        out_shape=jax.ShapeDtypeStruct((T, D), F32),
        scratch_shapes=[pltpu.VMEM((tm + 2 * HALO, aw + mw), BF16), pltpu.VMEM((tm + 2 * HALO, D), BF16),
                        pltpu.VMEM((tm, wg.shape[1]), BF16)],
        compiler_params=pltpu.CompilerParams(dimension_semantics=("arbitrary",), vmem_limit_bytes=VMEM_LIMIT),
        name="outproj_ffn_ple",
    )(x2d, x2d, x2d, attn, attn, attn, mem, mem, mem, p2, *consts)


def kernel(x, p, positions, ln_mix_g, w_in, mlstm_conv_w, mlstm_conv_b, w_mq, w_mk, b_igate, b_fgate,
           mlstm_gn_g, w_out, ln_ffn_g, w_ffn_gate, ffn_conv_w, ffn_conv_b, w_ffn_up, w_ffn_down,
           ln_ple_g, w_ple_gate, b_ple_gate, w_ple_proj, ln_final_g):
    B, S, D = x.shape
    depth = w_in.shape[0]
    T = B * S
    H, dh = MLSTM_HEADS, MLSTM_HEAD_DIM
    aw = D // 2
    mw = D - aw
    tm = ROW_TILE
    tm_in = INPROJ_TILE
    L = MLSTM_CHUNK
    nc = S // L

    bias = _attn_bias()
    pos3 = positions.reshape(T // tm_in, 1, tm_in)

    assert depth == 1, "the final norm is fused into the layer's last kernel"
    xf = x.reshape(T, D)
    for i in range(depth):
        nproj = 3 * aw + 3 * mw
        ng = w_in.shape[-1] - nproj
        w_pad = jnp.pad(w_in[i], ((0, 0), (0, LANES - ng))).astype(BF16)
        q, k, v, uc, vm, og, gates_t = _inproj(xf, pos3, ln_mix_g[i][None], w_pad, mlstm_conv_w[i],
                                               mlstm_conv_b[i][None], ng, aw, mw, tm_in, S)

        attn = _attn(q, k, v, bias)

        grow = gates_t.reshape(4, H, B, nc, L).transpose(2, 1, 0, 3, 4)
        gate_bias = jnp.stack([b_igate[i], b_fgate[i]]).astype(F32)
        mem = _mlstm(gate_bias, uc.reshape(B, S, mw), vm.reshape(B, S, mw), og.reshape(B, S, mw), grow,
                     w_mq[i].astype(BF16), w_mk[i].astype(BF16), mlstm_gn_g[i][None])

        assert w_ffn_gate.shape[-1] % FFN_CHUNK == 0
        xf = _ffn(xf, attn.reshape(T, aw), mem.reshape(T, mw), p[i].reshape(T, -1),
                  w_out[i].astype(BF16), ln_ffn_g[i][None],
                  w_ffn_gate[i].astype(BF16), w_ffn_up[i].astype(BF16), w_ffn_down[i].astype(BF16),
                  ffn_conv_w[i], ffn_conv_b[i][None], ln_ple_g[i][None],
                  w_ple_gate[i].astype(BF16), b_ple_gate[i][None], w_ple_proj[i].astype(BF16), ln_final_g[None], tm, S)
    return xf.reshape(B, S, D)
```

```python
import functools
import math

import jax
import jax.numpy as jnp
import numpy as np
from jax import lax
from jax.experimental import pallas as pl
from jax.experimental.pallas import tpu as pltpu

F32 = jnp.float32
BF16 = jnp.bfloat16

EPS = 1e-6
ATTN_HEAD_DIM = 64
ATTN_PATTERNS = ((128, 1), (512, 4), (2048, 16))
ATTN_RADIUS = 64
ROPE_THETA = 500000.0
ROT_DIM = ATTN_HEAD_DIM // 4
MLSTM_HEADS = 4
MLSTM_HEAD_DIM = 128
LANES = 128
ATT_BLK = 128
ATT_REGION = 8
MLSTM_CHUNK = 128
VMEM_LIMIT = 56 * 1024 * 1024
ROW_TILE = 1024
INPROJ_TILE = 1024
HALO = 16

_NEG = float("-inf")
LOG2E = math.log2(math.e)


def _rms(x, g):
    ms = jnp.sum(x * x, axis=-1, keepdims=True) * (1.0 / x.shape[-1])
    return x * lax.rsqrt(ms + EPS) * g


def _const_spec(shape):
    nd = len(shape)
    return pl.BlockSpec(shape, lambda *_: (0,) * nd, pipeline_mode=pl.Buffered(1))


def _inproj_kernel(x_ref, xp_ref, xn_ref, pos_ref, g_ref, invc_ref, sel_ref, unrot_ref, w_ref, cw_ref, cb_ref,
                   q_ref, k_ref, v_ref, uc_ref, vm_ref, og_ref, gate_ref, hcat_ref, *, tiles_per_seq):
    tm = x_ref.shape[0]
    aw3 = 3 * q_ref.shape[0] * LANES
    mw = uc_ref.shape[1]
    i = pl.program_id(0)
    first = (i % tiles_per_seq) == 0
    last = (i % tiles_per_seq) == tiles_per_seq - 1
    ang = invc_ref[...] * pos_ref[...].astype(F32)
    half_rows = lax.broadcasted_iota(jnp.int32, ang.shape, 0) < ROT_DIM // 2
    tab = jnp.where(half_rows, jnp.cos(ang), jnp.sin(ang))
    cs = sum(lax.dot_general(part, sel_ref[...], (((0,), (0,)), ((), ())), preferred_element_type=F32)
             for part in _split2(tab))
    cos = cs[:, :LANES] + unrot_ref[...]
    sin = cs[:, LANES:]
    def prep(x):
        scale = lax.rsqrt(jnp.sum(x * x, axis=-1, keepdims=True) * (1.0 / x.shape[-1]) + EPS)
        return (x * g_ref[...]).astype(BF16), scale

    h, rs = prep(x_ref[...])
    hp, rsp = prep(xp_ref[...])
    hn, rsn = prep(xn_ref[...])
    hcat_ref[0:HALO, :] = hp
    hcat_ref[HALO:HALO + tm, :] = h
    hcat_ref[HALO + tm:, :] = hn
    lane = lax.broadcasted_iota(jnp.int32, (1, LANES), 1)
    first_half = (lane % ATTN_HEAD_DIM) < (ROT_DIM // 2)

    def rotary(z):
        up = pltpu.roll(z, LANES - ROT_DIM // 2, 1)
        dn = pltpu.roll(z, ROT_DIM // 2, 1)
        return z * cos + jnp.where(first_half, up, dn) * sin

    aw = aw3 // 3
    npair = aw // LANES

    def proj(c0, width):
        return jnp.dot(h, w_ref[:, c0:c0 + width], preferred_element_type=F32) * rs

    zqkv = proj(0, aw3)
    q_scale = ATTN_HEAD_DIM ** -0.5 * LOG2E
    for j in range(npair):
        sl = slice(j * LANES, (j + 1) * LANES)
        q_ref[j] = rotary(zqkv[:, sl]) * q_scale
        k_ref[j] = rotary(zqkv[:, aw + j * LANES:aw + (j + 1) * LANES])
        v_ref[j] = zqkv[:, 2 * aw + j * LANES:2 * aw + (j + 1) * LANES]
    rows = tm + 2 * HALO
    ridx = lax.broadcasted_iota(jnp.int32, (rows, 1), 0)
    outside = ((ridx < HALO) & first) | ((ridx >= HALO + tm) & last)
    zu = jnp.dot(hcat_ref[...], w_ref[:, aw3:aw3 + mw], preferred_element_type=F32) \
        * jnp.concatenate([rsp, rs, rsn], axis=0)
    zu = jnp.where(outside, 0.0, zu)
    cw = cw_ref[...]
    a = pltpu.roll(zu, 1, 0)[HALO:HALO + tm] * cw[0:1] + zu[HALO:HALO + tm] * cw[1:2] \
        + pltpu.roll(zu, rows - 1, 0)[HALO:HALO + tm] * cw[2:3] + cb_ref[...]
    uc_ref[...] = (a * jax.nn.sigmoid(a)).astype(BF16)
    zvo = proj(aw3 + mw, 2 * mw)
    vm_ref[...] = zvo[:, :mw].astype(BF16)
    og_ref[...] = jax.nn.sigmoid(zvo[:, mw:])
    gates = proj(aw3 + 3 * mw, LANES)
    gate_ref[...] = gates.T[:gate_ref.shape[0]]


def _rotary_tables():
    half = ROT_DIM // 2
    inv = jnp.power(ROPE_THETA, -jnp.arange(0, ROT_DIM, 2, dtype=F32) / ROT_DIM)
    invc = jnp.concatenate([inv, inv])[:, None]
    sel = np.zeros((2 * half, 2 * LANES), np.float32)
    unrot = np.ones((1, LANES), np.float32)
    for l in range(LANES):
        hl = l % ATTN_HEAD_DIM
        if hl < ROT_DIM:
            sel[hl % half, l] = 1.0
            sel[half + hl % half, LANES + l] = -1.0 if hl < half else 1.0
            unrot[0, l] = 0.0
    return invc, jnp.asarray(sel, BF16), jnp.asarray(unrot)


def _inproj(x2, pos3, g, w, cw, cb, ng, aw, mw, tm, S):
    T, D = x2.shape
    assert w.shape[1] == 3 * aw + 3 * mw + LANES
    row = lambda n: pl.BlockSpec((tm, n), lambda i: (i, 0))
    hb = tm // HALO
    nhb = T // HALO
    prev = pl.BlockSpec((HALO, D), lambda i: (jnp.maximum(i * hb - 1, 0), 0))
    nxt = pl.BlockSpec((HALO, D), lambda i: (jnp.minimum((i + 1) * hb, nhb - 1), 0))
    tps = S // tm
    npair = aw // LANES
    pair_major = pl.BlockSpec((None, npair, tm, LANES), lambda i: (i // tps, 0, i % tps, 0))
    invc, sel, unrot = _rotary_tables()
    consts = (g, invc, sel, unrot, w, cw, cb)
    return pl.pallas_call(
        functools.partial(_inproj_kernel, tiles_per_seq=tps),
        grid=(T // tm,),
        in_specs=[row(D), prev, nxt, pl.BlockSpec((None, 1, tm), lambda i: (i, 0, 0))]
        + [_const_spec(c.shape) for c in consts],
        out_specs=[pair_major, pair_major, pair_major, row(mw), row(mw), row(mw),
                   pl.BlockSpec((ng, tm), lambda i: (0, i))],
        out_shape=[jax.ShapeDtypeStruct((T // S, npair, S, LANES), F32)] * 3
        + [jax.ShapeDtypeStruct((T, mw), BF16), jax.ShapeDtypeStruct((T, mw), BF16),
           jax.ShapeDtypeStruct((T, mw), F32), jax.ShapeDtypeStruct((ng, T), F32)],
        scratch_shapes=[pltpu.VMEM((tm + 2 * HALO, D), BF16)],
        compiler_params=pltpu.CompilerParams(dimension_semantics=("arbitrary",), vmem_limit_bytes=VMEM_LIMIT),
        name="inproj",
    )(x2, x2, x2, pos3, *consts)


def _attn_bias():
    W, R, Q = 2 * ATT_BLK, ATTN_RADIUS, ATT_BLK
    kj = np.arange(W)[:, None]
    qi = np.arange(Q)[None, :]
    interior = (kj - qi >= 0) & (kj - qi <= 2 * R)
    near = np.abs(kj - qi) <= R
    edge = ((qi < Q // 2) & (kj < Q) & near) | ((qi >= Q // 2) & (kj >= Q) & (np.abs(kj - Q - qi) <= R))
    single = near & (kj < Q)
    return jnp.asarray(np.where(np.stack([interior, edge, single]), 0.0, _NEG).astype(np.float32), BF16)


def _attn_kernel(q_ref, k_ref, v_ref, bias_ref, out_ref, qc_ref, kc_ref, vt0_ref, vt1_ref, o_scr, l_scr,
                 f4_scr, outf_scr):
    S = q_ref.shape[0]
    Q, HD = ATT_BLK, ATTN_HEAD_DIM
    lane = lax.broadcasted_iota(jnp.int32, (1, LANES), 1)
    head0 = lane < HD
    npat = len(ATTN_PATTERNS)
    vt0_ref[:, HD:, :] = jnp.ones((npat, HD, S), BF16)
    vt1_ref[:, :HD, :] = jnp.ones((npat, HD, S), BF16)

    def cat(parts, axis):
        return parts[0] if len(parts) == 1 else jnp.concatenate(parts, axis=axis)

    dm = ATTN_PATTERNS[1][1]
    assert [d for _, d in ATTN_PATTERNS] == [1, dm, dm * dm]
    Mm = S // dm

    def put(pi, base, xq, xk, xv):
        n = xq.shape[0]
        qc_ref[pi, base:base + n, :] = xq.astype(BF16)
        kc_ref[pi, base:base + n, :] = xk.astype(BF16)
        vt = xv.T.astype(BF16)
        vt0_ref[pi, :HD, base:base + n] = vt[:HD]
        vt1_ref[pi, HD:, base:base + n] = vt[HD:]

    put(0, 0, q_ref[...], k_ref[...], v_ref[...])
    for r in range(dm):
        xs = [src[pl.ds(r, Mm, stride=dm), :] for src in (q_ref, k_ref, v_ref)]
        for a, x in enumerate(xs):
            f4_scr[a, r * Mm:(r + 1) * Mm, :] = x
        put(1, r * Mm, *xs)
    for r in range(dm * dm):
        put(2, r * (Mm // dm), *[f4_scr[a, pl.ds((r % dm) * Mm + r // dm, Mm // dm, stride=dm), :] for a in range(3)])

    def out_rows(pi, r, start, size):
        if pi == 0:
            return pl.ds(start, size)
        if pi == 1:
            return pl.ds(r * Mm + start, size)
        return pl.ds((r % dm) * Mm + r // dm + dm * start, size, stride=dm)

    blocks = []
    for pi, (_, d) in enumerate(ATTN_PATTERNS):
        M = S // d
        nb = M // Q
        for r in range(d):
            base = r * M
            if nb == 1:
                blocks.append(dict(pi=pi, q=[slice(base, base + Q)], k=[slice(base, base + Q)], bias=2, W=Q,
                                   out=[(out_rows(pi, r, 0, Q), slice(0, Q))]))
                continue
            for i in range(nb - 1):
                k0 = base + i * Q
                blocks.append(dict(pi=pi, q=[slice(k0 + Q // 2, k0 + Q // 2 + Q)], k=[slice(k0, k0 + 2 * Q)],
                                   bias=0, W=2 * Q, out=[(out_rows(pi, r, i * Q + Q // 2, Q), slice(0, Q))]))
            blocks.append(dict(pi=pi, q=[slice(base, base + Q // 2), slice(base + M - Q // 2, base + M)],
                               k=[slice(base, base + Q), slice(base + M - Q, base + M)], bias=1, W=2 * Q,
                               out=[(out_rows(pi, r, 0, Q // 2), slice(0, Q // 2)),
                                    (out_rows(pi, r, M - Q // 2, Q // 2), slice(Q // 2, Q))]))

    def region_scores(region):
        scores = []
        for b in region:
            qb = cat([qc_ref[b["pi"], s, :] for s in b["q"]], 0)
            kw = cat([kc_ref[b["pi"], s, :] for s in b["k"]], 0)
            bias = bias_ref[b["bias"]][:b["W"]]
            for hh in range(2):
                qm = jnp.where(head0 if hh == 0 else jnp.logical_not(head0), qb, jnp.zeros_like(qb))
                s = lax.dot_general(kw, qm, (((1,), (1,)), ((), ())), preferred_element_type=F32)
                scores.append(s.astype(BF16) + bias)
        return scores

    def region_finish(region, scores):
        maxes = [jnp.max(s, axis=0, keepdims=True) for s in scores]
        probs = [jnp.exp2(s - m) for s, m in zip(scores, maxes)]
        maxes = [m.astype(F32) for m in maxes]
        accs = []
        for bi, b in enumerate(region):
            for hh, vref in enumerate((vt0_ref, vt1_ref)):
                vw = cat([vref[b["pi"], :, s] for s in b["k"]], 1)
                accs.append(jnp.dot(vw, probs[2 * bi + hh], preferred_element_type=F32))
        for bi, b in enumerate(region):
            o_t, l_t = [], []
            for hh in range(2):
                a = accs[2 * bi + hh]
                l = a[HD:HD + 1] if hh == 0 else a[0:1]
                o_t.append((a[:HD] if hh == 0 else a[HD:]) * (1.0 / l))
                l_t.append(jnp.broadcast_to(maxes[2 * bi + hh] + jnp.log2(l), (HD, Q)))
            o = jnp.concatenate(o_t, axis=0).T
            lse = jnp.concatenate(l_t, axis=0).T
            for rows, sl in b["out"]:
                o_scr[b["pi"], rows, :] = o[sl]
                l_scr[b["pi"], rows, :] = lse[sl]

    for g in range(0, len(blocks), ATT_REGION):
        region = blocks[g:g + ATT_REGION]
        region_finish(region, region_scores(region))

    for r in range(dm):
        tok = pl.ds(r, Mm, stride=dm)
        cls = slice(r * Mm, (r + 1) * Mm)
        os = (o_scr[0, tok, :], o_scr[1, cls, :], o_scr[2, cls, :])
        ls = (l_scr[0, tok, :], l_scr[1, cls, :], l_scr[2, cls, :])
        mx = jnp.maximum(jnp.maximum(ls[0], ls[1]), ls[2])
        ws = [jnp.exp2(l - mx) for l in ls]
        outf_scr[tok, :] = (ws[0] * os[0] + ws[1] * os[1] + ws[2] * os[2]) * (1.0 / (ws[0] + ws[1] + ws[2]))
    out_ref[...] = outf_scr[...].astype(out_ref.dtype)


def _attn(q, k, v, bias):
    B, npair, S, _ = q.shape
    blk = pl.BlockSpec((None, None, S, LANES), lambda b, j: (b, j, 0, 0))
    npat = len(ATTN_PATTERNS)
    return pl.pallas_call(
        _attn_kernel,
        grid=(B, npair),
        in_specs=[blk, blk, blk, _const_spec(bias.shape)],
        out_specs=pl.BlockSpec((None, S, LANES), lambda b, j: (b, 0, j)),
        out_shape=jax.ShapeDtypeStruct((B, S, npair * LANES), BF16),
        scratch_shapes=[pltpu.VMEM((npat, S, LANES), BF16), pltpu.VMEM((npat, S, LANES), BF16),
                        pltpu.VMEM((npat, LANES, S), BF16), pltpu.VMEM((npat, LANES, S), BF16),
                        pltpu.VMEM((npat, S, LANES), F32), pltpu.VMEM((npat, S, LANES), F32),
                        pltpu.VMEM((3, S, LANES), F32), pltpu.VMEM((S, LANES), F32)],
        compiler_params=pltpu.CompilerParams(dimension_semantics=("arbitrary", "arbitrary"),
                                             vmem_limit_bytes=VMEM_LIMIT),
        name="dilated_attn",
    )(q, k, v, bias)


def _split2(x):
    hi = x.astype(BF16)
    return hi, (x - hi.astype(F32)).astype(BF16)


def _log_sigmoid(x):
    return jnp.minimum(x, 0.0) - jnp.log1p(jnp.exp(-jnp.abs(x)))


def _mlstm_kernel(bias_ref, uc_ref, vm_ref, og_ref, grow_ref, wq_ref, wk_ref, gn_ref,
                  out_ref, q_scr, kt_scr, va_scr, hs_scr, lf_scr, b_scr, i_scr, w_scr, mp_scr, dc_scr,
                  e_scr, a_scr, f_scr, p_scr, nd_scr, cp_scr):
    S, dh = uc_ref.shape
    L = MLSTM_CHUNK
    nc = S // L
    hd = pl.program_id(1)

    uc = uc_ref[...]
    q_scr[...] = jnp.dot(uc, wq_ref[...], preferred_element_type=F32).astype(BF16)
    k = jnp.dot(uc, wk_ref[...], preferred_element_type=F32) * (dh ** -0.5)
    for c in range(nc):
        kt_scr[c] = k[c * L:(c + 1) * L].T.astype(BF16)
    va_scr[:, :dh] = vm_ref[...]
    va_scr[:, dh:] = jnp.ones((S, dh), BF16)

    ri = lax.broadcasted_iota(jnp.int32, (L, L), 0)
    ci = lax.broadcasted_iota(jnp.int32, (L, L), 1)
    lower = ri >= ci
    upper = ri <= ci

    for di in range(2):
        b_i = bias_ref[0, di, hd]
        b_f = bias_ref[1, di, hd]
        irow = (grow_ref[di] + b_i) * LOG2E
        lf = _log_sigmoid(grow_ref[2 + di] + b_f) * LOG2E
        tri = (upper if di == 0 else lower).astype(BF16)
        brow = sum(jnp.dot(part, tri, preferred_element_type=F32) for part in _split2(lf))
        g = jnp.sum(lf, axis=1, keepdims=True)
        logw = g - brow + irow
        gb = jnp.broadcast_to(g, (nc, dh))
        mxb = jnp.broadcast_to(jnp.max(logw, axis=1, keepdims=True), (nc, dh))
        m = jnp.zeros((1, dh), F32)
        mprev, mnew = [None] * nc, [None] * nc
        for step in range(nc):
            c = step if di == 0 else nc - 1 - step
            mprev[c] = m
            m = jnp.maximum(gb[c:c + 1] + m, mxb[c:c + 1])
            mnew[c] = m
        mprev = jnp.concatenate(mprev, axis=0)
        mnew = jnp.concatenate(mnew, axis=0)
        lf_scr[di] = lf
        b_scr[di] = brow
        i_scr[di] = irow
        w_scr[di] = jnp.exp2(logw - mnew)
        mp_scr[di] = mprev
        dc_scr[di] = jnp.exp2(gb + mprev - mnew)

    ones_rhs = jnp.ones((2 * L, dh), BF16)
    chunk = lambda c: slice(c * L, (c + 1) * L)

    for c in range(nc):
        for di in range(2):
            lfr, brow, irow, mprev = (ref[di, c:c + 1, :] for ref in (lf_scr, b_scr, i_scr, mp_scr))
            causal = lower if di == 0 else upper
            hi, lo = _split2(jnp.where(causal, lfr, 0.0))
            bcol = jnp.dot(jnp.concatenate([hi, lo], axis=1), ones_rhs, preferred_element_type=F32)
            dmat = jnp.where(causal, bcol - brow + irow, _NEG)
            m_inter = bcol + mprev
            m_t = jnp.maximum(m_inter, jnp.max(dmat, axis=1, keepdims=True))
            e_scr[di, c] = jnp.exp2(dmat - m_t)
            a_scr[di, c] = jnp.exp2(m_inter - m_t)
            f_scr[di, c] = jnp.exp2(-m_t)

    for c in range(nc):
        qk = jnp.dot(q_scr[chunk(c), :], kt_scr[c], preferred_element_type=F32)
        for di in range(2):
            p_scr[di, c] = (qk * e_scr[di, c]).astype(BF16)

    states = [jnp.zeros((dh, 2 * dh), F32)] * 2
    for step in range(nc):
        for di in range(2):
            c = step if di == 0 else nc - 1 - step
            va = va_scr[chunk(c), :]
            nd_scr[di, c] = jnp.dot(p_scr[di, c], va, preferred_element_type=F32)
            cp_scr[di, c] = states[di].astype(BF16)
            ktw = (kt_scr[c].astype(F32) * w_scr[di, c:c + 1, :]).astype(BF16)
            decay = dc_scr[di, c:c + 1, :]
            states[di] = jnp.concatenate([decay, decay], axis=1) * states[di] \
                + jnp.dot(ktw, va, preferred_element_type=F32)

    for c in range(nc):
        qc = q_scr[chunk(c), :]
        hsum = None
        for di in range(2):
            a_in = a_scr[di, c]
            nd = jnp.concatenate([a_in, a_in], axis=1) * jnp.dot(qc, cp_scr[di, c], preferred_element_type=F32) \
                + nd_scr[di, c]
            hval = nd[:, :dh] * (1.0 / jnp.maximum(jnp.abs(nd[:, dh:]), f_scr[di, c]))
            hsum = hval if hsum is None else hsum + hval
        hs_scr[chunk(c), :] = hsum

    hs = hs_scr[...]
    mu = jnp.sum(hs, axis=1, keepdims=True) * (1.0 / dh)
    cen = hs - mu
    var = jnp.sum(cen * cen, axis=1, keepdims=True) * (1.0 / dh)
    hn = cen * lax.rsqrt(var + EPS)
    out_ref[...] = (hn * gn_ref[...] * og_ref[...]).astype(out_ref.dtype)


def _mlstm(gate_bias, uc, vm, og, grow, wq, wk, gn):
    B, S, MW = uc.shape
    H, dh = MLSTM_HEADS, MLSTM_HEAD_DIM
    L = MLSTM_CHUNK
    assert L == dh == LANES
    nc = S // L
    tok = pl.BlockSpec((None, S, dh), lambda b, h: (b, 0, h))
    return pl.pallas_call(
        _mlstm_kernel,
        grid=(B, H),
        in_specs=[pl.BlockSpec(memory_space=pltpu.SMEM), tok, tok, tok,
                  pl.BlockSpec((None, None, 4, nc, L), lambda b, h: (b, h, 0, 0, 0)),
                  pl.BlockSpec((None, dh, dh), lambda b, h: (h, 0, 0)),
                  pl.BlockSpec((None, dh, dh), lambda b, h: (h, 0, 0)),
                  pl.BlockSpec((1, dh), lambda b, h: (0, h))],
        out_specs=tok,
        out_shape=jax.ShapeDtypeStruct((B, S, MW), BF16),
        scratch_shapes=[pltpu.VMEM((S, dh), BF16), pltpu.VMEM((nc, dh, L), BF16), pltpu.VMEM((S, 2 * dh), BF16),
                        pltpu.VMEM((S, dh), F32)] + [pltpu.VMEM((2, nc, L), F32)] * 6
        + [pltpu.VMEM((2, nc, L, L), F32), pltpu.VMEM((2, nc, L, dh), F32), pltpu.VMEM((2, nc, L, dh), F32),
           pltpu.VMEM((2, nc, L, L), BF16), pltpu.VMEM((2, nc, L, 2 * dh), F32),
           pltpu.VMEM((2, nc, dh, 2 * dh), BF16)],
        compiler_params=pltpu.CompilerParams(dimension_semantics=("arbitrary", "arbitrary"),
                                             vmem_limit_bytes=VMEM_LIMIT),
        name="mlstm",
    )(gate_bias, uc, vm, og, grow, wq, wk, gn)


FFN_CHUNK = 256


def _ffn_kernel(x_ref, xp_ref, xn_ref, a_ref, ap_ref, an_ref, m_ref, mp_ref, mn_ref, p_ref,
                wo_ref, g2_ref, wg_ref, wu_ref, wd_ref, cw_ref, cb_ref, gp_ref, wpg_ref, bpg_ref, wpp_ref, gf_ref,
                out_ref, mix_ref, hcat_ref, act_ref, *, tiles_per_seq):
    tm, aw = a_ref.shape
    ck = FFN_CHUNK
    nck = wg_ref.shape[1] // ck
    i = pl.program_id(0)
    first = (i % tiles_per_seq) == 0
    last = (i % tiles_per_seq) == tiles_per_seq - 1

    for lo, aa, mm in ((0, ap_ref, mp_ref), (HALO, a_ref, m_ref), (HALO + tm, an_ref, mn_ref)):
        mix_ref[lo:lo + aa.shape[0], :aw] = aa[...]
        mix_ref[lo:lo + aa.shape[0], aw:] = mm[...]
    y = jnp.dot(mix_ref[...], wo_ref[...], preferred_element_type=F32)
    x1 = x_ref[...] + y[HALO:HALO + tm]
    out_ref[...] = x1
    g2 = g2_ref[...]
    hcat_ref[HALO:HALO + tm, :] = _rms(x1, g2).astype(BF16)
    hp = _rms(xp_ref[...] + y[:HALO], g2).astype(BF16)
    hn = _rms(xn_ref[...] + y[HALO + tm:], g2).astype(BF16)
    hcat_ref[0:HALO, :] = jnp.where(first, jnp.zeros_like(hp), hp)
    hcat_ref[HALO + tm:, :] = jnp.where(last, jnp.zeros_like(hn), hn)

    rows = tm + 2 * HALO
    for c in range(nck):
        cols = slice(c * ck, (c + 1) * ck)
        g = jnp.dot(hcat_ref[...], wg_ref[:, cols], preferred_element_type=F32)
        up = jnp.dot(hcat_ref[HALO:HALO + tm, :], wu_ref[:, cols], preferred_element_type=F32)
        cw = cw_ref[:, cols]
        a = pltpu.roll(g, 1, 0)[HALO:HALO + tm] * cw[0:1] + g[HALO:HALO + tm] * cw[1:2] \
            + pltpu.roll(g, rows - 1, 0)[HALO:HALO + tm] * cw[2:3] + cb_ref[:, cols]
        act_ref[:, cols] = (0.5 * a * (1.0 + lax.erf(a * (2.0 ** -0.5))) * up).astype(BF16)
    x2 = out_ref[...] + jnp.dot(act_ref[...], wd_ref[...], preferred_element_type=F32)
    ms = jnp.sum(x2 * x2, axis=-1, keepdims=True) * (1.0 / x2.shape[-1])
    zg = jnp.dot((x2 * gp_ref[...]).astype(BF16), wpg_ref[...], preferred_element_type=F32)
    gate = jax.nn.sigmoid(zg * lax.rsqrt(ms + EPS) + bpg_ref[...])
    pp = jnp.dot(p_ref[...].astype(BF16), wpp_ref[...], preferred_element_type=F32)
    out_ref[...] = _rms(x2 + pp * gate, gf_ref[...])


def _ffn(x2d, attn, mem, p2, wo, g2, wg, wu, wd, cw, cb, gp, wpg, bpg, wpp, gf, tm, S):
    T, D = x2d.shape
    row = lambda w: pl.BlockSpec((tm, w), lambda i: (i, 0))
    hb = tm // HALO
    nhb = T // HALO
    prev = lambda w: pl.BlockSpec((HALO, w), lambda i: (jnp.maximum(i * hb - 1, 0), 0))
    nxt = lambda w: pl.BlockSpec((HALO, w), lambda i: (jnp.minimum((i + 1) * hb, nhb - 1), 0))
    aw, mw = attn.shape[1], mem.shape[1]
    consts = (wo, g2, wg, wu, wd, cw, cb, gp, wpg, bpg, wpp, gf)
    return pl.pallas_call(
        functools.partial(_ffn_kernel, tiles_per_seq=S // tm),
        grid=(T // tm,),
        in_specs=[row(D), prev(D), nxt(D), row(aw), prev(aw), nxt(aw), row(mw), prev(mw), nxt(mw),
                  row(p2.shape[1])] + [_const_spec(c.shape) for c in consts],
        out_specs=row(D),
        out_shape=jax.ShapeDtypeStruct((T, D), F32),
        scratch_shapes=[pltpu.VMEM((tm + 2 * HALO, aw + mw), BF16), pltpu.VMEM((tm + 2 * HALO, D), BF16),
                        pltpu.VMEM((tm, wg.shape[1]), BF16)],
        compiler_params=pltpu.CompilerParams(dimension_semantics=("arbitrary",), vmem_limit_bytes=VMEM_LIMIT),
        name="outproj_ffn_ple",
    )(x2d, x2d, x2d, attn, attn, attn, mem, mem, mem, p2, *consts)


def kernel(x, p, positions, ln_mix_g, w_in, mlstm_conv_w, mlstm_conv_b, w_mq, w_mk, b_igate, b_fgate,
           mlstm_gn_g, w_out, ln_ffn_g, w_ffn_gate, ffn_conv_w, ffn_conv_b, w_ffn_up, w_ffn_down,
           ln_ple_g, w_ple_gate, b_ple_gate, w_ple_proj, ln_final_g):
    B, S, D = x.shape
    depth = w_in.shape[0]
    T = B * S
    H, dh = MLSTM_HEADS, MLSTM_HEAD_DIM
    aw = D // 2
    mw = D - aw
    tm = ROW_TILE
    tm_in = INPROJ_TILE
    L = MLSTM_CHUNK
    nc = S // L

    bias = _attn_bias()
    pos3 = positions.reshape(T // tm_in, 1, tm_in)

    assert depth == 1, "the final norm is fused into the layer's last kernel"
    xf = x.reshape(T, D)
    for i in range(depth):
        nproj = 3 * aw + 3 * mw
        ng = w_in.shape[-1] - nproj
        w_pad = jnp.pad(w_in[i], ((0, 0), (0, LANES - ng))).astype(BF16)
        q, k, v, uc, vm, og, gates_t = _inproj(xf, pos3, ln_mix_g[i][None], w_pad, mlstm_conv_w[i],
                                               mlstm_conv_b[i][None], ng, aw, mw, tm_in, S)

        attn = _attn(q, k, v, bias)

        grow = gates_t.reshape(4, H, B, nc, L).transpose(2, 1, 0, 3, 4)
        gate_bias = jnp.stack([b_igate[i], b_fgate[i]]).astype(F32)
        mem = _mlstm(gate_bias, uc.reshape(B, S, mw), vm.reshape(B, S, mw), og.reshape(B, S, mw), grow,
                     w_mq[i].astype(BF16), w_mk[i].astype(BF16), mlstm_gn_g[i][None])

        assert w_ffn_gate.shape[-1] % FFN_CHUNK == 0
        xf = _ffn(xf, attn.reshape(T, aw), mem.reshape(T, mw), p[i].reshape(T, -1),
                  w_out[i].astype(BF16), ln_ffn_g[i][None],
                  w_ffn_gate[i].astype(BF16), w_ffn_up[i].astype(BF16), w_ffn_down[i].astype(BF16),
                  ffn_conv_w[i], ffn_conv_b[i][None], ln_ple_g[i][None],
                  w_ple_gate[i].astype(BF16), b_ple_gate[i][None], w_ple_proj[i].astype(BF16), ln_final_g[None], tm, S)
    return xf.reshape(B, S, D)
```

```python
import functools
import math

import jax
import jax.numpy as jnp
import numpy as np
from jax import lax
from jax.experimental import pallas as pl
from jax.experimental.pallas import tpu as pltpu

F32 = jnp.float32
BF16 = jnp.bfloat16

EPS = 1e-6
ATTN_HEAD_DIM = 64
ATTN_PATTERNS = ((128, 1), (512, 4), (2048, 16))
ATTN_RADIUS = 64
ROPE_THETA = 500000.0
ROT_DIM = ATTN_HEAD_DIM // 4
MLSTM_HEADS = 4
MLSTM_HEAD_DIM = 128
LANES = 128
ATT_BLK = 128
ATT_REGION = 8
MLSTM_CHUNK = 128
VMEM_LIMIT = 56 * 1024 * 1024
ROW_TILE = 1024
INPROJ_TILE = 1024
HALO = 16

_NEG = float("-inf")
LOG2E = math.log2(math.e)


def _rms(x, g):
    ms = jnp.sum(x * x, axis=-1, keepdims=True) * (1.0 / x.shape[-1])
    return x * lax.rsqrt(ms + EPS) * g


def _const_spec(shape):
    nd = len(shape)
    return pl.BlockSpec(shape, lambda *_: (0,) * nd, pipeline_mode=pl.Buffered(1))


def _inproj_kernel(x_ref, xp_ref, xn_ref, pos_ref, g_ref, invc_ref, sel_ref, unrot_ref, w_ref, wgt_ref, cw_ref, cb_ref,
                   q_ref, k_ref, v_ref, uc_ref, vm_ref, og_ref, gate_ref, hcat_ref, *, tiles_per_seq):
    tm = x_ref.shape[0]
    aw3 = 3 * q_ref.shape[0] * LANES
    mw = uc_ref.shape[1]
    i = pl.program_id(0)
    first = (i % tiles_per_seq) == 0
    last = (i % tiles_per_seq) == tiles_per_seq - 1
    ang = invc_ref[...] * pos_ref[...].astype(F32)
    half_rows = lax.broadcasted_iota(jnp.int32, ang.shape, 0) < ROT_DIM // 2
    tab = jnp.where(half_rows, jnp.cos(ang), jnp.sin(ang))
    cs = sum(lax.dot_general(part, sel_ref[...], (((0,), (0,)), ((), ())), preferred_element_type=F32)
             for part in _split2(tab))
    cos = cs[:, :LANES] + unrot_ref[...]
    sin = cs[:, LANES:]
    def prep(x):
        scale = lax.rsqrt(jnp.sum(x * x, axis=-1, keepdims=True) * (1.0 / x.shape[-1]) + EPS)
        return (x * g_ref[...]).astype(BF16), scale

    h, rs = prep(x_ref[...])
    hp, rsp = prep(xp_ref[...])
    hn, rsn = prep(xn_ref[...])
    hcat_ref[0:HALO, :] = hp
    hcat_ref[HALO:HALO + tm, :] = h
    hcat_ref[HALO + tm:, :] = hn
    lane = lax.broadcasted_iota(jnp.int32, (1, LANES), 1)
    first_half = (lane % ATTN_HEAD_DIM) < (ROT_DIM // 2)

    def rotary(z):
        up = pltpu.roll(z, LANES - ROT_DIM // 2, 1)
        dn = pltpu.roll(z, ROT_DIM // 2, 1)
        return z * cos + jnp.where(first_half, up, dn) * sin

    aw = aw3 // 3
    npair = aw // LANES

    def proj(c0, width):
        return jnp.dot(h, w_ref[:, c0:c0 + width], preferred_element_type=F32) * rs

    zqkv = proj(0, aw3)
    q_scale = ATTN_HEAD_DIM ** -0.5 * LOG2E
    for j in range(npair):
        sl = slice(j * LANES, (j + 1) * LANES)
        q_ref[j] = rotary(zqkv[:, sl]) * q_scale
        k_ref[j] = rotary(zqkv[:, aw + j * LANES:aw + (j + 1) * LANES])
        v_ref[j] = zqkv[:, 2 * aw + j * LANES:2 * aw + (j + 1) * LANES]
    rows = tm + 2 * HALO
    ridx = lax.broadcasted_iota(jnp.int32, (rows, 1), 0)
    outside = ((ridx < HALO) & first) | ((ridx >= HALO + tm) & last)
    zu = jnp.dot(hcat_ref[...], w_ref[:, aw3:aw3 + mw], preferred_element_type=F32) \
        * jnp.concatenate([rsp, rs, rsn], axis=0)
    zu = jnp.where(outside, 0.0, zu)
    cw = cw_ref[...]
    a = pltpu.roll(zu, 1, 0)[HALO:HALO + tm] * cw[0:1] + zu[HALO:HALO + tm] * cw[1:2] \
        + pltpu.roll(zu, rows - 1, 0)[HALO:HALO + tm] * cw[2:3] + cb_ref[...]
    uc_ref[...] = (a * jax.nn.sigmoid(a)).astype(BF16)
    zvo = proj(aw3 + mw, 2 * mw)
    vm_ref[...] = zvo[:, :mw].astype(BF16)
    og_ref[...] = jax.nn.sigmoid(zvo[:, mw:])
    gates = jnp.dot(h, wgt_ref[...], preferred_element_type=F32) * rs
    gate_ref[...] = gates.T[:gate_ref.shape[0]]


def _rotary_tables():
    half = ROT_DIM // 2
    inv = jnp.power(ROPE_THETA, -jnp.arange(0, ROT_DIM, 2, dtype=F32) / ROT_DIM)
    invc = jnp.concatenate([inv, inv])[:, None]
    sel = np.zeros((2 * half, 2 * LANES), np.float32)
    unrot = np.ones((1, LANES), np.float32)
    for l in range(LANES):
        hl = l % ATTN_HEAD_DIM
        if hl < ROT_DIM:
            sel[hl % half, l] = 1.0
            sel[half + hl % half, LANES + l] = -1.0 if hl < half else 1.0
            unrot[0, l] = 0.0
    return invc, jnp.asarray(sel, BF16), jnp.asarray(unrot)


def _inproj(x2, pos3, g, w, wgt, cw, cb, ng, aw, mw, tm, S):
    T, D = x2.shape
    assert wgt.shape == (D, LANES)
    row = lambda n: pl.BlockSpec((tm, n), lambda i: (i, 0))
    hb = tm // HALO
    nhb = T // HALO
    prev = pl.BlockSpec((HALO, D), lambda i: (jnp.maximum(i * hb - 1, 0), 0))
    nxt = pl.BlockSpec((HALO, D), lambda i: (jnp.minimum((i + 1) * hb, nhb - 1), 0))
    tps = S // tm
    npair = aw // LANES
    pair_major = pl.BlockSpec((None, npair, tm, LANES), lambda i: (i // tps, 0, i % tps, 0))
    invc, sel, unrot = _rotary_tables()
    consts = (g, invc, sel, unrot, w, wgt, cw, cb)
    return pl.pallas_call(
        functools.partial(_inproj_kernel, tiles_per_seq=tps),
        grid=(T // tm,),
        in_specs=[row(D), prev, nxt, pl.BlockSpec((None, 1, tm), lambda i: (i, 0, 0))]
        + [_const_spec(c.shape) for c in consts],
        out_specs=[pair_major, pair_major, pair_major, row(mw), row(mw), row(mw),
                   pl.BlockSpec((ng, tm), lambda i: (0, i))],
        out_shape=[jax.ShapeDtypeStruct((T // S, npair, S, LANES), F32)] * 3
        + [jax.ShapeDtypeStruct((T, mw), BF16), jax.ShapeDtypeStruct((T, mw), BF16),
           jax.ShapeDtypeStruct((T, mw), F32), jax.ShapeDtypeStruct((ng, T), F32)],
        scratch_shapes=[pltpu.VMEM((tm + 2 * HALO, D), BF16)],
        compiler_params=pltpu.CompilerParams(dimension_semantics=("arbitrary",), vmem_limit_bytes=VMEM_LIMIT),
        name="inproj",
    )(x2, x2, x2, pos3, *consts)


def _attn_bias():
    W, R, Q = 2 * ATT_BLK, ATTN_RADIUS, ATT_BLK
    kj = np.arange(W)[:, None]
    qi = np.arange(Q)[None, :]
    interior = (kj - qi >= 0) & (kj - qi <= 2 * R)
    near = np.abs(kj - qi) <= R
    edge = ((qi < Q // 2) & (kj < Q) & near) | ((qi >= Q // 2) & (kj >= Q) & (np.abs(kj - Q - qi) <= R))
    single = near & (kj < Q)
    return jnp.asarray(np.where(np.stack([interior, edge, single]), 0.0, _NEG).astype(np.float32), BF16)


def _attn_kernel(q_ref, k_ref, v_ref, bias_ref, out_ref, qc_ref, kc_ref, vt0_ref, vt1_ref, o_scr, l_scr,
                 f4_scr, outf_scr):
    S = q_ref.shape[0]
    Q, HD = ATT_BLK, ATTN_HEAD_DIM
    lane = lax.broadcasted_iota(jnp.int32, (1, LANES), 1)
    head0 = lane < HD
    npat = len(ATTN_PATTERNS)
    vt0_ref[:, HD:, :] = jnp.ones((npat, HD, S), BF16)
    vt1_ref[:, :HD, :] = jnp.ones((npat, HD, S), BF16)

    def cat(parts, axis):
        return parts[0] if len(parts) == 1 else jnp.concatenate(parts, axis=axis)

    dm = ATTN_PATTERNS[1][1]
    assert [d for _, d in ATTN_PATTERNS] == [1, dm, dm * dm]
    Mm = S // dm

    def put(pi, base, xq, xk, xv):
        n = xq.shape[0]
        qc_ref[pi, base:base + n, :] = xq.astype(BF16)
        kc_ref[pi, base:base + n, :] = xk.astype(BF16)
        vt = xv.T.astype(BF16)
        vt0_ref[pi, :HD, base:base + n] = vt[:HD]
        vt1_ref[pi, HD:, base:base + n] = vt[HD:]

    put(0, 0, q_ref[...], k_ref[...], v_ref[...])
    for r in range(dm):
        xs = [src[pl.ds(r, Mm, stride=dm), :] for src in (q_ref, k_ref, v_ref)]
        for a, x in enumerate(xs):
            f4_scr[a, r * Mm:(r + 1) * Mm, :] = x
        put(1, r * Mm, *xs)
    for r in range(dm * dm):
        put(2, r * (Mm // dm), *[f4_scr[a, pl.ds((r % dm) * Mm + r // dm, Mm // dm, stride=dm), :] for a in range(3)])

    def out_rows(pi, r, start, size):
        if pi == 0:
            return pl.ds(start, size)
        if pi == 1:
            return pl.ds(r * Mm + start, size)
        return pl.ds((r % dm) * Mm + r // dm + dm * start, size, stride=dm)

    blocks = []
    for pi, (_, d) in enumerate(ATTN_PATTERNS):
        M = S // d
        nb = M // Q
        for r in range(d):
            base = r * M
            if nb == 1:
                blocks.append(dict(pi=pi, q=[slice(base, base + Q)], k=[slice(base, base + Q)], bias=2, W=Q,
                                   out=[(out_rows(pi, r, 0, Q), slice(0, Q))]))
                continue
            for i in range(nb - 1):
                k0 = base + i * Q
                blocks.append(dict(pi=pi, q=[slice(k0 + Q // 2, k0 + Q // 2 + Q)], k=[slice(k0, k0 + 2 * Q)],
                                   bias=0, W=2 * Q, out=[(out_rows(pi, r, i * Q + Q // 2, Q), slice(0, Q))]))
            blocks.append(dict(pi=pi, q=[slice(base, base + Q // 2), slice(base + M - Q // 2, base + M)],
                               k=[slice(base, base + Q), slice(base + M - Q, base + M)], bias=1, W=2 * Q,
                               out=[(out_rows(pi, r, 0, Q // 2), slice(0, Q // 2)),
                                    (out_rows(pi, r, M - Q // 2, Q // 2), slice(Q // 2, Q))]))

    def region_scores(region):
        scores = []
        for b in region:
            qb = cat([qc_ref[b["pi"], s, :] for s in b["q"]], 0)
            kw = cat([kc_ref[b["pi"], s, :] for s in b["k"]], 0)
            bias = bias_ref[b["bias"]][:b["W"]]
            for hh in range(2):
                qm = jnp.where(head0 if hh == 0 else jnp.logical_not(head0), qb, jnp.zeros_like(qb))
                s = lax.dot_general(kw, qm, (((1,), (1,)), ((), ())), preferred_element_type=F32)
                scores.append(s.astype(BF16) + bias)
        return scores

    def region_finish(region, scores):
        maxes = [jnp.max(s, axis=0, keepdims=True) for s in scores]
        probs = [jnp.exp2(s - m) for s, m in zip(scores, maxes)]
        maxes = [m.astype(F32) for m in maxes]
        accs = []
        for bi, b in enumerate(region):
            for hh, vref in enumerate((vt0_ref, vt1_ref)):
                vw = cat([vref[b["pi"], :, s] for s in b["k"]], 1)
                accs.append(jnp.dot(vw, probs[2 * bi + hh], preferred_element_type=F32))
        for bi, b in enumerate(region):
            o_t, l_t = [], []
            for hh in range(2):
                a = accs[2 * bi + hh]
                l = a[HD:HD + 1] if hh == 0 else a[0:1]
                o_t.append((a[:HD] if hh == 0 else a[HD:]) * (1.0 / l))
                l_t.append(jnp.broadcast_to(maxes[2 * bi + hh] + jnp.log2(l), (HD, Q)))
            o = jnp.concatenate(o_t, axis=0).T
            lse = jnp.concatenate(l_t, axis=0).T
            for rows, sl in b["out"]:
                o_scr[b["pi"], rows, :] = o[sl]
                l_scr[b["pi"], rows, :] = lse[sl]

    for g in range(0, len(blocks), ATT_REGION):
        region = blocks[g:g + ATT_REGION]
        region_finish(region, region_scores(region))

    for r in range(dm):
        tok = pl.ds(r, Mm, stride=dm)
        cls = slice(r * Mm, (r + 1) * Mm)
        os = (o_scr[0, tok, :], o_scr[1, cls, :], o_scr[2, cls, :])
        ls = (l_scr[0, tok, :], l_scr[1, cls, :], l_scr[2, cls, :])
        mx = jnp.maximum(jnp.maximum(ls[0], ls[1]), ls[2])
        ws = [jnp.exp2(l - mx) for l in ls]
        outf_scr[tok, :] = (ws[0] * os[0] + ws[1] * os[1] + ws[2] * os[2]) * (1.0 / (ws[0] + ws[1] + ws[2]))
    out_ref[...] = outf_scr[...].astype(out_ref.dtype)


def _attn(q, k, v, bias):
    B, npair, S, _ = q.shape
    blk = pl.BlockSpec((None, None, S, LANES), lambda b, j: (b, j, 0, 0))
    npat = len(ATTN_PATTERNS)
    return pl.pallas_call(
        _attn_kernel,
        grid=(B, npair),
        in_specs=[blk, blk, blk, _const_spec(bias.shape)],
        out_specs=pl.BlockSpec((None, S, LANES), lambda b, j: (b, 0, j)),
        out_shape=jax.ShapeDtypeStruct((B, S, npair * LANES), BF16),
        scratch_shapes=[pltpu.VMEM((npat, S, LANES), BF16), pltpu.VMEM((npat, S, LANES), BF16),
                        pltpu.VMEM((npat, LANES, S), BF16), pltpu.VMEM((npat, LANES, S), BF16),
                        pltpu.VMEM((npat, S, LANES), F32), pltpu.VMEM((npat, S, LANES), F32),
                        pltpu.VMEM((3, S, LANES), F32), pltpu.VMEM((S, LANES), F32)],
        compiler_params=pltpu.CompilerParams(dimension_semantics=("arbitrary", "arbitrary"),
                                             vmem_limit_bytes=VMEM_LIMIT),
        name="dilated_attn",
    )(q, k, v, bias)


def _split2(x):
    hi = x.astype(BF16)
    return hi, (x - hi.astype(F32)).astype(BF16)


def _log_sigmoid(x):
    return jnp.minimum(x, 0.0) - jnp.log1p(jnp.exp(-jnp.abs(x)))


def _mlstm_kernel(bias_ref, uc_ref, vm_ref, og_ref, grow_ref, wq_ref, wk_ref, gn_ref,
                  out_ref, q_scr, kt_scr, va_scr, hs_scr, lf_scr, b_scr, i_scr, w_scr, mp_scr, dc_scr,
                  e_scr, a_scr, f_scr, p_scr, nd_scr, cp_scr):
    S, dh = uc_ref.shape
    L = MLSTM_CHUNK
    nc = S // L
    hd = pl.program_id(1)

    uc = uc_ref[...]
    q_scr[...] = jnp.dot(uc, wq_ref[...], preferred_element_type=F32).astype(BF16)
    k = jnp.dot(uc, wk_ref[...], preferred_element_type=F32) * (dh ** -0.5)
    for c in range(nc):
        kt_scr[c] = k[c * L:(c + 1) * L].T.astype(BF16)
    va_scr[:, :dh] = vm_ref[...]
    va_scr[:, dh:] = jnp.ones((S, dh), BF16)

    ri = lax.broadcasted_iota(jnp.int32, (L, L), 0)
    ci = lax.broadcasted_iota(jnp.int32, (L, L), 1)
    lower = ri >= ci
    upper = ri <= ci

    for di in range(2):
        b_i = bias_ref[0, di, hd]
        b_f = bias_ref[1, di, hd]
        irow = (grow_ref[di] + b_i) * LOG2E
        lf = _log_sigmoid(grow_ref[2 + di] + b_f) * LOG2E
        tri = (upper if di == 0 else lower).astype(BF16)
        brow = sum(jnp.dot(part, tri, preferred_element_type=F32) for part in _split2(lf))
        g = jnp.sum(lf, axis=1, keepdims=True)
        logw = g - brow + irow
        gb = jnp.broadcast_to(g, (nc, dh))
        mxb = jnp.broadcast_to(jnp.max(logw, axis=1, keepdims=True), (nc, dh))
        m = jnp.zeros((1, dh), F32)
        mprev, mnew = [None] * nc, [None] * nc
        for step in range(nc):
            c = step if di == 0 else nc - 1 - step
            mprev[c] = m
            m = jnp.maximum(gb[c:c + 1] + m, mxb[c:c + 1])
            mnew[c] = m
        mprev = jnp.concatenate(mprev, axis=0)
        mnew = jnp.concatenate(mnew, axis=0)
        lf_scr[di] = lf
        b_scr[di] = brow
        i_scr[di] = irow
        w_scr[di] = jnp.exp2(logw - mnew)
        mp_scr[di] = mprev
        dc_scr[di] = jnp.exp2(gb + mprev - mnew)

    ones_rhs = jnp.ones((2 * L, dh), BF16)
    chunk = lambda c: slice(c * L, (c + 1) * L)

    for c in range(nc):
        for di in range(2):
            lfr, brow, irow, mprev = (ref[di, c:c + 1, :] for ref in (lf_scr, b_scr, i_scr, mp_scr))
            causal = lower if di == 0 else upper
            hi, lo = _split2(jnp.where(causal, lfr, 0.0))
            bcol = jnp.dot(jnp.concatenate([hi, lo], axis=1), ones_rhs, preferred_element_type=F32)
            dmat = jnp.where(causal, bcol - brow + irow, _NEG)
            m_inter = bcol + mprev
            m_t = jnp.maximum(m_inter, jnp.max(dmat, axis=1, keepdims=True))
            e_scr[di, c] = jnp.exp2(dmat - m_t)
            a_scr[di, c] = jnp.exp2(m_inter - m_t)
            f_scr[di, c] = jnp.exp2(-m_t)

    for c in range(nc):
        qk = jnp.dot(q_scr[chunk(c), :], kt_scr[c], preferred_element_type=F32)
        for di in range(2):
            p_scr[di, c] = (qk * e_scr[di, c]).astype(BF16)

    states = [jnp.zeros((dh, 2 * dh), F32)] * 2
    for step in range(nc):
        for di in range(2):
            c = step if di == 0 else nc - 1 - step
            va = va_scr[chunk(c), :]
            nd_scr[di, c] = jnp.dot(p_scr[di, c], va, preferred_element_type=F32)
            cp_scr[di, c] = states[di].astype(BF16)
            ktw = (kt_scr[c].astype(F32) * w_scr[di, c:c + 1, :]).astype(BF16)
            decay = dc_scr[di, c:c + 1, :]
            states[di] = jnp.concatenate([decay, decay], axis=1) * states[di] \
                + jnp.dot(ktw, va, preferred_element_type=F32)

    for c in range(nc):
        qc = q_scr[chunk(c), :]
        hsum = None
        for di in range(2):
            a_in = a_scr[di, c]
            nd = jnp.concatenate([a_in, a_in], axis=1) * jnp.dot(qc, cp_scr[di, c], preferred_element_type=F32) \
                + nd_scr[di, c]
            hval = nd[:, :dh] * (1.0 / jnp.maximum(jnp.abs(nd[:, dh:]), f_scr[di, c]))
            hsum = hval if hsum is None else hsum + hval
        hs_scr[chunk(c), :] = hsum

    hs = hs_scr[...]
    mu = jnp.sum(hs, axis=1, keepdims=True) * (1.0 / dh)
    cen = hs - mu
    var = jnp.sum(cen * cen, axis=1, keepdims=True) * (1.0 / dh)
    hn = cen * lax.rsqrt(var + EPS)
    out_ref[...] = (hn * gn_ref[...] * og_ref[...]).astype(out_ref.dtype)


def _mlstm(gate_bias, uc, vm, og, grow, wq, wk, gn):
    B, S, MW = uc.shape
    H, dh = MLSTM_HEADS, MLSTM_HEAD_DIM
    L = MLSTM_CHUNK
    assert L == dh == LANES
    nc = S // L
    tok = pl.BlockSpec((None, S, dh), lambda b, h: (b, 0, h))
    return pl.pallas_call(
        _mlstm_kernel,
        grid=(B, H),
        in_specs=[pl.BlockSpec(memory_space=pltpu.SMEM), tok, tok, tok,
                  pl.BlockSpec((None, None, 4, nc, L), lambda b, h: (b, h, 0, 0, 0)),
                  pl.BlockSpec((None, dh, dh), lambda b, h: (h, 0, 0)),
                  pl.BlockSpec((None, dh, dh), lambda b, h: (h, 0, 0)),
                  pl.BlockSpec((1, dh), lambda b, h: (0, h))],
        out_specs=tok,
        out_shape=jax.ShapeDtypeStruct((B, S, MW), BF16),
        scratch_shapes=[pltpu.VMEM((S, dh), BF16), pltpu.VMEM((nc, dh, L), BF16), pltpu.VMEM((S, 2 * dh), BF16),
                        pltpu.VMEM((S, dh), F32)] + [pltpu.VMEM((2, nc, L), F32)] * 6
        + [pltpu.VMEM((2, nc, L, L), F32), pltpu.VMEM((2, nc, L, dh), F32), pltpu.VMEM((2, nc, L, dh), F32),
           pltpu.VMEM((2, nc, L, L), BF16), pltpu.VMEM((2, nc, L, 2 * dh), F32),
           pltpu.VMEM((2, nc, dh, 2 * dh), BF16)],
        compiler_params=pltpu.CompilerParams(dimension_semantics=("arbitrary", "arbitrary"),
                                             vmem_limit_bytes=VMEM_LIMIT),
        name="mlstm",
    )(gate_bias, uc, vm, og, grow, wq, wk, gn)


FFN_CHUNK = 256


def _ffn_kernel(x_ref, xp_ref, xn_ref, a_ref, ap_ref, an_ref, m_ref, mp_ref, mn_ref, p_ref,
                wo_ref, g2_ref, wg_ref, wu_ref, wd_ref, cw_ref, cb_ref, gp_ref, wpg_ref, bpg_ref, wpp_ref, gf_ref,
                out_ref, mix_ref, hcat_ref, act_ref, *, tiles_per_seq):
    tm, aw = a_ref.shape
    ck = FFN_CHUNK
    nck = wg_ref.shape[1] // ck
    i = pl.program_id(0)
    first = (i % tiles_per_seq) == 0
    last = (i % tiles_per_seq) == tiles_per_seq - 1

    for lo, aa, mm in ((0, ap_ref, mp_ref), (HALO, a_ref, m_ref), (HALO + tm, an_ref, mn_ref)):
        mix_ref[lo:lo + aa.shape[0], :aw] = aa[...]
        mix_ref[lo:lo + aa.shape[0], aw:] = mm[...]
    y = jnp.dot(mix_ref[...], wo_ref[...], preferred_element_type=F32)
    x1 = x_ref[...] + y[HALO:HALO + tm]
    out_ref[...] = x1
    def prep(v):
        scale = lax.rsqrt(jnp.sum(v * v, axis=-1, keepdims=True) * (1.0 / v.shape[-1]) + EPS)
        return (v * g2_ref[...]).astype(BF16), scale

    h, rs = prep(x1)
    hp, rsp = prep(xp_ref[...] + y[:HALO])
    hn, rsn = prep(xn_ref[...] + y[HALO + tm:])
    hcat_ref[HALO:HALO + tm, :] = h
    hcat_ref[0:HALO, :] = jnp.where(first, jnp.zeros_like(hp), hp)
    hcat_ref[HALO + tm:, :] = jnp.where(last, jnp.zeros_like(hn), hn)
    rs_ext = jnp.concatenate([rsp, rs, rsn], axis=0)

    rows = tm + 2 * HALO
    for c in range(nck):
        cols = slice(c * ck, (c + 1) * ck)
        g = jnp.dot(hcat_ref[...], wg_ref[:, cols], preferred_element_type=F32) * rs_ext
        up = jnp.dot(hcat_ref[HALO:HALO + tm, :], wu_ref[:, cols], preferred_element_type=F32) * rs
        cw = cw_ref[:, cols]
        a = pltpu.roll(g, 1, 0)[HALO:HALO + tm] * cw[0:1] + g[HALO:HALO + tm] * cw[1:2] \
            + pltpu.roll(g, rows - 1, 0)[HALO:HALO + tm] * cw[2:3] + cb_ref[:, cols]
        act_ref[:, cols] = (0.5 * a * (1.0 + lax.erf(a * (2.0 ** -0.5))) * up).astype(BF16)
    x2 = out_ref[...] + jnp.dot(act_ref[...], wd_ref[...], preferred_element_type=F32)
    ms = jnp.sum(x2 * x2, axis=-1, keepdims=True) * (1.0 / x2.shape[-1])
    zg = jnp.dot((x2 * gp_ref[...]).astype(BF16), wpg_ref[...], preferred_element_type=F32)
    gate = jax.nn.sigmoid(zg * lax.rsqrt(ms + EPS) + bpg_ref[...])
    pp = jnp.dot(p_ref[...].astype(BF16), wpp_ref[...], preferred_element_type=F32)
    out_ref[...] = _rms(x2 + pp * gate, gf_ref[...])


def _ffn(x2d, attn, mem, p2, wo, g2, wg, wu, wd, cw, cb, gp, wpg, bpg, wpp, gf, tm, S):
    T, D = x2d.shape
    row = lambda w: pl.BlockSpec((tm, w), lambda i: (i, 0))
    hb = tm // HALO
    nhb = T // HALO
    prev = lambda w: pl.BlockSpec((HALO, w), lambda i: (jnp.maximum(i * hb - 1, 0), 0))
    nxt = lambda w: pl.BlockSpec((HALO, w), lambda i: (jnp.minimum((i + 1) * hb, nhb - 1), 0))
    aw, mw = attn.shape[1], mem.shape[1]
    consts = (wo, g2, wg, wu, wd, cw, cb, gp, wpg, bpg, wpp, gf)
    return pl.pallas_call(
        functools.partial(_ffn_kernel, tiles_per_seq=S // tm),
        grid=(T // tm,),
        in_specs=[row(D), prev(D), nxt(D), row(aw), prev(aw), nxt(aw), row(mw), prev(mw), nxt(mw),
                  row(p2.shape[1])] + [_const_spec(c.shape) for c in consts],
        out_specs=row(D),
        out_shape=jax.ShapeDtypeStruct((T, D), F32),
        scratch_shapes=[pltpu.VMEM((tm + 2 * HALO, aw + mw), BF16), pltpu.VMEM((tm + 2 * HALO, D), BF16),
                        pltpu.VMEM((tm, wg.shape[1]), BF16)],
        compiler_params=pltpu.CompilerParams(dimension_semantics=("arbitrary",), vmem_limit_bytes=VMEM_LIMIT),
        name="outproj_ffn_ple",
    )(x2d, x2d, x2d, attn, attn, attn, mem, mem, mem, p2, *consts)


def kernel(x, p, positions, ln_mix_g, w_in, mlstm_conv_w, mlstm_conv_b, w_mq, w_mk, b_igate, b_fgate,
           mlstm_gn_g, w_out, ln_ffn_g, w_ffn_gate, ffn_conv_w, ffn_conv_b, w_ffn_up, w_ffn_down,
           ln_ple_g, w_ple_gate, b_ple_gate, w_ple_proj, ln_final_g):
    B, S, D = x.shape
    depth = w_in.shape[0]
    T = B * S
    H, dh = MLSTM_HEADS, MLSTM_HEAD_DIM
    aw = D // 2
    mw = D - aw
    tm = ROW_TILE
    tm_in = INPROJ_TILE
    L = MLSTM_CHUNK
    nc = S // L

    bias = _attn_bias()
    pos3 = positions.reshape(T // tm_in, 1, tm_in)

    assert depth == 1, "the final norm is fused into the layer's last kernel"
    xf = x.reshape(T, D)
    for i in range(depth):
        nproj = 3 * aw + 3 * mw
        ng = w_in.shape[-1] - nproj
        w_gates = jnp.pad(w_in[i][:, nproj:], ((0, 0), (0, LANES - ng))).astype(BF16)
        q, k, v, uc, vm, og, gates_t = _inproj(xf, pos3, ln_mix_g[i][None], w_in[i].astype(BF16), w_gates, mlstm_conv_w[i],
                                               mlstm_conv_b[i][None], ng, aw, mw, tm_in, S)

        attn = _attn(q, k, v, bias)

        grow = gates_t.reshape(4, H, B, nc, L).transpose(2, 1, 0, 3, 4)
        gate_bias = jnp.stack([b_igate[i], b_fgate[i]]).astype(F32)
        mem = _mlstm(gate_bias, uc.reshape(B, S, mw), vm.reshape(B, S, mw), og.reshape(B, S, mw), grow,
                     w_mq[i].astype(BF16), w_mk[i].astype(BF16), mlstm_gn_g[i][None])

        assert w_ffn_gate.shape[-1] % FFN_CHUNK == 0
        xf = _ffn(xf, attn.reshape(T, aw), mem.reshape(T, mw), p[i].reshape(T, -1),
                  w_out[i].astype(BF16), ln_ffn_g[i][None],
                  w_ffn_gate[i].astype(BF16), w_ffn_up[i].astype(BF16), w_ffn_down[i].astype(BF16),
                  ffn_conv_w[i], ffn_conv_b[i][None], ln_ple_g[i][None],
                  w_ple_gate[i].astype(BF16), b_ple_gate[i][None], w_ple_proj[i].astype(BF16), ln_final_g[None], tm, S)
    return xf.reshape(B, S, D)
```

```python
import functools
import math

import jax
import jax.numpy as jnp
import numpy as np
from jax import lax
from jax.experimental import pallas as pl
from jax.experimental.pallas import tpu as pltpu

F32 = jnp.float32
BF16 = jnp.bfloat16

EPS = 1e-6
ATTN_HEAD_DIM = 64
ATTN_PATTERNS = ((128, 1), (512, 4), (2048, 16))
ATTN_RADIUS = 64
ROPE_THETA = 500000.0
ROT_DIM = ATTN_HEAD_DIM // 4
MLSTM_HEADS = 4
MLSTM_HEAD_DIM = 128
LANES = 128
ATT_BLK = 128
ATT_REGION = 8
MLSTM_CHUNK = 128
VMEM_LIMIT = 56 * 1024 * 1024
ROW_TILE = 1024
INPROJ_TILE = 1024
HALO = 16

_NEG = float("-inf")
LOG2E = math.log2(math.e)


def _rms(x, g):
    ms = jnp.sum(x * x, axis=-1, keepdims=True) * (1.0 / x.shape[-1])
    return x * lax.rsqrt(ms + EPS) * g


def _const_spec(shape):
    nd = len(shape)
    return pl.BlockSpec(shape, lambda *_: (0,) * nd, pipeline_mode=pl.Buffered(1))


def _inproj_kernel(x_ref, xp_ref, xn_ref, pos_ref, g_ref, invc_ref, sel_ref, unrot_ref, w_ref, wgt_ref, cw_ref, cb_ref,
                   q_ref, k_ref, v_ref, uc_ref, vm_ref, og_ref, gate_ref, hcat_ref, *, tiles_per_seq):
    tm = x_ref.shape[0]
    aw3 = 3 * q_ref.shape[0] * LANES
    mw = uc_ref.shape[1]
    i = pl.program_id(0)
    first = (i % tiles_per_seq) == 0
    last = (i % tiles_per_seq) == tiles_per_seq - 1
    ang = invc_ref[...] * pos_ref[...].astype(F32)
    half_rows = lax.broadcasted_iota(jnp.int32, ang.shape, 0) < ROT_DIM // 2
    tab = jnp.where(half_rows, jnp.cos(ang), jnp.sin(ang))
    cs = sum(lax.dot_general(part, sel_ref[...], (((0,), (0,)), ((), ())), preferred_element_type=F32)
             for part in _split2(tab))
    cos = cs[:, :LANES] + unrot_ref[...]
    sin = cs[:, LANES:]
    def prep(x):
        scale = lax.rsqrt(jnp.sum(x * x, axis=-1, keepdims=True) * (1.0 / x.shape[-1]) + EPS)
        return (x * g_ref[...]).astype(BF16), scale

    h, rs = prep(x_ref[...])
    hp, rsp = prep(xp_ref[...])
    hn, rsn = prep(xn_ref[...])
    hcat_ref[0:HALO, :] = hp
    hcat_ref[HALO:HALO + tm, :] = h
    hcat_ref[HALO + tm:, :] = hn
    lane = lax.broadcasted_iota(jnp.int32, (1, LANES), 1)
    first_half = (lane % ATTN_HEAD_DIM) < (ROT_DIM // 2)

    def rotary(z):
        up = pltpu.roll(z, LANES - ROT_DIM // 2, 1)
        dn = pltpu.roll(z, ROT_DIM // 2, 1)
        return z * cos + jnp.where(first_half, up, dn) * sin

    aw = aw3 // 3
    npair = aw // LANES

    def proj(c0, width):
        return jnp.dot(h, w_ref[:, c0:c0 + width], preferred_element_type=F32) * rs

    zqkv = proj(0, aw3)
    q_scale = ATTN_HEAD_DIM ** -0.5 * LOG2E
    for j in range(npair):
        sl = slice(j * LANES, (j + 1) * LANES)
        q_ref[j] = rotary(zqkv[:, sl]) * q_scale
        k_ref[j] = rotary(zqkv[:, aw + j * LANES:aw + (j + 1) * LANES])
        v_ref[j] = zqkv[:, 2 * aw + j * LANES:2 * aw + (j + 1) * LANES]
    rows = tm + 2 * HALO
    ridx = lax.broadcasted_iota(jnp.int32, (rows, 1), 0)
    outside = ((ridx < HALO) & first) | ((ridx >= HALO + tm) & last)
    zu = jnp.dot(hcat_ref[...], w_ref[:, aw3:aw3 + mw], preferred_element_type=F32) \
        * jnp.concatenate([rsp, rs, rsn], axis=0)
    zu = jnp.where(outside, 0.0, zu)
    cw = cw_ref[...]
    a = pltpu.roll(zu, 1, 0)[HALO:HALO + tm] * cw[0:1] + zu[HALO:HALO + tm] * cw[1:2] \
        + pltpu.roll(zu, rows - 1, 0)[HALO:HALO + tm] * cw[2:3] + cb_ref[...]
    uc_ref[...] = (a * jax.nn.sigmoid(a)).astype(BF16)
    zvo = proj(aw3 + mw, 2 * mw)
    vm_ref[...] = zvo[:, :mw].astype(BF16)
    og_ref[...] = jax.nn.sigmoid(zvo[:, mw:])
    gates = jnp.dot(h, wgt_ref[...], preferred_element_type=F32) * rs
    gate_ref[...] = gates.T[:gate_ref.shape[0]]


def _rotary_tables():
    half = ROT_DIM // 2
    inv = jnp.power(ROPE_THETA, -jnp.arange(0, ROT_DIM, 2, dtype=F32) / ROT_DIM)
    invc = jnp.concatenate([inv, inv])[:, None]
    sel = np.zeros((2 * half, 2 * LANES), np.float32)
    unrot = np.ones((1, LANES), np.float32)
    for l in range(LANES):
        hl = l % ATTN_HEAD_DIM
        if hl < ROT_DIM:
            sel[hl % half, l] = 1.0
            sel[half + hl % half, LANES + l] = -1.0 if hl < half else 1.0
            unrot[0, l] = 0.0
    return invc, jnp.asarray(sel, BF16), jnp.asarray(unrot)


def _inproj(x2, pos3, g, w, wgt, cw, cb, ng, aw, mw, tm, S):
    T, D = x2.shape
    assert wgt.shape == (D, LANES)
    row = lambda n: pl.BlockSpec((tm, n), lambda i: (i, 0))
    hb = tm // HALO
    nhb = T // HALO
    prev = pl.BlockSpec((HALO, D), lambda i: (jnp.maximum(i * hb - 1, 0), 0))
    nxt = pl.BlockSpec((HALO, D), lambda i: (jnp.minimum((i + 1) * hb, nhb - 1), 0))
    tps = S // tm
    npair = aw // LANES
    pair_major = pl.BlockSpec((None, npair, tm, LANES), lambda i: (i // tps, 0, i % tps, 0))
    invc, sel, unrot = _rotary_tables()
    consts = (g, invc, sel, unrot, w, wgt, cw, cb)
    return pl.pallas_call(
        functools.partial(_inproj_kernel, tiles_per_seq=tps),
        grid=(T // tm,),
        in_specs=[row(D), prev, nxt, pl.BlockSpec((None, 1, tm), lambda i: (i, 0, 0))]
        + [_const_spec(c.shape) for c in consts],
        out_specs=[pair_major, pair_major, pair_major, row(mw), row(mw), row(mw),
                   pl.BlockSpec((ng, tm), lambda i: (0, i))],
        out_shape=[jax.ShapeDtypeStruct((T // S, npair, S, LANES), F32)] * 3
        + [jax.ShapeDtypeStruct((T, mw), BF16), jax.ShapeDtypeStruct((T, mw), BF16),
           jax.ShapeDtypeStruct((T, mw), F32), jax.ShapeDtypeStruct((ng, T), F32)],
        scratch_shapes=[pltpu.VMEM((tm + 2 * HALO, D), BF16)],
        compiler_params=pltpu.CompilerParams(dimension_semantics=("arbitrary",), vmem_limit_bytes=VMEM_LIMIT),
        name="inproj",
    )(x2, x2, x2, pos3, *consts)


def _attn_bias():
    W, R, Q = 2 * ATT_BLK, ATTN_RADIUS, ATT_BLK
    kj = np.arange(W)[:, None]
    qi = np.arange(Q)[None, :]
    interior = (kj - qi >= 0) & (kj - qi <= 2 * R)
    near = np.abs(kj - qi) <= R
    edge = ((qi < Q // 2) & (kj < Q) & near) | ((qi >= Q // 2) & (kj >= Q) & (np.abs(kj - Q - qi) <= R))
    single = near & (kj < Q)
    return jnp.asarray(np.where(np.stack([interior, edge, single]), 0.0, _NEG).astype(np.float32), BF16)


def _attn_kernel(q_ref, k_ref, v_ref, bias_ref, out_ref, qc_ref, kc_ref, vt0_ref, vt1_ref, o_scr, l_scr,
                 f4_scr, outf_scr):
    S = q_ref.shape[0]
    Q, HD = ATT_BLK, ATTN_HEAD_DIM
    lane = lax.broadcasted_iota(jnp.int32, (1, LANES), 1)
    head0 = lane < HD
    npat = len(ATTN_PATTERNS)
    vt0_ref[:, HD:, :] = jnp.ones((npat, HD, S), BF16)
    vt1_ref[:, :HD, :] = jnp.ones((npat, HD, S), BF16)

    def cat(parts, axis):
        return parts[0] if len(parts) == 1 else jnp.concatenate(parts, axis=axis)

    dm = ATTN_PATTERNS[1][1]
    assert [d for _, d in ATTN_PATTERNS] == [1, dm, dm * dm]
    Mm = S // dm

    def put(pi, base, xq, xk, xv):
        n = xq.shape[0]
        qc_ref[pi, base:base + n, :] = xq.astype(BF16)
        kc_ref[pi, base:base + n, :] = xk.astype(BF16)
        vt = xv.T.astype(BF16)
        vt0_ref[pi, :HD, base:base + n] = vt[:HD]
        vt1_ref[pi, HD:, base:base + n] = vt[HD:]

    put(0, 0, q_ref[...], k_ref[...], v_ref[...])
    for r in range(dm):
        xs = [src[pl.ds(r, Mm, stride=dm), :] for src in (q_ref, k_ref, v_ref)]
        for a, x in enumerate(xs):
            f4_scr[a, r * Mm:(r + 1) * Mm, :] = x
        put(1, r * Mm, *xs)
    for r in range(dm * dm):
        put(2, r * (Mm // dm), *[f4_scr[a, pl.ds((r % dm) * Mm + r // dm, Mm // dm, stride=dm), :] for a in range(3)])

    def out_rows(pi, r, start, size):
        if pi == 0:
            return pl.ds(start, size)
        if pi == 1:
            return pl.ds(r * Mm + start, size)
        return pl.ds((r % dm) * Mm + r // dm + dm * start, size, stride=dm)

    blocks = []
    for pi, (_, d) in enumerate(ATTN_PATTERNS):
        M = S // d
        nb = M // Q
        for r in range(d):
            base = r * M
            if nb == 1:
                blocks.append(dict(pi=pi, q=[slice(base, base + Q)], k=[slice(base, base + Q)], bias=2, W=Q,
                                   out=[(out_rows(pi, r, 0, Q), slice(0, Q))]))
                continue
            for i in range(nb - 1):
                k0 = base + i * Q
                blocks.append(dict(pi=pi, q=[slice(k0 + Q // 2, k0 + Q // 2 + Q)], k=[slice(k0, k0 + 2 * Q)],
                                   bias=0, W=2 * Q, out=[(out_rows(pi, r, i * Q + Q // 2, Q), slice(0, Q))]))
            blocks.append(dict(pi=pi, q=[slice(base, base + Q // 2), slice(base + M - Q // 2, base + M)],
                               k=[slice(base, base + Q), slice(base + M - Q, base + M)], bias=1, W=2 * Q,
                               out=[(out_rows(pi, r, 0, Q // 2), slice(0, Q // 2)),
                                    (out_rows(pi, r, M - Q // 2, Q // 2), slice(Q // 2, Q))]))

    def region_scores(region):
        scores = []
        for b in region:
            qb = cat([qc_ref[b["pi"], s, :] for s in b["q"]], 0)
            kw = cat([kc_ref[b["pi"], s, :] for s in b["k"]], 0)
            bias = bias_ref[b["bias"]][:b["W"]]
            for hh in range(2):
                qm = jnp.where(head0 if hh == 0 else jnp.logical_not(head0), qb, jnp.zeros_like(qb))
                s = lax.dot_general(kw, qm, (((1,), (1,)), ((), ())), preferred_element_type=F32)
                scores.append(s.astype(BF16) + bias)
        return scores

    def region_finish(region, scores):
        maxes = [jnp.max(s, axis=0, keepdims=True) for s in scores]
        probs = [jnp.exp2(s - m) for s, m in zip(scores, maxes)]
        maxes = [m.astype(F32) for m in maxes]
        accs = []
        for bi, b in enumerate(region):
            for hh, vref in enumerate((vt0_ref, vt1_ref)):
                vw = cat([vref[b["pi"], :, s] for s in b["k"]], 1)
                accs.append(jnp.dot(vw, probs[2 * bi + hh], preferred_element_type=F32))
        for bi, b in enumerate(region):
            o_t, l_t = [], []
            for hh in range(2):
                a = accs[2 * bi + hh]
                l = a[HD:HD + 1] if hh == 0 else a[0:1]
                o_t.append((a[:HD] if hh == 0 else a[HD:]) * (1.0 / l))
                l_t.append(jnp.broadcast_to(maxes[2 * bi + hh] + jnp.log2(l), (HD, Q)))
            o = jnp.concatenate(o_t, axis=0).T
            lse = jnp.concatenate(l_t, axis=0).T
            for rows, sl in b["out"]:
                o_scr[b["pi"], rows, :] = o[sl]
                l_scr[b["pi"], rows, :] = lse[sl]

    for g in range(0, len(blocks), ATT_REGION):
        region = blocks[g:g + ATT_REGION]
        region_finish(region, region_scores(region))

    for r in range(dm):
        tok = pl.ds(r, Mm, stride=dm)
        cls = slice(r * Mm, (r + 1) * Mm)
        os = (o_scr[0, tok, :], o_scr[1, cls, :], o_scr[2, cls, :])
        ls = (l_scr[0, tok, :], l_scr[1, cls, :], l_scr[2, cls, :])
        mx = jnp.maximum(jnp.maximum(ls[0], ls[1]), ls[2])
        ws = [jnp.exp2(l - mx) for l in ls]
        outf_scr[tok, :] = (ws[0] * os[0] + ws[1] * os[1] + ws[2] * os[2]) * (1.0 / (ws[0] + ws[1] + ws[2]))
    out_ref[...] = outf_scr[...].astype(out_ref.dtype)


def _attn(q, k, v, bias):
    B, npair, S, _ = q.shape
    blk = pl.BlockSpec((None, None, S, LANES), lambda b, j: (b, j, 0, 0))
    npat = len(ATTN_PATTERNS)
    return pl.pallas_call(
        _attn_kernel,
        grid=(B, npair),
        in_specs=[blk, blk, blk, _const_spec(bias.shape)],
        out_specs=pl.BlockSpec((None, S, LANES), lambda b, j: (b, 0, j)),
        out_shape=jax.ShapeDtypeStruct((B, S, npair * LANES), BF16),
        scratch_shapes=[pltpu.VMEM((npat, S, LANES), BF16), pltpu.VMEM((npat, S, LANES), BF16),
                        pltpu.VMEM((npat, LANES, S), BF16), pltpu.VMEM((npat, LANES, S), BF16),
                        pltpu.VMEM((npat, S, LANES), F32), pltpu.VMEM((npat, S, LANES), F32),
                        pltpu.VMEM((3, S, LANES), F32), pltpu.VMEM((S, LANES), F32)],
        compiler_params=pltpu.CompilerParams(dimension_semantics=("arbitrary", "arbitrary"),
                                             vmem_limit_bytes=VMEM_LIMIT),
        name="dilated_attn",
    )(q, k, v, bias)


def _split2(x):
    hi = x.astype(BF16)
    return hi, (x - hi.astype(F32)).astype(BF16)


def _log_sigmoid(x):
    return jnp.minimum(x, 0.0) - jnp.log1p(jnp.exp(-jnp.abs(x)))


def _mlstm_kernel(bias_ref, uc_ref, vm_ref, og_ref, grow_ref, wq_ref, wk_ref, gn_ref,
                  out_ref, q_scr, kt_scr, va_scr, hs_scr, lf_scr, b_scr, i_scr, w_scr, mp_scr, dc_scr,
                  e_scr, a_scr, f_scr, p_scr, nd_scr, cp_scr):
    S, dh = uc_ref.shape
    L = MLSTM_CHUNK
    nc = S // L
    hd = pl.program_id(1)

    uc = uc_ref[...]
    q_scr[...] = jnp.dot(uc, wq_ref[...], preferred_element_type=F32).astype(BF16)
    k = jnp.dot(uc, wk_ref[...], preferred_element_type=F32) * (dh ** -0.5)
    for c in range(nc):
        kt_scr[c] = k[c * L:(c + 1) * L].T.astype(BF16)
    va_scr[:, :dh] = vm_ref[...]
    va_scr[:, dh:] = jnp.ones((S, dh), BF16)

    ri = lax.broadcasted_iota(jnp.int32, (L, L), 0)
    ci = lax.broadcasted_iota(jnp.int32, (L, L), 1)
    lower = ri >= ci
    upper = ri <= ci

    for di in range(2):
        b_i = bias_ref[0, di, hd]
        b_f = bias_ref[1, di, hd]
        irow = (grow_ref[di] + b_i) * LOG2E
        lf = _log_sigmoid(grow_ref[2 + di] + b_f) * LOG2E
        tri = (upper if di == 0 else lower).astype(BF16)
        brow = sum(jnp.dot(part, tri, preferred_element_type=F32) for part in _split2(lf))
        g = jnp.sum(lf, axis=1, keepdims=True)
        logw = g - brow + irow
        gb = jnp.broadcast_to(g, (nc, dh))
        mxb = jnp.broadcast_to(jnp.max(logw, axis=1, keepdims=True), (nc, dh))
        m = jnp.zeros((1, dh), F32)
        mprev, mnew = [None] * nc, [None] * nc
        for step in range(nc):
            c = step if di == 0 else nc - 1 - step
            mprev[c] = m
            m = jnp.maximum(gb[c:c + 1] + m, mxb[c:c + 1])
            mnew[c] = m
        mprev = jnp.concatenate(mprev, axis=0)
        mnew = jnp.concatenate(mnew, axis=0)
        lf_scr[di] = lf
        b_scr[di] = brow
        i_scr[di] = irow
        w_scr[di] = jnp.exp2(logw - mnew)
        mp_scr[di] = mprev
        dc_scr[di] = jnp.exp2(gb + mprev - mnew)

    ones_rhs = jnp.ones((2 * L, dh), BF16)
    chunk = lambda c: slice(c * L, (c + 1) * L)

    for c in range(nc):
        for di in range(2):
            lfr, brow, irow, mprev = (ref[di, c:c + 1, :] for ref in (lf_scr, b_scr, i_scr, mp_scr))
            causal = lower if di == 0 else upper
            hi, lo = _split2(jnp.where(causal, lfr, 0.0))
            bcol = jnp.dot(jnp.concatenate([hi, lo], axis=1), ones_rhs, preferred_element_type=F32)
            dmat = jnp.where(causal, bcol - brow + irow, _NEG)
            m_inter = bcol + mprev
            m_t = jnp.maximum(m_inter, jnp.max(dmat, axis=1, keepdims=True))
            e_scr[di, c] = jnp.exp2((dmat - m_t).astype(BF16))
            a_scr[di, c] = jnp.exp2(m_inter - m_t)
            f_scr[di, c] = jnp.exp2(-m_t)

    for c in range(nc):
        qk = jnp.dot(q_scr[chunk(c), :], kt_scr[c], preferred_element_type=F32).astype(BF16)
        for di in range(2):
            p_scr[di, c] = qk * e_scr[di, c]

    states = [jnp.zeros((dh, 2 * dh), F32)] * 2
    for step in range(nc):
        for di in range(2):
            c = step if di == 0 else nc - 1 - step
            va = va_scr[chunk(c), :]
            nd_scr[di, c] = jnp.dot(p_scr[di, c], va, preferred_element_type=F32)
            cp_scr[di, c] = states[di].astype(BF16)
            ktw = (kt_scr[c].astype(F32) * w_scr[di, c:c + 1, :]).astype(BF16)
            decay = dc_scr[di, c:c + 1, :]
            states[di] = jnp.concatenate([decay, decay], axis=1) * states[di] \
                + jnp.dot(ktw, va, preferred_element_type=F32)

    for c in range(nc):
        qc = q_scr[chunk(c), :]
        hsum = None
        for di in range(2):
            a_in = a_scr[di, c]
            nd = jnp.concatenate([a_in, a_in], axis=1) * jnp.dot(qc, cp_scr[di, c], preferred_element_type=F32) \
                + nd_scr[di, c]
            hval = nd[:, :dh] * (1.0 / jnp.maximum(jnp.abs(nd[:, dh:]), f_scr[di, c]))
            hsum = hval if hsum is None else hsum + hval
        hs_scr[chunk(c), :] = hsum

    hs = hs_scr[...]
    mu = jnp.sum(hs, axis=1, keepdims=True) * (1.0 / dh)
    cen = hs - mu
    var = jnp.sum(cen * cen, axis=1, keepdims=True) * (1.0 / dh)
    hn = cen * lax.rsqrt(var + EPS)
    out_ref[...] = (hn * gn_ref[...] * og_ref[...]).astype(out_ref.dtype)


def _mlstm(gate_bias, uc, vm, og, grow, wq, wk, gn):
    B, S, MW = uc.shape
    H, dh = MLSTM_HEADS, MLSTM_HEAD_DIM
    L = MLSTM_CHUNK
    assert L == dh == LANES
    nc = S // L
    tok = pl.BlockSpec((None, S, dh), lambda b, h: (b, 0, h))
    return pl.pallas_call(
        _mlstm_kernel,
        grid=(B, H),
        in_specs=[pl.BlockSpec(memory_space=pltpu.SMEM), tok, tok, tok,
                  pl.BlockSpec((None, None, 4, nc, L), lambda b, h: (b, h, 0, 0, 0)),
                  pl.BlockSpec((None, dh, dh), lambda b, h: (h, 0, 0)),
                  pl.BlockSpec((None, dh, dh), lambda b, h: (h, 0, 0)),
                  pl.BlockSpec((1, dh), lambda b, h: (0, h))],
        out_specs=tok,
        out_shape=jax.ShapeDtypeStruct((B, S, MW), BF16),
        scratch_shapes=[pltpu.VMEM((S, dh), BF16), pltpu.VMEM((nc, dh, L), BF16), pltpu.VMEM((S, 2 * dh), BF16),
                        pltpu.VMEM((S, dh), F32)] + [pltpu.VMEM((2, nc, L), F32)] * 6
        + [pltpu.VMEM((2, nc, L, L), BF16), pltpu.VMEM((2, nc, L, dh), F32), pltpu.VMEM((2, nc, L, dh), F32),
           pltpu.VMEM((2, nc, L, L), BF16), pltpu.VMEM((2, nc, L, 2 * dh), F32),
           pltpu.VMEM((2, nc, dh, 2 * dh), BF16)],
        compiler_params=pltpu.CompilerParams(dimension_semantics=("arbitrary", "arbitrary"),
                                             vmem_limit_bytes=VMEM_LIMIT),
        name="mlstm",
    )(gate_bias, uc, vm, og, grow, wq, wk, gn)


FFN_CHUNK = 256


def _ffn_kernel(x_ref, xp_ref, xn_ref, a_ref, ap_ref, an_ref, m_ref, mp_ref, mn_ref, p_ref,
                wo_ref, g2_ref, wg_ref, wu_ref, wd_ref, cw_ref, cb_ref, gp_ref, wpg_ref, bpg_ref, wpp_ref, gf_ref,
                out_ref, mix_ref, hcat_ref, act_ref, *, tiles_per_seq):
    tm, aw = a_ref.shape
    ck = FFN_CHUNK
    nck = wg_ref.shape[1] // ck
    i = pl.program_id(0)
    first = (i % tiles_per_seq) == 0
    last = (i % tiles_per_seq) == tiles_per_seq - 1

    for lo, aa, mm in ((0, ap_ref, mp_ref), (HALO, a_ref, m_ref), (HALO + tm, an_ref, mn_ref)):
        mix_ref[lo:lo + aa.shape[0], :aw] = aa[...]
        mix_ref[lo:lo + aa.shape[0], aw:] = mm[...]
    y = jnp.dot(mix_ref[...], wo_ref[...], preferred_element_type=F32)
    x1 = x_ref[...] + y[HALO:HALO + tm]
    out_ref[...] = x1
    g2 = g2_ref[...]
    hcat_ref[HALO:HALO + tm, :] = _rms(x1, g2).astype(BF16)
    hp = _rms(xp_ref[...] + y[:HALO], g2).astype(BF16)
    hn = _rms(xn_ref[...] + y[HALO + tm:], g2).astype(BF16)
    hcat_ref[0:HALO, :] = jnp.where(first, jnp.zeros_like(hp), hp)
    hcat_ref[HALO + tm:, :] = jnp.where(last, jnp.zeros_like(hn), hn)

    rows = tm + 2 * HALO
    for c in range(nck):
        cols = slice(c * ck, (c + 1) * ck)
        g = jnp.dot(hcat_ref[...], wg_ref[:, cols], preferred_element_type=F32)
        up = jnp.dot(hcat_ref[HALO:HALO + tm, :], wu_ref[:, cols], preferred_element_type=F32)
        cw = cw_ref[:, cols]
        a = pltpu.roll(g, 1, 0)[HALO:HALO + tm] * cw[0:1] + g[HALO:HALO + tm] * cw[1:2] \
            + pltpu.roll(g, rows - 1, 0)[HALO:HALO + tm] * cw[2:3] + cb_ref[:, cols]
        act_ref[:, cols] = (0.5 * a * (1.0 + lax.erf(a * (2.0 ** -0.5))) * up).astype(BF16)
    x2 = out_ref[...] + jnp.dot(act_ref[...], wd_ref[...], preferred_element_type=F32)
    ms = jnp.sum(x2 * x2, axis=-1, keepdims=True) * (1.0 / x2.shape[-1])
    zg = jnp.dot((x2 * gp_ref[...]).astype(BF16), wpg_ref[...], preferred_element_type=F32)
    gate = jax.nn.sigmoid(zg * lax.rsqrt(ms + EPS) + bpg_ref[...])
    pp = jnp.dot(p_ref[...].astype(BF16), wpp_ref[...], preferred_element_type=F32)
    out_ref[...] = _rms(x2 + pp * gate, gf_ref[...])


def _ffn(x2d, attn, mem, p2, wo, g2, wg, wu, wd, cw, cb, gp, wpg, bpg, wpp, gf, tm, S):
    T, D = x2d.shape
    row = lambda w: pl.BlockSpec((tm, w), lambda i: (i, 0))
    hb = tm // HALO
    nhb = T // HALO
    prev = lambda w: pl.BlockSpec((HALO, w), lambda i: (jnp.maximum(i * hb - 1, 0), 0))
    nxt = lambda w: pl.BlockSpec((HALO, w), lambda i: (jnp.minimum((i + 1) * hb, nhb - 1), 0))
    aw, mw = attn.shape[1], mem.shape[1]
    consts = (wo, g2, wg, wu, wd, cw, cb, gp, wpg, bpg, wpp, gf)
    return pl.pallas_call(
        functools.partial(_ffn_kernel, tiles_per_seq=S // tm),
        grid=(T // tm,),
        in_specs=[row(D), prev(D), nxt(D), row(aw), prev(aw), nxt(aw), row(mw), prev(mw), nxt(mw),
                  row(p2.shape[1])] + [_const_spec(c.shape) for c in consts],
        out_specs=row(D),
        out_shape=jax.ShapeDtypeStruct((T, D), F32),
        scratch_shapes=[pltpu.VMEM((tm + 2 * HALO, aw + mw), BF16), pltpu.VMEM((tm + 2 * HALO, D), BF16),
                        pltpu.VMEM((tm, wg.shape[1]), BF16)],
        compiler_params=pltpu.CompilerParams(dimension_semantics=("arbitrary",), vmem_limit_bytes=VMEM_LIMIT),
        name="outproj_ffn_ple",
    )(x2d, x2d, x2d, attn, attn, attn, mem, mem, mem, p2, *consts)


def kernel(x, p, positions, ln_mix_g, w_in, mlstm_conv_w, mlstm_conv_b, w_mq, w_mk, b_igate, b_fgate,
           mlstm_gn_g, w_out, ln_ffn_g, w_ffn_gate, ffn_conv_w, ffn_conv_b, w_ffn_up, w_ffn_down,
           ln_ple_g, w_ple_gate, b_ple_gate, w_ple_proj, ln_final_g):
    B, S, D = x.shape
    depth = w_in.shape[0]
    T = B * S
    H, dh = MLSTM_HEADS, MLSTM_HEAD_DIM
    aw = D // 2
    mw = D - aw
    tm = ROW_TILE
    tm_in = INPROJ_TILE
    L = MLSTM_CHUNK
    nc = S // L

    bias = _attn_bias()
    pos3 = positions.reshape(T // tm_in, 1, tm_in)

    assert depth == 1, "the final norm is fused into the layer's last kernel"
    xf = x.reshape(T, D)
    for i in range(depth):
        nproj = 3 * aw + 3 * mw
        ng = w_in.shape[-1] - nproj
        w_gates = jnp.pad(w_in[i][:, nproj:], ((0, 0), (0, LANES - ng))).astype(BF16)
        q, k, v, uc, vm, og, gates_t = _inproj(xf, pos3, ln_mix_g[i][None], w_in[i].astype(BF16), w_gates, mlstm_conv_w[i],
                                               mlstm_conv_b[i][None], ng, aw, mw, tm_in, S)

        attn = _attn(q, k, v, bias)

        grow = gates_t.reshape(4, H, B, nc, L).transpose(2, 1, 0, 3, 4)
        gate_bias = jnp.stack([b_igate[i], b_fgate[i]]).astype(F32)
        mem = _mlstm(gate_bias, uc.reshape(B, S, mw), vm.reshape(B, S, mw), og.reshape(B, S, mw), grow,
                     w_mq[i].astype(BF16), w_mk[i].astype(BF16), mlstm_gn_g[i][None])

        assert w_ffn_gate.shape[-1] % FFN_CHUNK == 0
        xf = _ffn(xf, attn.reshape(T, aw), mem.reshape(T, mw), p[i].reshape(T, -1),
                  w_out[i].astype(BF16), ln_ffn_g[i][None],
                  w_ffn_gate[i].astype(BF16), w_ffn_up[i].astype(BF16), w_ffn_down[i].astype(BF16),
                  ffn_conv_w[i], ffn_conv_b[i][None], ln_ple_g[i][None],
                  w_ple_gate[i].astype(BF16), b_ple_gate[i][None], w_ple_proj[i].astype(BF16), ln_final_g[None], tm, S)
    return xf.reshape(B, S, D)
```
